```python
import jax
import jax.numpy as jnp
from jax import lax
import numpy as np

D_MODEL = 1024
BATCH = 16
SEQ = 256
DEPTH = 2
DEC_BATCH = 4
DEC_SEQ = 4096
PAST_LEN = 512

GRID_W = 64
EPS = 1e-6
GN_EPS = 64e-5
CHUNK = 64
Q_BLOCK = 128
ROPE_THETA = 10000.0

RW_HEADS = 8
RW_DIM = 64
RW_W = RW_HEADS * RW_DIM
RW_LORA_W = 64
RW_LORA_A = 64
RW_SLAB = 3 * RW_W + 2 * RW_LORA_W + RW_LORA_A
RW_DECAY_SCALE = 0.6065306597126334

AT_HEADS = 8
AT_KV_HEADS = 2
AT_DIM = 64
AT_W = AT_HEADS * AT_DIM
AT_KV_W = AT_KV_HEADS * AT_DIM
ATT_SCALE = AT_DIM ** -0.5

GLA_HEADS = 4
GLA_DK = 64
GLA_DV = 128
GLA_QK_W = GLA_HEADS * GLA_DK
GLA_W = GLA_HEADS * GLA_DV
GLA_LORA = 16
GLA_GATE_NORM = 16.0
GLA_SCALE = GLA_DK ** -0.5

GDN_HEADS = 4
GDN_DK = 128
GDN_DV = 128
GDN_QK_W = GDN_HEADS * GDN_DK
GDN_W = GDN_HEADS * GDN_DV
GDN_CONV = 3
GDN_SCALE = GDN_DK ** -0.5

L0_SPLITS = (RW_SLAB, RW_W, AT_W, AT_KV_W, AT_KV_W, AT_W)
L0_IN = sum(L0_SPLITS)
L0_OUT = RW_W + AT_W
L1_SPLITS = (GLA_QK_W, GLA_QK_W, GLA_W, GLA_LORA, GLA_LORA, GLA_W,
             2 * GDN_QK_W + GDN_W, GDN_HEADS, GDN_HEADS, GDN_HEADS, GDN_W)
L1_IN = sum(L1_SPLITS)
L1_OUT = GLA_W + GDN_W

kernel_name = 'hybrid_diffusion_rwkv7_gqa_gla_gdn_step'


def rmsnorm(x, g):
    xf = x.astype(jnp.float32)
    y = xf * lax.rsqrt(jnp.mean(xf * xf, axis=-1, keepdims=True) + EPS)
    return (y * g.astype(jnp.float32)).astype(x.dtype)


def l2norm(x):
    xf = x.astype(jnp.float32)
    return (xf * lax.rsqrt(jnp.sum(xf * xf, axis=-1, keepdims=True) + EPS)).astype(x.dtype)


def head_groupnorm(o, g, b):
    of = o.astype(jnp.float32)
    mu = jnp.mean(of, axis=-1, keepdims=True)
    var = jnp.mean(jnp.square(of - mu), axis=-1, keepdims=True)
    y = (of - mu) * lax.rsqrt(var + GN_EPS) * g.astype(jnp.float32) + b.astype(jnp.float32)
    return y.astype(o.dtype)


def split_cols(p, sizes):
    idx = [int(s) for s in np.cumsum(sizes)[:-1]]
    return jnp.split(p, idx, axis=-1)


def token_shift(p):
    prev = jnp.pad(p[:, :-1], ((0, 0), (1, 0), (0, 0)))
    nxt = jnp.pad(p[:, 1:], ((0, 0), (0, 1), (0, 0)))
    return 0.5 * (prev + nxt)


def centred_dwconv(x, w):
    k = w.shape[0]
    return lax.conv_general_dilated(
        x, w[:, None, :].astype(x.dtype), window_strides=(1,),
        padding=[(k // 2, k // 2)], dimension_numbers=('NWC', 'WIO', 'NWC'),
        feature_group_count=x.shape[-1])


def rope_2d(x):
    t = x.shape[1]
    rows = t // GRID_W
    row_id = jnp.repeat(jnp.arange(rows, dtype=jnp.float32), GRID_W)
    col_id = jnp.tile(jnp.arange(GRID_W, dtype=jnp.float32), rows)
    half = x.shape[-1] // 2
    nf = half // 2
    inv = ROPE_THETA ** (-jnp.arange(nf, dtype=jnp.float32) / nf)

    def rot(xh, pos):
        ang = pos[:, None] * inv[None, :]
        cos = jnp.cos(ang)[None, :, None, :].astype(x.dtype)
        sin = jnp.sin(ang)[None, :, None, :].astype(x.dtype)
        x1, x2 = xh[..., :nf], xh[..., nf:]
        return jnp.concatenate([x1 * cos - x2 * sin, x2 * cos + x1 * sin], axis=-1)

    return jnp.concatenate([rot(x[..., :half], row_id), rot(x[..., half:], col_id)], axis=-1)


def block_attention(q, k, v):
    b, tq, hq, dh = q.shape
    g = hq // AT_KV_HEADS
    qb = jnp.moveaxis(q.reshape(b, tq // Q_BLOCK, Q_BLOCK, AT_KV_HEADS, g, dh), 1, 0)

    def one_block(qblk):
        s = jnp.einsum('bqhgd,bkhd->bhgqk', qblk, k).astype(jnp.float32) * ATT_SCALE
        pr = jax.nn.softmax(s, axis=-1).astype(v.dtype)
        return jnp.einsum('bhgqk,bkhd->bqhgd', pr, v)

    o = lax.map(one_block, qb)
    return jnp.moveaxis(o, 0, 1).reshape(b, tq, hq, dh)


def rwkv_scan(s0, r, w, k, v, kk, a, reverse):
    def step(s, inp):
        r_t, w_t, k_t, v_t, kk_t, a_t = inp
        sa = jnp.einsum('bhvk,bhk->bhv', s, -kk_t)
        s = (s * w_t[:, :, None, :] + sa[..., None] * (kk_t * a_t)[:, :, None, :]
             + v_t[..., None] * k_t[:, :, None, :])
        return s, jnp.einsum('bhvk,bhk->bhv', s, r_t)

    xs = tuple(jnp.moveaxis(t, 1, 0) for t in (r, w, k, v, kk, a))
    s_fin, o = lax.scan(step, s0, xs, reverse=reverse)
    return s_fin, jnp.moveaxis(o, 0, 1)


def gla_chunked(q, k, v, logw, s0):
    b, t, h, dk = q.shape
    dt = q.dtype
    n = t // CHUNK

    def chunks(x):
        return jnp.moveaxis(x.reshape(b, n, CHUNK, h, x.shape[-1]), 1, 0)

    mask = jnp.tril(jnp.ones((CHUNK, CHUNK), dtype=bool))[None, :, :, None, None]

    def step(s, inp):
        q_c, k_c, v_c, g_c = inp
        cum = jnp.cumsum(g_c.astype(jnp.float32), axis=1)
        diff = cum[:, :, None] - cum[:, None, :]
        dec = jnp.where(mask, jnp.exp(jnp.minimum(diff, 0.0)), 0.0).astype(dt)
        att = jnp.einsum('bthd,bshd,btshd->bhts', q_c, k_c, dec)
        o = (jnp.einsum('bhts,bshv->bthv', att, v_c)
             + jnp.einsum('bthd,bhdv->bthv', q_c * jnp.exp(cum).astype(dt), s))
        last = cum[:, -1]
        k_dec = k_c * jnp.exp(last[:, None] - cum).astype(dt)
        s = s * jnp.exp(last).astype(dt)[..., None] + jnp.einsum('bshd,bshv->bhdv', k_dec, v_c)
        return s, o

    s_fin, o = lax.scan(step, s0, (chunks(q), chunks(k), chunks(v), chunks(logw)))
    return s_fin, jnp.moveaxis(o, 0, 1).reshape(b, t, h, v.shape[-1])


def gdn_chunked(q, k, v, beta, loga, s0):
    b, t, h, dk = q.shape
    dv = v.shape[-1]
    dt = q.dtype
    n = t // CHUNK

    def chunks(x):
        return x.reshape(b, n, CHUNK, h, x.shape[-1]).transpose(1, 0, 3, 2, 4)

    qc, kc, vc = chunks(q), chunks(k), chunks(v)
    bc = chunks(beta[..., None])[..., 0].astype(jnp.float32)
    g = jnp.cumsum(chunks(loga[..., None])[..., 0].astype(jnp.float32), axis=-1)
    dec = jnp.exp(jnp.minimum(g[..., :, None] - g[..., None, :], 0.0))
    strict = jnp.tril(jnp.ones((CHUNK, CHUNK), dtype=bool), -1)
    incl = jnp.tril(jnp.ones((CHUNK, CHUNK), dtype=bool))
    kk = jnp.einsum('nbhtd,nbhsd->nbhts', kc, kc).astype(jnp.float32)
    lhs = jnp.eye(CHUNK, dtype=jnp.float32) + jnp.where(strict, dec * kk * bc[..., None], 0.0)
    gam = jnp.exp(g)
    w_mat = lax.linalg.triangular_solve(
        lhs, (bc * gam)[..., None] * kc.astype(jnp.float32),
        left_side=True, lower=True, unit_diagonal=True).astype(dt)
    u_mat = lax.linalg.triangular_solve(
        lhs, bc[..., None] * vc.astype(jnp.float32),
        left_side=True, lower=True, unit_diagonal=True).astype(dt)
    aqk = jnp.where(incl, dec * jnp.einsum('nbhtd,nbhsd->nbhts', qc, kc).astype(jnp.float32),
                    0.0).astype(dt)
    q_g = qc * gam[..., None].astype(dt)
    k_dec = kc * jnp.exp(g[..., -1:] - g)[..., None].astype(dt)
    g_last = jnp.exp(g[..., -1]).astype(dt)

    def step(s, inp):
        w_c, u_c, a_c, q_c, k_c, gl = inp
        u = u_c - jnp.einsum('bhtd,bhdv->bhtv', w_c, s)
        o = jnp.einsum('bhtd,bhdv->bhtv', q_c, s) + jnp.einsum('bhts,bhsv->bhtv', a_c, u)
        s = s * gl[..., None, None] + jnp.einsum('bhsd,bhsv->bhdv', k_c, u)
        return s, o

    s_fin, o = lax.scan(step, s0, (w_mat, u_mat, aqk, q_g, k_dec, g_last))
    return s_fin, o.transpose(1, 0, 3, 2, 4).reshape(b, t, h, dv)


def mix_even(h, p, ctx):
    b, t, _ = h.shape
    proj = h @ p['w_in']
    slab, rw_gate, q, k, v, at_gate = split_cols(proj, L0_SPLITS)
    slab = slab + p['rw_mu'] * (token_shift(slab) - slab)
    r, kr, vr, wf_in, wb_in, a_in = split_cols(
        slab, (RW_W, RW_W, RW_W, RW_LORA_W, RW_LORA_W, RW_LORA_A))

    def hd(x):
        return x.reshape(b, t, RW_HEADS, RW_DIM)

    w_f = hd(jnp.exp(-RW_DECAY_SCALE * jax.nn.sigmoid(p['rw_w0_f'] + jnp.tanh(wf_in) @ p['rw_w2_f'])))
    w_b = hd(jnp.exp(-RW_DECAY_SCALE * jax.nn.sigmoid(p['rw_w0_b'] + jnp.tanh(wb_in) @ p['rw_w2_b'])))
    a = hd(jax.nn.sigmoid(p['rw_a0'] + a_in @ p['rw_a2']))
    r, kr, vr = hd(r), hd(kr), hd(vr)
    kk = l2norm(kr * p['rw_k_k'])
    kr = kr * (1.0 + (a - 1.0) * p['rw_k_a'])
    if ctx is None:
        s0f = jnp.zeros((b, RW_HEADS, RW_DIM, RW_DIM), h.dtype)
        s0b = jnp.zeros((b, RW_HEADS, RW_DIM, RW_DIM), h.dtype)
    else:
        s0f, s0b, k_ctx, v_ctx = ctx
    s_f, o_f = rwkv_scan(s0f, r, w_f, kr, vr, kk, a, False)
    s_b, o_b = rwkv_scan(s0b, r, w_b, kr, vr, kk, a, True)
    o_rw = (head_groupnorm(o_f + o_b, p['rw_gn_g'], p['rw_gn_b'])
            + jnp.sum(r * kr * p['rw_r_k'], axis=-1, keepdims=True) * vr)
    o_rw = o_rw.reshape(b, t, RW_W) * jax.nn.silu(rw_gate)

    q = rmsnorm(q.reshape(b, t, AT_HEADS, AT_DIM), p['at_gq'])
    k = rmsnorm(k.reshape(b, t, AT_KV_HEADS, AT_DIM), p['at_gk'])
    v = v.reshape(b, t, AT_KV_HEADS, AT_DIM)
    if ctx is None:
        o_at = block_attention(q, k, v)
        new = (s_f, s_b, k, v)
    else:
        k_all = jnp.concatenate([k_ctx, rope_2d(k)], axis=1)
        v_all = jnp.concatenate([v_ctx, v], axis=1)
        o_at = block_attention(rope_2d(q), k_all, v_all)
        new = None
    o_at = o_at.reshape(b, t, AT_W) * jax.nn.silu(at_gate)
    return jnp.concatenate([o_rw, o_at], axis=-1) @ p['w_out'], new


def mix_odd(h, p, ctx):
    b, t, _ = h.shape
    proj = h @ p['w_in']
    gq, gk, gv, gf_in, gb_in, gla_gate, d_qkv, af_in, ab_in, beta_in, gdn_gate = split_cols(proj, L1_SPLITS)
    q = gq.reshape(b, t, GLA_HEADS, GLA_DK) * GLA_SCALE
    k = gk.reshape(b, t, GLA_HEADS, GLA_DK)
    v = gv.reshape(b, t, GLA_HEADS, GLA_DV)
    logw_f = (jax.nn.log_sigmoid(gf_in @ p['gla_g2_f'] + p['gla_gb_f']) / GLA_GATE_NORM).reshape(b, t, GLA_HEADS, GLA_DK)
    logw_b = (jax.nn.log_sigmoid(gb_in @ p['gla_g2_b'] + p['gla_gb_b']) / GLA_GATE_NORM).reshape(b, t, GLA_HEADS, GLA_DK)

    dq, dk, dv = split_cols(jax.nn.silu(centred_dwconv(d_qkv, p['gdn_conv'])), (GDN_QK_W, GDN_QK_W, GDN_W))
    dq = l2norm(dq.reshape(b, t, GDN_HEADS, GDN_DK)) * GDN_SCALE
    dk = l2norm(dk.reshape(b, t, GDN_HEADS, GDN_DK))
    dv = dv.reshape(b, t, GDN_HEADS, GDN_DV)
    beta = jax.nn.sigmoid(beta_in)
    loga_f = -jnp.exp(p['gdn_A_log_f']) * jax.nn.softplus(af_in + p['gdn_dt_bias_f'])
    loga_b = -jnp.exp(p['gdn_A_log_b']) * jax.nn.softplus(ab_in + p['gdn_dt_bias_b'])

    if ctx is None:
        s_gla_f = jnp.zeros((b, GLA_HEADS, GLA_DK, GLA_DV), h.dtype)
        s_gla_b = jnp.zeros((b, GLA_HEADS, GLA_DK, GLA_DV), h.dtype)
        s_gdn_f = jnp.zeros((b, GDN_HEADS, GDN_DK, GDN_DV), h.dtype)
        s_gdn_b = jnp.zeros((b, GDN_HEADS, GDN_DK, GDN_DV), h.dtype)
    else:
        s_gla_f, s_gla_b, s_gdn_f, s_gdn_b = ctx

    def flip(x):
        return jnp.flip(x, axis=1)

    sgf, o_gf = gla_chunked(q, k, v, logw_f, s_gla_f)
    sgb, o_gb = gla_chunked(flip(q), flip(k), flip(v), flip(logw_b), s_gla_b)
    o_gla = rmsnorm(o_gf + flip(o_gb), p['gla_norm_g']).reshape(b, t, GLA_W) * jax.nn.silu(gla_gate)
    sdf, o_df = gdn_chunked(dq, dk, dv, beta, loga_f, s_gdn_f)
    sdb, o_db = gdn_chunked(flip(dq), flip(dk), flip(dv), flip(beta), flip(loga_b), s_gdn_b)
    o_gdn = rmsnorm(o_df + flip(o_db), p['gdn_norm_g']).reshape(b, t, GDN_W) * jax.nn.silu(gdn_gate)
    out = jnp.concatenate([o_gla, o_gdn], axis=-1) @ p['w_out']
    new = (sgf, sgb, sdf, sdb) if ctx is None else None
    return out, new


def sublayer(x, cvec, p, mix_fn, ctx):
    shift, scale, gate = jnp.split(jax.nn.silu(cvec) @ p['mod_w'] + p['mod_b'], 3, axis=-1)
    h = rmsnorm(x, p['g_pre']) * (1.0 + scale) + shift
    out, new = mix_fn(h, p, ctx)
    return x + gate * rmsnorm(out, p['g_post']), new


def setup_inputs(seed: int = 0) -> dict:
    key = jax.random.key(seed)
    keys = jax.random.split(key, 64)
    counter = [0]

    def nxt():
        kk = keys[counter[0]]
        counter[0] += 1
        return kk

    def nrm(shape, scale=1.0):
        return scale * jax.random.normal(nxt(), shape, dtype=jnp.float32)

    def gain(shape):
        return 1.0 + nrm(shape, 0.01)

    def unif(shape, lo, hi):
        return jax.random.uniform(nxt(), shape, dtype=jnp.float32, minval=lo, maxval=hi)

    d = {}
    d['x_prompt'] = nrm((BATCH, SEQ, D_MODEL))
    d['x_sample'] = nrm((DEC_BATCH, DEC_SEQ, D_MODEL))
    d['state_l0_rwkv_fwd'] = nrm((DEC_BATCH, RW_HEADS, RW_DIM, RW_DIM), 0.1)
    d['state_l0_rwkv_bwd'] = nrm((DEC_BATCH, RW_HEADS, RW_DIM, RW_DIM), 0.1)
    d['cache_l0_k'] = nrm((DEC_BATCH, PAST_LEN, AT_KV_HEADS, AT_DIM))
    d['cache_l0_v'] = nrm((DEC_BATCH, PAST_LEN, AT_KV_HEADS, AT_DIM))
    d['state_l1_gla_fwd'] = nrm((DEC_BATCH, GLA_HEADS, GLA_DK, GLA_DV), 0.1)
    d['state_l1_gla_bwd'] = nrm((DEC_BATCH, GLA_HEADS, GLA_DK, GLA_DV), 0.1)
    d['state_l1_gdn_fwd'] = nrm((DEC_BATCH, GDN_HEADS, GDN_DK, GDN_DV), 0.1)
    d['state_l1_gdn_bwd'] = nrm((DEC_BATCH, GDN_HEADS, GDN_DK, GDN_DV), 0.1)
    d['c'] = nrm((DEC_BATCH, D_MODEL))
    d['c_ctx'] = nrm((D_MODEL,))
    d['l0_mod_w'] = nrm((D_MODEL, 3 * D_MODEL), 0.5 * D_MODEL ** -0.5)
    d['l0_mod_b'] = nrm((3 * D_MODEL,), 0.01)
    d['l0_g_pre'] = gain((D_MODEL,))
    d['l0_g_post'] = gain((D_MODEL,))
    d['l0_w_in'] = nrm((D_MODEL, L0_IN), D_MODEL ** -0.5)
    d['l0_w_out'] = nrm((L0_OUT, D_MODEL), L0_OUT ** -0.5)
    d['l0_rw_mu'] = unif((RW_SLAB,), 0.0, 1.0)
    d['l0_rw_w0_f'] = nrm((RW_W,), 0.5)
    d['l0_rw_w2_f'] = nrm((RW_LORA_W, RW_W), RW_LORA_W ** -0.5)
    d['l0_rw_w0_b'] = nrm((RW_W,), 0.5)
    d['l0_rw_w2_b'] = nrm((RW_LORA_W, RW_W), RW_LORA_W ** -0.5)
    d['l0_rw_a0'] = nrm((RW_W,), 0.5)
    d['l0_rw_a2'] = nrm((RW_LORA_A, RW_W), RW_LORA_A ** -0.5)
    d['l0_rw_k_k'] = 0.85 + nrm((RW_HEADS, RW_DIM), 0.05)
    d['l0_rw_k_a'] = 1.0 + nrm((RW_HEADS, RW_DIM), 0.05)
    d['l0_rw_r_k'] = nrm((RW_HEADS, RW_DIM), 0.1)
    d['l0_rw_gn_g'] = gain((RW_HEADS, RW_DIM))
    d['l0_rw_gn_b'] = nrm((RW_HEADS, RW_DIM), 0.01)
    d['l0_at_gq'] = gain((AT_DIM,))
    d['l0_at_gk'] = gain((AT_DIM,))
    d['l1_mod_w'] = nrm((D_MODEL, 3 * D_MODEL), 0.5 * D_MODEL ** -0.5)
    d['l1_mod_b'] = nrm((3 * D_MODEL,), 0.01)
    d['l1_g_pre'] = gain((D_MODEL,))
    d['l1_g_post'] = gain((D_MODEL,))
    d['l1_w_in'] = nrm((D_MODEL, L1_IN), D_MODEL ** -0.5)
    d['l1_w_out'] = nrm((L1_OUT, D_MODEL), L1_OUT ** -0.5)
    d['l1_gla_g2_f'] = nrm((GLA_LORA, GLA_QK_W), GLA_LORA ** -0.5)
    d['l1_gla_gb_f'] = nrm((GLA_QK_W,), 0.5)
    d['l1_gla_g2_b'] = nrm((GLA_LORA, GLA_QK_W), GLA_LORA ** -0.5)
    d['l1_gla_gb_b'] = nrm((GLA_QK_W,), 0.5)
    d['l1_gla_norm_g'] = gain((GLA_DV,))
    d['l1_gdn_conv'] = nrm((GDN_CONV, 2 * GDN_QK_W + GDN_W), GDN_CONV ** -0.5)
    d['l1_gdn_A_log_f'] = jnp.log(unif((GDN_HEADS,), 1.0, 8.0))
    d['l1_gdn_dt_bias_f'] = unif((GDN_HEADS,), -6.0, -2.5)
    d['l1_gdn_A_log_b'] = jnp.log(unif((GDN_HEADS,), 1.0, 8.0))
    d['l1_gdn_dt_bias_b'] = unif((GDN_HEADS,), -6.0, -2.5)
    d['l1_gdn_norm_g'] = gain((GDN_DV,))
    return d


def reference(x_prompt, x_sample, state_l0_rwkv_fwd, state_l0_rwkv_bwd, cache_l0_k, cache_l0_v,
              state_l1_gla_fwd, state_l1_gla_bwd, state_l1_gdn_fwd, state_l1_gdn_bwd, c, c_ctx,
              l0_mod_w, l0_mod_b, l0_g_pre, l0_g_post, l0_w_in, l0_w_out, l0_rw_mu,
              l0_rw_w0_f, l0_rw_w2_f, l0_rw_w0_b, l0_rw_w2_b, l0_rw_a0, l0_rw_a2,
              l0_rw_k_k, l0_rw_k_a, l0_rw_r_k, l0_rw_gn_g, l0_rw_gn_b, l0_at_gq, l0_at_gk,
              l1_mod_w, l1_mod_b, l1_g_pre, l1_g_post, l1_w_in, l1_w_out,
              l1_gla_g2_f, l1_gla_gb_f, l1_gla_g2_b, l1_gla_gb_b, l1_gla_norm_g,
              l1_gdn_conv, l1_gdn_A_log_f, l1_gdn_dt_bias_f, l1_gdn_A_log_b, l1_gdn_dt_bias_b,
              l1_gdn_norm_g):
    p0 = {'mod_w': l0_mod_w, 'mod_b': l0_mod_b, 'g_pre': l0_g_pre, 'g_post': l0_g_post,
          'w_in': l0_w_in, 'w_out': l0_w_out, 'rw_mu': l0_rw_mu,
          'rw_w0_f': l0_rw_w0_f, 'rw_w2_f': l0_rw_w2_f, 'rw_w0_b': l0_rw_w0_b, 'rw_w2_b': l0_rw_w2_b,
          'rw_a0': l0_rw_a0, 'rw_a2': l0_rw_a2, 'rw_k_k': l0_rw_k_k, 'rw_k_a': l0_rw_k_a,
          'rw_r_k': l0_rw_r_k, 'rw_gn_g': l0_rw_gn_g, 'rw_gn_b': l0_rw_gn_b,
          'at_gq': l0_at_gq, 'at_gk': l0_at_gk}
    p1 = {'mod_w': l1_mod_w, 'mod_b': l1_mod_b, 'g_pre': l1_g_pre, 'g_post': l1_g_post,
          'w_in': l1_w_in, 'w_out': l1_w_out,
          'gla_g2_f': l1_gla_g2_f, 'gla_gb_f': l1_gla_gb_f, 'gla_g2_b': l1_gla_g2_b,
          'gla_gb_b': l1_gla_gb_b, 'gla_norm_g': l1_gla_norm_g, 'gdn_conv': l1_gdn_conv,
          'gdn_A_log_f': l1_gdn_A_log_f, 'gdn_dt_bias_f': l1_gdn_dt_bias_f,
          'gdn_A_log_b': l1_gdn_A_log_b, 'gdn_dt_bias_b': l1_gdn_dt_bias_b,
          'gdn_norm_g': l1_gdn_norm_g}
    params = (p0, p1)
    mixers = (mix_even, mix_odd)
    caches = ((state_l0_rwkv_fwd, state_l0_rwkv_bwd, cache_l0_k, cache_l0_v),
              (state_l1_gla_fwd, state_l1_gla_bwd, state_l1_gdn_fwd, state_l1_gdn_bwd))

    x = x_prompt
    ctx_out = []
    for l in range(DEPTH):
        x, new = sublayer(x, c_ctx, params[l], mixers[l % 2], None)
        ctx_out.append(new)
    y_prompt = x

    x = x_sample
    c_lat = c[:, None, :]
    for l in range(DEPTH):
        x, _ = sublayer(x, c_lat, params[l], mixers[l % 2], caches[l])
    y_sample = x

    new_rw_f, new_rw_b, new_k, new_v = ctx_out[0]
    new_gla_f, new_gla_b, new_gdn_f, new_gdn_b = ctx_out[1]
    return (y_prompt, y_sample, new_rw_f, new_rw_b, new_k, new_v, new_gla_f, new_gla_b, new_gdn_f, new_gdn_b)
```

```python
import functools

import numpy as np
import jax
import jax.numpy as jnp
from jax import lax
from jax.experimental import pallas as pl
from jax.experimental.pallas import tpu as pltpu

F32 = jnp.float32
BF16 = jnp.bfloat16

EPS = 1e-6
GN_EPS = 64e-5
CHUNK = 64
GRID_W = 64
ROPE_THETA = 10000.0
RW_DECAY_SCALE = 0.6065306597126334
GLA_GATE_NORM = 16.0
HEAD64 = 64
HEAD128 = 128
LANES = 128
SUBLANES = 8
VMEM_LIMIT = 56 * 1024 * 1024


def _mm(a, b):
    return jnp.dot(a.astype(BF16), b.astype(BF16), preferred_element_type=F32)


def _mm_nt(a, b):
    return lax.dot_general(a.astype(BF16), b.astype(BF16), (((1,), (1,)), ((), ())),
                           preferred_element_type=F32)


def _mm_tn(a, b):
    return lax.dot_general(a.astype(BF16), b.astype(BF16), (((0,), (0,)), ((), ())),
                           preferred_element_type=F32)


def _split3(x):
    hi = x.astype(BF16)
    r1 = x - hi.astype(F32)
    mid = r1.astype(BF16)
    lo = (r1 - mid.astype(F32)).astype(BF16)
    return hi, mid, lo


def _mm_exact_l(mask_bf16, x):
    hi, mid, lo = _split3(x)
    d = functools.partial(jnp.dot, preferred_element_type=F32)
    return d(mask_bf16, hi) + d(mask_bf16, mid) + d(mask_bf16, lo)


def _mm_exact_r(x, mask_bf16):
    hi, mid, lo = _split3(x)
    d = functools.partial(jnp.dot, preferred_element_type=F32)
    return d(hi, mask_bf16) + d(mid, mask_bf16) + d(lo, mask_bf16)


def _mm_exact_nt(x, mask_bf16):
    hi, mid, lo = _split3(x)
    d = lambda a: lax.dot_general(a, mask_bf16, (((1,), (1,)), ((), ())), preferred_element_type=F32)
    return d(hi) + d(mid) + d(lo)


def _mm3(a, b):
    ah = a.astype(BF16)
    al = (a - ah.astype(F32)).astype(BF16)
    bh = b.astype(BF16)
    bl = (b - bh.astype(F32)).astype(BF16)
    d = functools.partial(jnp.dot, preferred_element_type=F32)
    return d(ah, bh) + d(ah, bl) + d(al, bh)


def _silu(x):
    return x * jax.nn.sigmoid(x)


def _softplus(x):
    return jnp.maximum(x, 0.0) + jnp.log(1.0 + jnp.exp(-jnp.abs(x)))


def _log_sigmoid(x):
    return jnp.minimum(x, 0.0) - jnp.log(1.0 + jnp.exp(-jnp.abs(x)))


def _order_masks(n, sgn, reps=1):
    row = lax.broadcasted_iota(jnp.int32, (reps * n, n), 0) & (n - 1)
    col = lax.broadcasted_iota(jnp.int32, (reps * n, n), 1)
    d = (row - col) * sgn
    return d > 0, d >= 0


def _tri_inv(nmat, mm):
    n = nmat.shape[0]
    row = lax.broadcasted_iota(jnp.int32, (n, n), 0)
    col = lax.broadcasted_iota(jnp.int32, (n, n), 1)
    x = row ^ col
    t = jnp.where(row == col, 1.0, 0.0).astype(F32) - jnp.where(x == 1, nmat, 0.0)
    s = 1
    while (2 << s) <= n:
        c = jnp.where((x >> s) == 1, nmat, 0.0)
        t = t - mm(mm(t, c), t)
        s += 1
    return t


def _lane_group_masks(width, group, count):
    lane = lax.broadcasted_iota(jnp.int32, (1, width), 1)
    return [((lane >= g * group) & (lane < (g + 1) * group)) for g in range(count)]


def _stack_heads(x, masks):
    return jnp.concatenate([jnp.where(m, x, 0.0) for m in masks], axis=0)


def _mod_kernel(c_ref, w_ref, b_ref, o_ref):
    s = _silu(c_ref[...])
    o_ref[...] = _mm3(s, w_ref[...]) + b_ref[...]


def _modulation(cvec, mod_w, mod_b):
    rows, d = cvec.shape
    n = mod_w.shape[1]
    tn = 1024
    return pl.pallas_call(
        _mod_kernel,
        out_shape=jax.ShapeDtypeStruct((rows, n), F32),
        grid=(n // tn,),
        in_specs=[pl.BlockSpec((rows, d), lambda j: (0, 0)),
                  pl.BlockSpec((d, tn), lambda j: (0, j)),
                  pl.BlockSpec((1, tn), lambda j: (0, j))],
        out_specs=pl.BlockSpec((rows, tn), lambda j: (0, j)),
        compiler_params=pltpu.CompilerParams(vmem_limit_bytes=VMEM_LIMIT),
        name="modulation",
    )(cvec, mod_w, mod_b.reshape(1, n))


def _prenorm(x, g, scale1p, shift):
    ms = jnp.mean(x * x, axis=-1, keepdims=True)
    return x * lax.rsqrt(ms + EPS) * g * scale1p + shift


def _seg_sum(x, bd_ref):
    return _mm_exact_r(x, bd_ref[...])


def _rope(x, cos, sin_signed):
    w = x.shape[-1]
    lane = lax.broadcasted_iota(jnp.int32, (1, w), 1)
    first = (lane & 31) < 16
    partner = jnp.where(first, pltpu.roll(x, w - 16, axis=1), pltpu.roll(x, 16, axis=1))
    return x * cos + partner * sin_signed


def _tile_lanes(x, reps):
    return jnp.concatenate([x] * reps, axis=-1) if reps > 1 else x


def _in0_kernel(use_rope, x_ref, sc_ref, sh_ref, g_ref, wslab_ref, wrg_ref, wq_ref, wk_ref, wv_ref,
                wag_ref, bdq_ref, bdk_ref, gq_ref, gk_ref, cos_ref, sin_ref,
                slab_ref, rg_ref, q_ref, k_ref, v_ref, ag_ref):
    h = _prenorm(x_ref[0], g_ref[...], sc_ref[0], sh_ref[0]).astype(BF16)
    d = functools.partial(jnp.dot, preferred_element_type=F32)
    slab_ref[0] = d(h, wslab_ref[...])
    rg_ref[0] = d(h, wrg_ref[...])
    v_ref[0] = d(h, wv_ref[...])
    ag_ref[0] = d(h, wag_ref[...])
    q = d(h, wq_ref[...])
    k = d(h, wk_ref[...])
    q = q * lax.rsqrt(_seg_sum(q * q, bdq_ref) * (1.0 / HEAD64) + EPS) * gq_ref[...]
    k = k * lax.rsqrt(_seg_sum(k * k, bdk_ref) * (1.0 / HEAD64) + EPS) * gk_ref[...]
    if use_rope:
        cos = cos_ref[...]
        sin = sin_ref[...]
        q = _rope(q, _tile_lanes(cos, q.shape[-1] // LANES), _tile_lanes(sin, q.shape[-1] // LANES))
        k = _rope(k, cos, sin)
    q_ref[0] = q
    k_ref[0] = k


def _const_spec(shape):
    nd = len(shape)
    return pl.BlockSpec(shape, lambda *_: (0,) * nd)


def _mod_spec(arr, d):
    if arr.shape[0] == 1:
        return pl.BlockSpec((1, 1, d), lambda b, i: (0, 0, 0))
    return pl.BlockSpec((1, 1, d), lambda b, i: (b, 0, 0))


def _in0(x, scale1p, shift, w, use_rope, tm):
    bsz, t, d = x.shape
    widths = [w["slab"].shape[1], 512, 512, 128, 128, 512]
    outs = [jax.ShapeDtypeStruct((bsz, t, n), F32) for n in widths]
    tok = lambda n: pl.BlockSpec((1, tm, n), lambda b, i: (b, i, 0))
    in_specs = [tok(d), _mod_spec(scale1p, d), _mod_spec(shift, d), _const_spec((1, d))]
    in_specs += [_const_spec(w[k].shape) for k in ("slab", "rg", "q", "k", "v", "ag", "bdq", "bdk", "gq", "gk")]
    in_specs += [pl.BlockSpec((tm, LANES), lambda b, i: (i, 0))] * 2
    return pl.pallas_call(
        functools.partial(_in0_kernel, use_rope),
        out_shape=outs,
        grid=(bsz, t // tm),
        in_specs=in_specs,
        out_specs=[tok(n) for n in widths],
        compiler_params=pltpu.CompilerParams(dimension_semantics=("parallel", "parallel"),
                                             vmem_limit_bytes=VMEM_LIMIT),
        name="in_proj0",
    )(x, scale1p, shift, w["g_pre"], w["slab"], w["rg"], w["q"], w["k"], w["v"], w["ag"],
      w["bdq"], w["bdk"], w["gq"], w["gk"], w["cos"], w["sin"])


def _neighbours(x, prev_blk, next_blk, i, n_blocks):
    tm = x.shape[0]
    row = lax.broadcasted_iota(jnp.int32, (tm, 1), 0)
    prev_row = jnp.where(i == 0, 0.0, prev_blk[SUBLANES - 1:SUBLANES, :])
    next_row = jnp.where(i == n_blocks - 1, 0.0, next_blk[0:1, :])
    prev = jnp.where(row == 0, prev_row, pltpu.roll(x, 1, axis=0))
    nxt = jnp.where(row == tm - 1, next_row, pltpu.roll(x, tm - 1, axis=0))
    return prev, nxt


def _halo_specs(tm, c, t):
    r = tm // SUBLANES
    last = t // SUBLANES - 1
    main = pl.BlockSpec((1, tm, c), lambda b, i: (b, i, 0))
    prev = pl.BlockSpec((1, SUBLANES, c), lambda b, i: (b, jnp.maximum(i * r - 1, 0), 0))
    nxt = pl.BlockSpec((1, SUBLANES, c), lambda b, i: (b, jnp.minimum((i + 1) * r, last), 0))
    return [main, prev, nxt]


def _rwprep_kernel(n_blocks, s_ref, sp_ref, sn_ref, mu_ref, w2_ref, w0_ref, a2_ref, a0_ref,
                   kk_ref, ka_ref, bd_ref,
                   r_ref, k_ref, v_ref, kkn_ref, b_ref, lw_ref):
    i = pl.program_id(1)
    x = s_ref[0]
    prev, nxt = _neighbours(x, sp_ref[0], sn_ref[0], i, n_blocks)
    xs = x + mu_ref[...] * (0.5 * (prev + nxt) - x)
    c = 512
    r, kr, vr = xs[:, :c], xs[:, c:2 * c], xs[:, 2 * c:3 * c]
    lw_in = xs[:, 3 * c:3 * c + LANES]
    la_in = xs[:, 3 * c + LANES:3 * c + 2 * LANES]
    lor = _mm(jnp.tanh(lw_in), w2_ref[...]) + w0_ref[...]
    logw = -RW_DECAY_SCALE * jax.nn.sigmoid(lor)
    a = jax.nn.sigmoid(a0_ref[...] + _mm(la_in, a2_ref[...]))
    kkp = kr * kk_ref[...]
    kkn = kkp * lax.rsqrt(_seg_sum(kkp * kkp, bd_ref) + EPS)
    r_ref[0] = r
    k_ref[0] = kr * (1.0 + (a - 1.0) * ka_ref[...])
    v_ref[0] = vr
    kkn_ref[0] = kkn
    b_ref[0] = kkn * a
    lw_ref[0, 0] = logw[:, :c]
    lw_ref[1, 0] = logw[:, c:]


def _rwprep(slab, w, tm):
    bsz, t, cs = slab.shape
    c = 512
    nb = t // tm
    tok = pl.BlockSpec((1, tm, c), lambda b, i: (b, i, 0))
    outs = [jax.ShapeDtypeStruct((bsz, t, c), F32)] * 5 + [jax.ShapeDtypeStruct((2, bsz, t, c), F32)]
    names = ("mu", "w2", "w0", "a2", "a0", "k_k", "k_a", "bd")
    return pl.pallas_call(
        functools.partial(_rwprep_kernel, nb),
        out_shape=outs,
        grid=(bsz, nb),
        in_specs=_halo_specs(tm, cs, t) + [_const_spec(w[k].shape) for k in names],
        out_specs=[tok] * 5 + [pl.BlockSpec((2, 1, tm, c), lambda b, i: (0, b, i, 0))],
        compiler_params=pltpu.CompilerParams(dimension_semantics=("parallel", "parallel"),
                                             vmem_limit_bytes=VMEM_LIMIT),
        name="rwkv_prep",
    )(slab, slab, slab, *[w[k] for k in names])


QUAD = 4 * HEAD64


def _rwscan_kernel(nc, r_ref, k_ref, v_ref, kk_ref, b_ref, lw_ref, s0_ref, o_ref, sf_ref, st_ref):
    d = pl.program_id(0)
    c = pl.program_id(2)
    sgn = 1 - 2 * d
    L = CHUNK

    @pl.when(c == 0)
    def _():
        st_ref[...] = s0_ref[0, 0]

    _, incl = _order_masks(L, sgn)
    incl_bf = jnp.where(incl, 1.0, 0.0).astype(BF16)
    strict4, incl4 = _order_masks(L, sgn, reps=4)
    hmask = _lane_group_masks(QUAD, HEAD64, 4)
    rowq = lax.broadcasted_iota(jnp.int32, (QUAD, QUAD), 0)
    colq = lax.broadcasted_iota(jnp.int32, (QUAD, QUAD), 1)
    bd = (rowq >> 6) == (colq >> 6)

    outs = []
    for qd in range(r_ref.shape[-1] // QUAD):
        sl = slice(qd * QUAD, (qd + 1) * QUAD)
        r, k, v = r_ref[0, :, sl], k_ref[0, :, sl], v_ref[0, :, sl]
        kk, b, lw = kk_ref[0, :, sl], b_ref[0, :, sl], lw_ref[0, 0, :, sl]
        s_bd = st_ref[qd]

        cw = _mm_exact_l(incl_bf, lw)
        c0 = cw[L // 2:L // 2 + 1, :]
        tot = jnp.sum(lw, axis=0, keepdims=True)
        e_rel = jnp.exp(cw - c0)
        e_inv = jnp.exp(c0 - cw)
        ec0 = jnp.exp(c0)
        a_rel = kk * e_rel * jnp.exp(-lw)
        r_rel = r * e_rel
        b_rel = b * e_inv
        k_rel = k * e_inv
        e_fin = jnp.exp(tot - c0)

        sa = _stack_heads(a_rel, hmask)
        sr = _stack_heads(r_rel, hmask)
        n_ab = jnp.where(strict4, _mm_nt(sa, b_rel), 0.0)
        m_ak = jnp.where(strict4, _mm_nt(sa, k_rel), 0.0)
        p_rb = jnp.where(incl4, _mm_nt(sr, b_rel), 0.0)
        p_rk = jnp.where(incl4, _mm_nt(sr, k_rel), 0.0)

        rhs = -_mm_nt(a_rel * ec0, s_bd)
        for h in range(4):
            rhs = rhs - jnp.where(hmask[h], _mm(m_ak[h * L:(h + 1) * L], v), 0.0)
        e = jnp.zeros_like(rhs)
        for h in range(4):
            t_h = _tri_inv(n_ab[h * L:(h + 1) * L], _mm)
            e = e + jnp.where(hmask[h], _mm(t_h, rhs), 0.0)
        o = _mm_nt(r_rel * ec0, s_bd)
        for h in range(4):
            hs = slice(h * L, (h + 1) * L)
            o = o + jnp.where(hmask[h], _mm(p_rb[hs], e) + _mm(p_rk[hs], v), 0.0)
        ev = jnp.concatenate([e, v], axis=0)
        bk = jnp.concatenate([b_rel * e_fin, k_rel * e_fin], axis=0)
        st_ref[qd] = s_bd * jnp.exp(tot) + jnp.where(bd, _mm_tn(ev, bk), 0.0)
        outs.append(o)
    o_ref[0, 0] = jnp.concatenate(outs, axis=-1)

    @pl.when(c == nc - 1)
    def _():
        sf_ref[0, 0] = st_ref[...]


def _chunk_idx(d, c, nc):
    return c + d * (nc - 1 - 2 * c)


def _rwscan(r, k, v, kk, b, lw, s0):
    bsz, t, cdim = r.shape
    nc = t // CHUNK
    nq = cdim // QUAD
    tok = pl.BlockSpec((1, CHUNK, cdim), lambda d, b_, c: (b_, _chunk_idx(d, c, nc), 0))
    dtok = pl.BlockSpec((1, 1, CHUNK, cdim), lambda d, b_, c: (d, b_, _chunk_idx(d, c, nc), 0))
    sspec = pl.BlockSpec((1, 1, nq, QUAD, QUAD), lambda d, b_, c: (d, b_, 0, 0, 0))
    return pl.pallas_call(
        functools.partial(_rwscan_kernel, nc),
        out_shape=[jax.ShapeDtypeStruct((2, bsz, t, cdim), F32),
                   jax.ShapeDtypeStruct((2, bsz, nq, QUAD, QUAD), F32)],
        grid=(2, bsz, nc),
        in_specs=[tok] * 5 + [dtok, sspec],
        out_specs=[dtok, sspec],
        scratch_shapes=[pltpu.VMEM((nq, QUAD, QUAD), F32)],
        compiler_params=pltpu.CompilerParams(dimension_semantics=("parallel", "parallel", "arbitrary"),
                                             vmem_limit_bytes=VMEM_LIMIT),
        name="rwkv_scan",
    )(r, k, v, kk, b, lw, s0)


def _attn_kernel(q_ref, k_ref, v_ref, o_ref):
    n_pairs = q_ref.shape[-1] // LANES
    pairs_per_kv = n_pairs // (k_ref.shape[1] // 2)
    outs = []
    for p in range(n_pairs):
        kv = p // pairs_per_kv
        qp = (q_ref[0, :, p * LANES:(p + 1) * LANES] * (HEAD64 ** -0.5)).astype(BF16)
        acc = None
        for par in range(2):
            kmat = k_ref[0, 2 * kv + par]
            vmat = v_ref[0, 2 * kv + par]
            s = lax.dot_general(qp, kmat, (((1,), (1,)), ((), ())), preferred_element_type=F32)
            m = jnp.max(s, axis=-1, keepdims=True)
            pexp = jnp.exp(s - m)
            l = jnp.sum(pexp, axis=-1, keepdims=True)
            pv = jnp.dot(pexp.astype(BF16), vmat, preferred_element_type=F32) / l
            acc = pv if acc is None else acc + pv
        outs.append(acc)
    o_ref[0] = jnp.concatenate(outs, axis=-1)


def _attention(q, kpad, vpad, tq):
    bsz, t, c = q.shape
    tk = kpad.shape[2]
    kvspec = pl.BlockSpec((1, kpad.shape[1], tk, LANES), lambda b, i: (b, 0, 0, 0))
    return pl.pallas_call(
        _attn_kernel,
        out_shape=jax.ShapeDtypeStruct((bsz, t, c), F32),
        grid=(bsz, t // tq),
        in_specs=[pl.BlockSpec((1, tq, c), lambda b, i: (b, i, 0)), kvspec, kvspec],
        out_specs=pl.BlockSpec((1, tq, c), lambda b, i: (b, i, 0)),
        compiler_params=pltpu.CompilerParams(dimension_semantics=("parallel", "parallel"),
                                             vmem_limit_bytes=VMEM_LIMIT),
        name="attention",
    )(q, kpad, vpad)


def _postnorm_residual(x, out, g_post, gate):
    ms = jnp.mean(out * out, axis=-1, keepdims=True)
    return x + gate * (out * lax.rsqrt(ms + EPS) * g_post)


def _out0_kernel(x_ref, gate_ref, of_ref, ob_ref, r_ref, k_ref, v_ref, rg_ref, oat_ref, ag_ref,
                 gng_ref, gnb_ref, rk_ref, bd_ref, wo_rw_ref, wo_at_ref, gpost_ref, y_ref):
    o = of_ref[0, 0] + ob_ref[0, 0]
    inv = 1.0 / HEAD64
    mu = _seg_sum(o, bd_ref) * inv
    dlt = o - mu
    var = _seg_sum(dlt * dlt, bd_ref) * inv
    gn = dlt * lax.rsqrt(var + GN_EPS) * gng_ref[...] + gnb_ref[...]
    bonus = _seg_sum(r_ref[0] * k_ref[0] * rk_ref[...], bd_ref) * v_ref[0]
    o_rw = (gn + bonus) * _silu(rg_ref[0])
    o_at = oat_ref[0] * _silu(ag_ref[0])
    out = _mm(o_rw, wo_rw_ref[...]) + _mm(o_at, wo_at_ref[...])
    y_ref[0] = _postnorm_residual(x_ref[0], out, gpost_ref[...], gate_ref[0])


def _out0(x, gate, o_rw, r, k, v, rg, o_at, ag, w, tm):
    bsz, t, d = x.shape
    c = 512
    tokd = pl.BlockSpec((1, tm, d), lambda b, i: (b, i, 0))
    tok = pl.BlockSpec((1, tm, c), lambda b, i: (b, i, 0))
    dirspec = lambda dd: pl.BlockSpec((1, 1, tm, c), lambda b, i: (dd, b, i, 0))
    names = ("gn_g", "gn_b", "r_k", "bd", "wo_rw", "wo_at", "g_post")
    return pl.pallas_call(
        _out0_kernel,
        out_shape=jax.ShapeDtypeStruct((bsz, t, d), F32),
        grid=(bsz, t // tm),
        in_specs=[tokd, _mod_spec(gate, d), dirspec(0), dirspec(1)] + [tok] * 6
                 + [_const_spec(w[n].shape) for n in names],
        out_specs=tokd,
        compiler_params=pltpu.CompilerParams(dimension_semantics=("parallel", "parallel"),
                                             vmem_limit_bytes=VMEM_LIMIT),
        name="out_proj0",
    )(x, gate, o_rw, o_rw, r, k, v, rg, o_at, ag, *[w[n] for n in names])


def _in1_kernel(x_ref, sc_ref, sh_ref, g_ref, wq_ref, wk_ref, wv_ref, wl_ref, wgg_ref, wd_ref, ws_ref,
                wdg_ref, g2_ref, gb_ref,
                q_ref, k_ref, v_ref, lw_ref, gg_ref, dqkv_ref, small_ref, dg_ref):
    h = _prenorm(x_ref[0], g_ref[...], sc_ref[0], sh_ref[0]).astype(BF16)
    d = functools.partial(jnp.dot, preferred_element_type=F32)
    q_ref[0] = d(h, wq_ref[...]) * (HEAD64 ** -0.5)
    k_ref[0] = d(h, wk_ref[...])
    v_ref[0] = d(h, wv_ref[...])
    gg_ref[0] = d(h, wgg_ref[...])
    dqkv_ref[0] = d(h, wd_ref[...])
    small_ref[0] = d(h, ws_ref[...])
    dg_ref[0] = d(h, wdg_ref[...])
    lora = d(h, wl_ref[...])
    lw = _log_sigmoid(_mm(lora, g2_ref[...]) + gb_ref[...]) * (1.0 / GLA_GATE_NORM)
    half = lw.shape[-1] // 2
    lw_ref[0, 0] = lw[:, :half]
    lw_ref[1, 0] = lw[:, half:]


def _in1(x, scale1p, shift, w, tm):
    bsz, t, d = x.shape
    widths = [256, 256, 512, None, 512, 1536, 128, 512]
    tok = lambda n: pl.BlockSpec((1, tm, n), lambda b, i: (b, i, 0))
    outs, ospecs = [], []
    for n in widths:
        if n is None:
            outs.append(jax.ShapeDtypeStruct((2, bsz, t, 256), F32))
            ospecs.append(pl.BlockSpec((2, 1, tm, 256), lambda b, i: (0, b, i, 0)))
        else:
            outs.append(jax.ShapeDtypeStruct((bsz, t, n), F32))
            ospecs.append(tok(n))
    names = ("q", "k", "v", "lora", "gg", "dqkv", "small", "dg", "g2", "gb")
    return pl.pallas_call(
        _in1_kernel,
        out_shape=outs,
        grid=(bsz, t // tm),
        in_specs=[tok(d), _mod_spec(scale1p, d), _mod_spec(shift, d), _const_spec((1, d))]
                 + [_const_spec(w[n].shape) for n in names],
        out_specs=ospecs,
        compiler_params=pltpu.CompilerParams(dimension_semantics=("parallel", "parallel"),
                                             vmem_limit_bytes=VMEM_LIMIT),
        name="in_proj1",
    )(x, scale1p, shift, w["g_pre"], *[w[n] for n in names])


def _gdnprep_kernel(n_blocks, x_ref, xp_ref, xn_ref, s_ref, cw_ref, alog_ref, dtb_ref,
                    q_ref, k_ref, v_ref, g_ref):
    i = pl.program_id(1)
    x = x_ref[0]
    prev, nxt = _neighbours(x, xp_ref[0], xn_ref[0], i, n_blocks)
    y = _silu(prev * cw_ref[0:1, :] + x * cw_ref[1:2, :] + nxt * cw_ref[2:3, :])
    c = 512
    qs, ks = [], []
    for h in range(c // HEAD128):
        qh = y[:, h * HEAD128:(h + 1) * HEAD128]
        kh = y[:, c + h * HEAD128:c + (h + 1) * HEAD128]
        qs.append(qh * lax.rsqrt(jnp.sum(qh * qh, axis=-1, keepdims=True) + EPS) * (HEAD128 ** -0.5))
        ks.append(kh * lax.rsqrt(jnp.sum(kh * kh, axis=-1, keepdims=True) + EPS))
    q_ref[0] = jnp.concatenate(qs, axis=-1)
    k_ref[0] = jnp.concatenate(ks, axis=-1)
    v_ref[0] = y[:, 2 * c:]
    s = s_ref[0]
    lane = lax.broadcasted_iota(jnp.int32, (1, LANES), 1)
    loga = -jnp.exp(alog_ref[...]) * _softplus(s + dtb_ref[...])
    g_ref[0] = jnp.where(lane < 8, loga, jax.nn.sigmoid(s))


def _gdnprep(dqkv, small, w, tm):
    bsz, t, cs = dqkv.shape
    c = 512
    nb = t // tm
    tok = pl.BlockSpec((1, tm, c), lambda b, i: (b, i, 0))
    tokl = pl.BlockSpec((1, tm, LANES), lambda b, i: (b, i, 0))
    names = ("conv", "alog", "dtb")
    return pl.pallas_call(
        functools.partial(_gdnprep_kernel, nb),
        out_shape=[jax.ShapeDtypeStruct((bsz, t, c), F32)] * 3 + [jax.ShapeDtypeStruct((bsz, t, LANES), F32)],
        grid=(bsz, nb),
        in_specs=_halo_specs(tm, cs, t) + [tokl] + [_const_spec(w[n].shape) for n in names],
        out_specs=[tok] * 3 + [tokl],
        compiler_params=pltpu.CompilerParams(dimension_semantics=("parallel", "parallel"),
                                             vmem_limit_bytes=VMEM_LIMIT),
        name="gdn_prep",
    )(dqkv, dqkv, dqkv, small, *[w[n] for n in names])


def _glascan_kernel(nc, q_ref, k_ref, v_ref, lw_ref, s0_ref, o_ref, sf_ref, st_ref):
    d = pl.program_id(0)
    c = pl.program_id(2)
    sgn = 1 - 2 * d
    L = CHUNK
    nh = q_ref.shape[-1] // HEAD64

    @pl.when(c == 0)
    def _():
        st_ref[...] = s0_ref[0, 0]

    _, incl = _order_masks(L, sgn)
    incl_bf = jnp.where(incl, 1.0, 0.0).astype(BF16)
    _, incl4 = _order_masks(L, sgn, reps=nh)
    hmask = _lane_group_masks(nh * HEAD64, HEAD64, nh)
    rowv = lax.broadcasted_iota(jnp.int32, st_ref.shape, 0)
    colk = lax.broadcasted_iota(jnp.int32, st_ref.shape, 1)
    bd = (rowv >> 7) == (colk >> 6)

    q, k, v, lw = q_ref[0], k_ref[0], v_ref[0], lw_ref[0, 0]
    st = st_ref[...]
    cw = _mm_exact_l(incl_bf, lw)
    c0 = cw[L // 2:L // 2 + 1, :]
    tot = jnp.sum(lw, axis=0, keepdims=True)
    q_rel = q * jnp.exp(cw - c0)
    k_rel = k * jnp.exp(c0 - cw)
    att = jnp.where(incl4, _mm_nt(_stack_heads(q_rel, hmask), k_rel), 0.0)
    intra = [_mm(att[h * L:(h + 1) * L], v[:, h * HEAD128:(h + 1) * HEAD128]) for h in range(nh)]
    o_ref[0, 0] = jnp.concatenate(intra, axis=-1) + _mm_nt(q_rel * jnp.exp(c0), st)
    st_ref[...] = st * jnp.exp(tot) + jnp.where(bd, _mm_tn(v, k_rel * jnp.exp(tot - c0)), 0.0)

    @pl.when(c == nc - 1)
    def _():
        sf_ref[0, 0] = st_ref[...]


def _glascan(q, k, v, lw, s0):
    bsz, t, ck = q.shape
    cv = v.shape[-1]
    nc = t // CHUNK
    tokk = pl.BlockSpec((1, CHUNK, ck), lambda d, b_, c: (b_, _chunk_idx(d, c, nc), 0))
    tokv = pl.BlockSpec((1, CHUNK, cv), lambda d, b_, c: (b_, _chunk_idx(d, c, nc), 0))
    dtokk = pl.BlockSpec((1, 1, CHUNK, ck), lambda d, b_, c: (d, b_, _chunk_idx(d, c, nc), 0))
    dtokv = pl.BlockSpec((1, 1, CHUNK, cv), lambda d, b_, c: (d, b_, _chunk_idx(d, c, nc), 0))
    sspec = pl.BlockSpec((1, 1, cv, ck), lambda d, b_, c: (d, b_, 0, 0))
    return pl.pallas_call(
        functools.partial(_glascan_kernel, nc),
        out_shape=[jax.ShapeDtypeStruct((2, bsz, t, cv), F32),
                   jax.ShapeDtypeStruct((2, bsz, cv, ck), F32)],
        grid=(2, bsz, nc),
        in_specs=[tokk, tokk, tokv, dtokk, sspec],
        out_specs=[dtokv, sspec],
        scratch_shapes=[pltpu.VMEM((cv, ck), F32)],
        compiler_params=pltpu.CompilerParams(dimension_semantics=("parallel", "parallel", "arbitrary"),
                                             vmem_limit_bytes=VMEM_LIMIT),
        name="gla_scan",
    )(q, k, v, lw, s0)


def _gdnscan_kernel(nc, q_ref, k_ref, v_ref, gc_ref, gr_ref, s0_ref, o_ref, sf_ref, st_ref):
    d = pl.program_id(0)
    c = pl.program_id(2)
    sgn = 1 - 2 * d
    L = CHUNK
    nh = q_ref.shape[-1] // HEAD128

    @pl.when(c == 0)
    def _():
        st_ref[...] = s0_ref[0, 0]

    strict, incl = _order_masks(L, sgn)
    incl_bf = jnp.where(incl, 1.0, 0.0).astype(BF16)
    gcol = gc_ref[0, 0]
    grow = gr_ref[0, 0, 0]
    cum_c = _mm_exact_l(incl_bf, gcol)
    cum_r = _mm_exact_nt(grow, incl_bf)
    tot = jnp.sum(gcol, axis=0, keepdims=True)

    outs = []
    for h in range(nh):
        hs = slice(h * HEAD128, (h + 1) * HEAD128)
        q, k, v = q_ref[0, :, hs], k_ref[0, :, hs], v_ref[0, :, hs]
        s = st_ref[h]
        g = cum_c[:, h:h + 1]
        beta = gcol[:, nh + h:nh + h + 1]
        g_last = tot[:, h:h + 1]
        dec = jnp.exp(jnp.minimum(g - cum_r[h:h + 1, :], 0.0))
        nmat = jnp.where(strict, dec * _mm_nt(k, k) * beta, 0.0)
        tinv = _tri_inv(nmat, _mm)
        gam = jnp.exp(g)
        w_mat = _mm(tinv, (beta * gam) * k)
        u_mat = _mm(tinv, beta * v)
        aqk = jnp.where(incl, dec * _mm_nt(q, k), 0.0)
        u = u_mat - _mm(w_mat, s)
        outs.append(_mm(q * gam, s) + _mm(aqk, u))
        st_ref[h] = s * jnp.exp(g_last) + _mm_tn(k * jnp.exp(g_last - g), u)
    o_ref[0, 0] = jnp.concatenate(outs, axis=-1)

    @pl.when(c == nc - 1)
    def _():
        sf_ref[0, 0] = st_ref[...]


def _gdnscan(q, k, v, gcol, grow, s0):
    bsz, t, cdim = q.shape
    nh = cdim // HEAD128
    nc = t // CHUNK
    tok = pl.BlockSpec((1, CHUNK, cdim), lambda d, b_, c: (b_, _chunk_idx(d, c, nc), 0))
    dtok = pl.BlockSpec((1, 1, CHUNK, cdim), lambda d, b_, c: (d, b_, _chunk_idx(d, c, nc), 0))
    gcs = pl.BlockSpec((1, 1, CHUNK, LANES), lambda d, b_, c: (d, b_, _chunk_idx(d, c, nc), 0))
    grs = pl.BlockSpec((1, 1, 1, 2 * nh, CHUNK), lambda d, b_, c: (d, b_, _chunk_idx(d, c, nc), 0, 0))
    sspec = pl.BlockSpec((1, 1, nh, HEAD128, HEAD128), lambda d, b_, c: (d, b_, 0, 0, 0))
    return pl.pallas_call(
        functools.partial(_gdnscan_kernel, nc),
        out_shape=[jax.ShapeDtypeStruct((2, bsz, t, cdim), F32),
                   jax.ShapeDtypeStruct((2, bsz, nh, HEAD128, HEAD128), F32)],
        grid=(2, bsz, nc),
        in_specs=[tok, tok, tok, gcs, grs, sspec],
        out_specs=[dtok, sspec],
        scratch_shapes=[pltpu.VMEM((nh, HEAD128, HEAD128), F32)],
        compiler_params=pltpu.CompilerParams(dimension_semantics=("parallel", "parallel", "arbitrary"),
                                             vmem_limit_bytes=VMEM_LIMIT),
        name="gdn_scan",
    )(q, k, v, gcol, grow, s0)


def _head_rmsnorm(o, g):
    parts = []
    for h in range(o.shape[-1] // HEAD128):
        oh = o[:, h * HEAD128:(h + 1) * HEAD128]
        parts.append(oh * lax.rsqrt(jnp.mean(oh * oh, axis=-1, keepdims=True) + EPS) * g)
    return jnp.concatenate(parts, axis=-1)


def _out1_kernel(x_ref, gate_ref, glf_ref, glb_ref, gdf_ref, gdb_ref, gg_ref, dg_ref,
                 gla_g_ref, gdn_g_ref, wo_gla_ref, wo_gdn_ref, gpost_ref, y_ref):
    o_gla = _head_rmsnorm(glf_ref[0, 0] + glb_ref[0, 0], gla_g_ref[...]) * _silu(gg_ref[0])
    o_gdn = _head_rmsnorm(gdf_ref[0, 0] + gdb_ref[0, 0], gdn_g_ref[...]) * _silu(dg_ref[0])
    out = _mm(o_gla, wo_gla_ref[...]) + _mm(o_gdn, wo_gdn_ref[...])
    y_ref[0] = _postnorm_residual(x_ref[0], out, gpost_ref[...], gate_ref[0])


def _out1(x, gate, o_gla, o_gdn, gg, dg, w, tm):
    bsz, t, d = x.shape
    c = 512
    tokd = pl.BlockSpec((1, tm, d), lambda b, i: (b, i, 0))
    tok = pl.BlockSpec((1, tm, c), lambda b, i: (b, i, 0))
    dirspec = lambda dd: pl.BlockSpec((1, 1, tm, c), lambda b, i: (dd, b, i, 0))
    names = ("gla_g", "gdn_g", "wo_gla", "wo_gdn", "g_post")
    return pl.pallas_call(
        _out1_kernel,
        out_shape=jax.ShapeDtypeStruct((bsz, t, d), F32),
        grid=(bsz, t // tm),
        in_specs=[tokd, _mod_spec(gate, d), dirspec(0), dirspec(1), dirspec(0), dirspec(1), tok, tok]
                 + [_const_spec(w[n].shape) for n in names],
        out_specs=tokd,
        compiler_params=pltpu.CompilerParams(dimension_semantics=("parallel", "parallel"),
                                             vmem_limit_bytes=VMEM_LIMIT),
        name="out_proj1",
    )(x, gate, o_gla, o_gla, o_gdn, o_gdn, gg, dg, *[w[n] for n in names])


def _row(v):
    return v.reshape(1, -1).astype(F32)


def _block_ones(n_groups, width):
    return jnp.asarray(np.kron(np.eye(n_groups, dtype=np.float32), np.ones((width, width), np.float32)), BF16)


def _rope_tables(t):
    rows = t // GRID_W
    row_id = np.repeat(np.arange(rows, dtype=np.float32), GRID_W)
    col_id = np.tile(np.arange(GRID_W, dtype=np.float32), rows)
    nf = HEAD64 // 4
    inv = jnp.asarray(ROPE_THETA, F32) ** (-jnp.arange(nf, dtype=F32) / nf)
    ang_r = jnp.asarray(row_id)[:, None] * inv[None, :]
    ang_c = jnp.asarray(col_id)[:, None] * inv[None, :]
    cos = jnp.concatenate([jnp.cos(ang_r)] * 2 + [jnp.cos(ang_c)] * 2, axis=-1)
    sin = jnp.concatenate([-jnp.sin(ang_r), jnp.sin(ang_r), -jnp.sin(ang_c), jnp.sin(ang_c)], axis=-1)
    return jnp.concatenate([cos] * 2, axis=-1), jnp.concatenate([sin] * 2, axis=-1)


def _pad_cols(w, n):
    return jnp.pad(w, ((0, 0), (0, n - w.shape[1])))


def _layer0_weights(p):
    w_in = p["w_in"]
    rw_w = 512
    slab_w = 3 * rw_w + 192
    cuts = np.cumsum([slab_w, 512, 512, 128, 128, 512])
    slab, rg, q, k, v, ag = jnp.split(w_in, [int(c) for c in cuts[:-1]], axis=1)
    w = {}
    w["slab"] = _pad_cols(slab, 3 * rw_w + 2 * LANES).astype(BF16)
    for n, a in (("rg", rg), ("q", q), ("k", k), ("v", v), ("ag", ag)):
        w[n] = a.astype(BF16)
    w["g_pre"] = _row(p["g_pre"])
    w["g_post"] = _row(p["g_post"])
    w["bdq"] = _block_ones(8, HEAD64)
    w["bdk"] = _block_ones(2, HEAD64)
    w["bd"] = w["bdq"]
    w["gq"] = _row(jnp.tile(p["at_gq"], 8))
    w["gk"] = _row(jnp.tile(p["at_gk"], 2))
    w["mu"] = _row(jnp.pad(p["rw_mu"], (0, LANES - 64)))
    z = jnp.zeros((64, rw_w), F32)
    w["w2"] = jnp.concatenate([jnp.concatenate([p["rw_w2_f"], z], axis=1),
                               jnp.concatenate([z, p["rw_w2_b"]], axis=1)], axis=0).astype(BF16)
    w["w0"] = _row(jnp.concatenate([p["rw_w0_f"], p["rw_w0_b"]]))
    w["a2"] = jnp.concatenate([p["rw_a2"], z], axis=0).astype(BF16)
    w["a0"] = _row(p["rw_a0"])
    w["k_k"] = _row(p["rw_k_k"])
    w["k_a"] = _row(p["rw_k_a"])
    w["r_k"] = _row(p["rw_r_k"])
    w["gn_g"] = _row(p["rw_gn_g"])
    w["gn_b"] = _row(p["rw_gn_b"])
    w["wo_rw"] = p["w_out"][:rw_w].astype(BF16)
    w["wo_at"] = p["w_out"][rw_w:].astype(BF16)
    return w


def _layer1_weights(p):
    w_in = p["w_in"]
    cuts = np.cumsum([256, 256, 512, 16, 16, 512, 1536, 4, 4, 4, 512])
    gq, gk, gv, gf, gb, gg, dqkv, af, ab, be, dg = jnp.split(w_in, [int(c) for c in cuts[:-1]], axis=1)
    w = {}
    for n, a in (("q", gq), ("k", gk), ("v", gv), ("gg", gg), ("dqkv", dqkv), ("dg", dg)):
        w[n] = a.astype(BF16)
    w["lora"] = _pad_cols(jnp.concatenate([gf, gb], axis=1), LANES).astype(BF16)
    w["small"] = _pad_cols(jnp.concatenate([af, ab, be], axis=1), LANES).astype(BF16)
    z = jnp.zeros((16, 256), F32)
    g2 = jnp.concatenate([jnp.concatenate([p["gla_g2_f"], z], axis=1),
                          jnp.concatenate([z, p["gla_g2_b"]], axis=1)], axis=0)
    w["g2"] = jnp.pad(g2, ((0, LANES - 32), (0, 0))).astype(BF16)
    w["gb"] = _row(jnp.concatenate([p["gla_gb_f"], p["gla_gb_b"]]))
    w["g_pre"] = _row(p["g_pre"])
    w["g_post"] = _row(p["g_post"])
    w["conv"] = p["gdn_conv"].astype(F32)
    w["alog"] = _row(jnp.pad(jnp.concatenate([p["gdn_A_log_f"], p["gdn_A_log_b"]]), (0, LANES - 8)))
    w["dtb"] = _row(jnp.pad(jnp.concatenate([p["gdn_dt_bias_f"], p["gdn_dt_bias_b"]]), (0, LANES - 8)))
    w["gla_g"] = _row(p["gla_norm_g"])
    w["gdn_g"] = _row(p["gdn_norm_g"])
    w["wo_gla"] = p["w_out"][:512].astype(BF16)
    w["wo_gdn"] = p["w_out"][512:].astype(BF16)
    return w


def _rw_state_to_bd(s):
    bsz = s.shape[0]
    s = s.reshape(bsz, 2, 4, HEAD64, HEAD64)
    eye = jnp.eye(4, dtype=s.dtype)
    return jnp.einsum("bqhvk,hg->bqhvgk", s, eye).reshape(bsz, 2, QUAD, QUAD)


def _rw_state_from_bd(s):
    bsz = s.shape[0]
    s = s.reshape(bsz, 2, 4, HEAD64, 4, HEAD64)
    return jnp.stack([s[:, :, h, :, h, :] for h in range(4)], axis=2).reshape(bsz, 8, HEAD64, HEAD64)


def _gla_state_to_bd(s):
    bsz = s.shape[0]
    eye = jnp.eye(4, dtype=s.dtype)
    return jnp.einsum("bhkv,hg->bhvgk", s, eye).reshape(bsz, 4 * HEAD128, 4 * HEAD64)


def _gla_state_from_bd(s):
    bsz = s.shape[0]
    s = s.reshape(bsz, 4, HEAD128, 4, HEAD64)
    return jnp.stack([jnp.swapaxes(s[:, h, :, h, :], -1, -2) for h in range(4)], axis=1)


def _pad_parity(x):
    z = jnp.zeros(x.shape[:-1] + (HEAD64,), x.dtype)
    parts = []
    for kv in range(2):
        xk = x[..., kv * HEAD64:(kv + 1) * HEAD64]
        parts += [jnp.concatenate([xk, z], axis=-1), jnp.concatenate([z, xk], axis=-1)]
    return jnp.stack(parts, axis=1).astype(BF16)


def _trunk(x, mods, w0, w1, ctx, tm, tq):
    bsz, t, _ = x.shape
    latent = ctx is not None

    sc, sh, gt = mods[0]
    slab, rg, q, k, v, ag = _in0(x, sc, sh, w0, latent, tm)
    r, kmod, vr, kkn, bvec, lw = _rwprep(slab, w0, tm)
    if latent:
        s0f, s0b, k_ctx, v_ctx = ctx[:4]
        s0 = jnp.stack([_rw_state_to_bd(s0f), _rw_state_to_bd(s0b)])
        k_all = jnp.concatenate([k_ctx.reshape(bsz, -1, 2 * HEAD64), k], axis=1)
        v_all = jnp.concatenate([v_ctx.reshape(bsz, -1, 2 * HEAD64), v], axis=1)
    else:
        s0 = jnp.zeros((2, bsz, 2, QUAD, QUAD), F32)
        k_all, v_all = k, v
    o_rw, s_rw = _rwscan(r, kmod, vr, kkn, bvec, lw, s0)
    o_at = _attention(q, _pad_parity(k_all), _pad_parity(v_all), tq)
    x1 = _out0(x, gt, o_rw, r, kmod, vr, rg, o_at, ag, w0, tm)

    sc, sh, gt = mods[1]
    gq, gk, gv, glw, gg, dqkv, small, dg = _in1(x1, sc, sh, w1, tm)
    dq, dk, dv, gsm = _gdnprep(dqkv, small, w1, tm)
    nc = t // CHUNK
    g8 = jnp.stack([jnp.concatenate([gsm[..., 0:4], gsm[..., 8:12]], axis=-1),
                    jnp.concatenate([gsm[..., 4:8], gsm[..., 8:12]], axis=-1)])
    gcol = jnp.pad(g8, ((0, 0), (0, 0), (0, 0), (0, LANES - 8)))
    grow = jnp.swapaxes(g8.reshape(2, bsz, nc, CHUNK, 8), -1, -2)
    if latent:
        sgf, sgb, sdf, sdb = ctx[4:]
        s0_gla = jnp.stack([_gla_state_to_bd(sgf), _gla_state_to_bd(sgb)])
        s0_gdn = jnp.stack([sdf, sdb])
    else:
        s0_gla = jnp.zeros((2, bsz, 4 * HEAD128, 4 * HEAD64), F32)
        s0_gdn = jnp.zeros((2, bsz, 4, HEAD128, HEAD128), F32)
    o_gla, s_gla = _glascan(gq, gk, gv, glw, s0_gla)
    o_gdn, s_gdn = _gdnscan(dq, dk, dv, gcol, grow, s0_gdn)
    y = _out1(x1, gt, o_gla, o_gdn, gg, dg, w1, tm)

    new = None
    if not latent:
        new = (_rw_state_from_bd(s_rw[0]), _rw_state_from_bd(s_rw[1]),
               k.reshape(bsz, t, 2, HEAD64), v.reshape(bsz, t, 2, HEAD64),
               _gla_state_from_bd(s_gla[0]), _gla_state_from_bd(s_gla[1]), s_gdn[0], s_gdn[1])
    return y, new


def _split_mod(m, d):
    shift, scale, gate = m[..., :d], m[..., d:2 * d], m[..., 2 * d:]
    return (1.0 + scale)[:, None, :], shift[:, None, :], gate[:, None, :]


def kernel(x_prompt, x_sample, state_l0_rwkv_fwd, state_l0_rwkv_bwd, cache_l0_k, cache_l0_v, state_l1_gla_fwd, state_l1_gla_bwd, state_l1_gdn_fwd, state_l1_gdn_bwd, c, c_ctx, l0_mod_w, l0_mod_b, l0_g_pre, l0_g_post, l0_w_in, l0_w_out, l0_rw_mu, l0_rw_w0_f, l0_rw_w2_f, l0_rw_w0_b, l0_rw_w2_b, l0_rw_a0, l0_rw_a2, l0_rw_k_k, l0_rw_k_a, l0_rw_r_k, l0_rw_gn_g, l0_rw_gn_b, l0_at_gq, l0_at_gk, l1_mod_w, l1_mod_b, l1_g_pre, l1_g_post, l1_w_in, l1_w_out, l1_gla_g2_f, l1_gla_gb_f, l1_gla_g2_b, l1_gla_gb_b, l1_gla_norm_g, l1_gdn_conv, l1_gdn_A_log_f, l1_gdn_dt_bias_f, l1_gdn_A_log_b, l1_gdn_dt_bias_b, l1_gdn_norm_g):
    p0 = {"g_pre": l0_g_pre, "g_post": l0_g_post, "w_in": l0_w_in, "w_out": l0_w_out, "rw_mu": l0_rw_mu,
          "rw_w0_f": l0_rw_w0_f, "rw_w2_f": l0_rw_w2_f, "rw_w0_b": l0_rw_w0_b, "rw_w2_b": l0_rw_w2_b,
          "rw_a0": l0_rw_a0, "rw_a2": l0_rw_a2, "rw_k_k": l0_rw_k_k, "rw_k_a": l0_rw_k_a,
          "rw_r_k": l0_rw_r_k, "rw_gn_g": l0_rw_gn_g, "rw_gn_b": l0_rw_gn_b,
          "at_gq": l0_at_gq, "at_gk": l0_at_gk}
    p1 = {"g_pre": l1_g_pre, "g_post": l1_g_post, "w_in": l1_w_in, "w_out": l1_w_out,
          "gla_g2_f": l1_gla_g2_f, "gla_gb_f": l1_gla_gb_f, "gla_g2_b": l1_gla_g2_b,
          "gla_gb_b": l1_gla_gb_b, "gla_norm_g": l1_gla_norm_g, "gdn_conv": l1_gdn_conv,
          "gdn_A_log_f": l1_gdn_A_log_f, "gdn_dt_bias_f": l1_gdn_dt_bias_f,
          "gdn_A_log_b": l1_gdn_A_log_b, "gdn_dt_bias_b": l1_gdn_dt_bias_b,
          "gdn_norm_g": l1_gdn_norm_g}
    d = x_prompt.shape[-1]
    nb = c.shape[0]
    w0 = _layer0_weights(p0)
    w1 = _layer1_weights(p1)
    cos, sin = _rope_tables(x_sample.shape[1])
    w0["cos"], w0["sin"] = cos, sin

    cvec = jnp.concatenate([c, c_ctx[None, :], jnp.zeros((SUBLANES - nb - 1, d), F32)], axis=0)
    m0 = _modulation(cvec, l0_mod_w, l0_mod_b)
    m1 = _modulation(cvec, l1_mod_w, l1_mod_b)
    mods_lat = [_split_mod(m[:nb], d) for m in (m0, m1)]
    mods_ctx = [_split_mod(m[nb:nb + 1], d) for m in (m0, m1)]

    y_prompt, new = _trunk(x_prompt, mods_ctx, w0, w1, None, 256, 256)
    ctx = (state_l0_rwkv_fwd, state_l0_rwkv_bwd, cache_l0_k, cache_l0_v,
           state_l1_gla_fwd, state_l1_gla_bwd, state_l1_gdn_fwd, state_l1_gdn_bwd)
    y_sample, _ = _trunk(x_sample, mods_lat, w0, w1, ctx, 256, 256)
    return (y_prompt, y_sample) + tuple(new)
```

```python
import functools

import numpy as np
import jax
import jax.numpy as jnp
from jax import lax
from jax.experimental import pallas as pl
from jax.experimental.pallas import tpu as pltpu

F32 = jnp.float32
BF16 = jnp.bfloat16

EPS = 1e-6
GN_EPS = 64e-5
CHUNK = 64
GRID_W = 64
ROPE_THETA = 10000.0
RW_DECAY_SCALE = 0.6065306597126334
GLA_GATE_NORM = 16.0
HEAD64 = 64
HEAD128 = 128
LANES = 128
SUBLANES = 8
VMEM_LIMIT = 56 * 1024 * 1024


def _mm(a, b):
    return jnp.dot(a.astype(BF16), b.astype(BF16), preferred_element_type=F32)


def _mm_nt(a, b):
    return lax.dot_general(a.astype(BF16), b.astype(BF16), (((1,), (1,)), ((), ())),
                           preferred_element_type=F32)


def _mm_tn(a, b):
    return lax.dot_general(a.astype(BF16), b.astype(BF16), (((0,), (0,)), ((), ())),
                           preferred_element_type=F32)


def _split3(x):
    hi = x.astype(BF16)
    r1 = x - hi.astype(F32)
    mid = r1.astype(BF16)
    lo = (r1 - mid.astype(F32)).astype(BF16)
    return hi, mid, lo


def _mm_exact_l(mask_bf16, x):
    hi, mid, lo = _split3(x)
    d = functools.partial(jnp.dot, preferred_element_type=F32)
    return d(mask_bf16, hi) + d(mask_bf16, mid) + d(mask_bf16, lo)


def _mm_exact_r(x, mask_bf16):
    hi, mid, lo = _split3(x)
    d = functools.partial(jnp.dot, preferred_element_type=F32)
    return d(hi, mask_bf16) + d(mid, mask_bf16) + d(lo, mask_bf16)


def _mm_exact_nt(x, mask_bf16):
    hi, mid, lo = _split3(x)
    d = lambda a: lax.dot_general(a, mask_bf16, (((1,), (1,)), ((), ())), preferred_element_type=F32)
    return d(hi) + d(mid) + d(lo)


def _mm3(a, b):
    ah = a.astype(BF16)
    al = (a - ah.astype(F32)).astype(BF16)
    bh = b.astype(BF16)
    bl = (b - bh.astype(F32)).astype(BF16)
    d = functools.partial(jnp.dot, preferred_element_type=F32)
    return d(ah, bh) + d(ah, bl) + d(al, bh)


def _silu(x):
    return x * jax.nn.sigmoid(x)


def _softplus(x):
    return jnp.maximum(x, 0.0) + jnp.log(1.0 + jnp.exp(-jnp.abs(x)))


def _log_sigmoid(x):
    return jnp.minimum(x, 0.0) - jnp.log(1.0 + jnp.exp(-jnp.abs(x)))


def _order_masks(n, reverse, reps=1):
    row = lax.broadcasted_iota(jnp.int32, (reps * n, n), 0) & (n - 1)
    col = lax.broadcasted_iota(jnp.int32, (reps * n, n), 1)
    d = (col - row) if reverse else (row - col)
    return d > 0, d >= 0


def _tri_inv_many(nmats, mm):
    n = nmats[0].shape[0]
    row = lax.broadcasted_iota(jnp.int32, (n, n), 0)
    col = lax.broadcasted_iota(jnp.int32, (n, n), 1)
    x = row ^ col
    eye = jnp.where(row == col, 1.0, 0.0).astype(F32)
    ts = [eye - jnp.where(x == 1, nm, 0.0) for nm in nmats]
    s = 1
    while (2 << s) <= n:
        lvl = (x >> s) == 1
        tcs = [mm(t, jnp.where(lvl, nm, 0.0)) for t, nm in zip(ts, nmats)]
        ts = [t - mm(tc, t) for t, tc in zip(ts, tcs)]
        s += 1
    return ts


def _lane_group_masks(width, group, count):
    lane = lax.broadcasted_iota(jnp.int32, (1, width), 1)
    return [((lane >= g * group) & (lane < (g + 1) * group)) for g in range(count)]


def _stack_heads(x, masks):
    return jnp.concatenate([jnp.where(m, x, 0.0) for m in masks], axis=0)


def _mod_kernel(c_ref, w_ref, b_ref, o_ref):
    s = _silu(c_ref[...])
    o_ref[...] = _mm3(s, w_ref[...]) + b_ref[...]


def _modulation(cvec, mod_w, mod_b):
    rows, d = cvec.shape
    n = mod_w.shape[1]
    tn = 1024
    return pl.pallas_call(
        _mod_kernel,
        out_shape=jax.ShapeDtypeStruct((rows, n), F32),
        grid=(n // tn,),
        in_specs=[pl.BlockSpec((rows, d), lambda j: (0, 0)),
                  pl.BlockSpec((d, tn), lambda j: (0, j)),
                  pl.BlockSpec((1, tn), lambda j: (0, j))],
        out_specs=pl.BlockSpec((rows, tn), lambda j: (0, j)),
        compiler_params=pltpu.CompilerParams(vmem_limit_bytes=VMEM_LIMIT),
        name="modulation",
    )(cvec, mod_w, mod_b.reshape(1, n))


def _prenorm(x, g, scale1p, shift):
    ms = jnp.mean(x * x, axis=-1, keepdims=True)
    return x * lax.rsqrt(ms + EPS) * g * scale1p + shift


def _seg_sum(x, bd_ref):
    return _mm_exact_r(x, bd_ref[...])


def _rope(x, cos, sin_signed):
    w = x.shape[-1]
    lane = lax.broadcasted_iota(jnp.int32, (1, w), 1)
    first = (lane & 31) < 16
    partner = jnp.where(first, pltpu.roll(x, w - 16, axis=1), pltpu.roll(x, 16, axis=1))
    return x * cos + partner * sin_signed


def _tile_lanes(x, reps):
    return jnp.concatenate([x] * reps, axis=-1) if reps > 1 else x


def _in0_kernel(use_rope, x_ref, sc_ref, sh_ref, g_ref, wslab_ref, wrg_ref, wq_ref, wk_ref, wv_ref,
                wag_ref, bdq_ref, bdk_ref, gq_ref, gk_ref, cos_ref, sin_ref,
                slab_ref, rg_ref, q_ref, k_ref, v_ref, ag_ref):
    h = _prenorm(x_ref[0], g_ref[...], sc_ref[0], sh_ref[0]).astype(BF16)
    d = functools.partial(jnp.dot, preferred_element_type=F32)
    slab_ref[0] = d(h, wslab_ref[...])
    rg_ref[0] = d(h, wrg_ref[...])
    v_ref[0] = d(h, wv_ref[...])
    ag_ref[0] = d(h, wag_ref[...])
    q = d(h, wq_ref[...])
    k = d(h, wk_ref[...])
    q = q * lax.rsqrt(_seg_sum(q * q, bdq_ref) * (1.0 / HEAD64) + EPS) * gq_ref[...]
    k = k * lax.rsqrt(_seg_sum(k * k, bdk_ref) * (1.0 / HEAD64) + EPS) * gk_ref[...]
    if use_rope:
        cos = cos_ref[...]
        sin = sin_ref[...]
        q = _rope(q, _tile_lanes(cos, q.shape[-1] // LANES), _tile_lanes(sin, q.shape[-1] // LANES))
        k = _rope(k, cos, sin)
    q_ref[0] = q
    k_ref[0] = k


def _const_spec(shape):
    nd = len(shape)
    return pl.BlockSpec(shape, lambda *_: (0,) * nd)


def _mod_spec(arr, d):
    if arr.shape[0] == 1:
        return pl.BlockSpec((1, 1, d), lambda b, i: (0, 0, 0))
    return pl.BlockSpec((1, 1, d), lambda b, i: (b, 0, 0))


def _in0(x, scale1p, shift, w, use_rope, tm):
    bsz, t, d = x.shape
    widths = [w["slab"].shape[1], 512, 512, 128, 128, 512]
    outs = [jax.ShapeDtypeStruct((bsz, t, n), F32) for n in widths]
    tok = lambda n: pl.BlockSpec((1, tm, n), lambda b, i: (b, i, 0))
    in_specs = [tok(d), _mod_spec(scale1p, d), _mod_spec(shift, d), _const_spec((1, d))]
    in_specs += [_const_spec(w[k].shape) for k in ("slab", "rg", "q", "k", "v", "ag", "bdq", "bdk", "gq", "gk")]
    in_specs += [pl.BlockSpec((tm, LANES), lambda b, i: (i, 0))] * 2
    return pl.pallas_call(
        functools.partial(_in0_kernel, use_rope),
        out_shape=outs,
        grid=(bsz, t // tm),
        in_specs=in_specs,
        out_specs=[tok(n) for n in widths],
        compiler_params=pltpu.CompilerParams(dimension_semantics=("parallel", "parallel"),
                                             vmem_limit_bytes=VMEM_LIMIT),
        name="in_proj0",
    )(x, scale1p, shift, w["g_pre"], w["slab"], w["rg"], w["q"], w["k"], w["v"], w["ag"],
      w["bdq"], w["bdk"], w["gq"], w["gk"], w["cos"], w["sin"])


def _neighbours(x, prev_blk, next_blk, i, n_blocks):
    tm = x.shape[0]
    row = lax.broadcasted_iota(jnp.int32, (tm, 1), 0)
    prev_row = jnp.where(i == 0, 0.0, prev_blk[SUBLANES - 1:SUBLANES, :])
    next_row = jnp.where(i == n_blocks - 1, 0.0, next_blk[0:1, :])
    prev = jnp.where(row == 0, prev_row, pltpu.roll(x, 1, axis=0))
    nxt = jnp.where(row == tm - 1, next_row, pltpu.roll(x, tm - 1, axis=0))
    return prev, nxt


def _halo_specs(tm, c, t):
    r = tm // SUBLANES
    last = t // SUBLANES - 1
    main = pl.BlockSpec((1, tm, c), lambda b, i: (b, i, 0))
    prev = pl.BlockSpec((1, SUBLANES, c), lambda b, i: (b, jnp.maximum(i * r - 1, 0), 0))
    nxt = pl.BlockSpec((1, SUBLANES, c), lambda b, i: (b, jnp.minimum((i + 1) * r, last), 0))
    return [main, prev, nxt]


def _rwprep_kernel(n_blocks, s_ref, sp_ref, sn_ref, mu_ref, w2_ref, w0_ref, a2_ref, a0_ref,
                   kk_ref, ka_ref, bd_ref,
                   r_ref, k_ref, v_ref, kkn_ref, b_ref, lw_ref):
    i = pl.program_id(1)
    x = s_ref[0]
    prev, nxt = _neighbours(x, sp_ref[0], sn_ref[0], i, n_blocks)
    xs = x + mu_ref[...] * (0.5 * (prev + nxt) - x)
    c = 512
    r, kr, vr = xs[:, :c], xs[:, c:2 * c], xs[:, 2 * c:3 * c]
    lw_in = xs[:, 3 * c:3 * c + LANES]
    la_in = xs[:, 3 * c + LANES:3 * c + 2 * LANES]
    lor = _mm(jnp.tanh(lw_in), w2_ref[...]) + w0_ref[...]
    logw = -RW_DECAY_SCALE * jax.nn.sigmoid(lor)
    a = jax.nn.sigmoid(a0_ref[...] + _mm(la_in, a2_ref[...]))
    kkp = kr * kk_ref[...]
    kkn = kkp * lax.rsqrt(_seg_sum(kkp * kkp, bd_ref) + EPS)
    r_ref[0] = r
    k_ref[0] = kr * (1.0 + (a - 1.0) * ka_ref[...])
    v_ref[0] = vr
    kkn_ref[0] = kkn
    b_ref[0] = kkn * a
    lw_ref[0, 0] = logw[:, :c]
    lw_ref[1, 0] = logw[:, c:]


def _rwprep(slab, w, tm):
    bsz, t, cs = slab.shape
    c = 512
    nb = t // tm
    tok = pl.BlockSpec((1, tm, c), lambda b, i: (b, i, 0))
    outs = [jax.ShapeDtypeStruct((bsz, t, c), F32)] * 5 + [jax.ShapeDtypeStruct((2, bsz, t, c), F32)]
    names = ("mu", "w2", "w0", "a2", "a0", "k_k", "k_a", "bd")
    return pl.pallas_call(
        functools.partial(_rwprep_kernel, nb),
        out_shape=outs,
        grid=(bsz, nb),
        in_specs=_halo_specs(tm, cs, t) + [_const_spec(w[k].shape) for k in names],
        out_specs=[tok] * 5 + [pl.BlockSpec((2, 1, tm, c), lambda b, i: (0, b, i, 0))],
        compiler_params=pltpu.CompilerParams(dimension_semantics=("parallel", "parallel"),
                                             vmem_limit_bytes=VMEM_LIMIT),
        name="rwkv_prep",
    )(slab, slab, slab, *[w[k] for k in names])


QUAD = 4 * HEAD64


def _sum_head_blocks(x, hmask, L):
    acc = jnp.where(hmask[0], x[0:L], 0.0)
    for h in range(1, len(hmask)):
        acc = acc + jnp.where(hmask[h], x[h * L:(h + 1) * L], 0.0)
    return acc


def _rwscan_kernel(nc, rf, kf, vf, kkf, bf, lwf, rb, kb, vb, kkb, bb, lwb, s0_ref,
                   of_ref, ob_ref, sf_ref, st_ref):
    c = pl.program_id(1)
    L = CHUNK
    nq = rf.shape[-1] // QUAD

    @pl.when(c == 0)
    def _():
        st_ref[...] = s0_ref[0]

    hmask = _lane_group_masks(QUAD, HEAD64, 4)
    rowq = lax.broadcasted_iota(jnp.int32, (QUAD, QUAD), 0)
    colq = lax.broadcasted_iota(jnp.int32, (QUAD, QUAD), 1)
    bd = (rowq >> 6) == (colq >> 6)
    masks = []
    for rev in (False, True):
        _, incl = _order_masks(L, rev)
        strict4, incl4 = _order_masks(L, rev, reps=4)
        masks.append((jnp.where(incl, 1.0, 0.0).astype(BF16), strict4, incl4))
    refs = ((rf, kf, vf, kkf, bf, lwf), (rb, kb, vb, kkb, bb, lwb))

    units = [(d, qd) for d in range(2) for qd in range(nq)]
    dat = []
    for d, qd in units:
        sl = slice(qd * QUAD, (qd + 1) * QUAD)
        r_, k_, v_, kk_, b_, lw_ = refs[d]
        dat.append(dict(r=r_[0, :, sl], k=k_[0, :, sl], v=v_[0, :, sl], kk=kk_[0, :, sl], b=b_[0, :, sl],
                        lw=lw_[0, 0, :, sl], s=st_ref[d, qd]))
    cws = [_mm_exact_l(masks[d][0], x["lw"]) for (d, _), x in zip(units, dat)]
    for x, cw in zip(dat, cws):
        c0 = cw[L // 2:L // 2 + 1, :]
        tot = jnp.sum(x["lw"], axis=0, keepdims=True)
        e_rel = jnp.exp(cw - c0)
        e_inv = jnp.exp(c0 - cw)
        ec0 = jnp.exp(c0)
        e_fin = jnp.exp(tot - c0)
        a_rel = x["kk"] * e_rel * jnp.exp(-x["lw"])
        r_rel = x["r"] * e_rel
        b_rel = x["b"] * e_inv
        k_rel = x["k"] * e_inv
        x.update(a_abs=a_rel * ec0, r_abs=r_rel * ec0, b_rel=b_rel, k_rel=k_rel,
                 bk_fin=jnp.concatenate([b_rel * e_fin, k_rel * e_fin], axis=0), w_tot=jnp.exp(tot),
                 sa=_stack_heads(a_rel, hmask), sr=_stack_heads(r_rel, hmask))
    for (d, _), x in zip(units, dat):
        _, strict4, incl4 = masks[d]
        x["n_ab"] = jnp.where(strict4, _mm_nt(x["sa"], x["b_rel"]), 0.0)
        x["m_ak"] = jnp.where(strict4, _mm_nt(x["sa"], x["k_rel"]), 0.0)
        x["p_rb"] = jnp.where(incl4, _mm_nt(x["sr"], x["b_rel"]), 0.0)
        x["p_rk"] = jnp.where(incl4, _mm_nt(x["sr"], x["k_rel"]), 0.0)
    tinv = _tri_inv_many([x["n_ab"][h * L:(h + 1) * L] for x in dat for h in range(4)], _mm)
    for i, x in enumerate(dat):
        x["rhs"] = -_mm_nt(x["a_abs"], x["s"]) - _sum_head_blocks(_mm(x["m_ak"], x["v"]), hmask, L)
        x["o_base"] = _mm_nt(x["r_abs"], x["s"])
        x["t4"] = jnp.concatenate(tinv[4 * i:4 * i + 4], axis=0)
    for x in dat:
        x["e"] = _sum_head_blocks(_mm(x["t4"], x["rhs"]), hmask, L)
    for x in dat:
        x["o"] = x["o_base"] + _sum_head_blocks(_mm(x["p_rb"], x["e"]) + _mm(x["p_rk"], x["v"]), hmask, L)
    for (d, qd), x in zip(units, dat):
        ev = jnp.concatenate([x["e"], x["v"]], axis=0)
        st_ref[d, qd] = x["s"] * x["w_tot"] + jnp.where(bd, _mm_tn(ev, x["bk_fin"]), 0.0)
    of_ref[0] = jnp.concatenate([dat[qd]["o"] for qd in range(nq)], axis=-1)
    ob_ref[0] = jnp.concatenate([dat[nq + qd]["o"] for qd in range(nq)], axis=-1)

    @pl.when(c == nc - 1)
    def _():
        sf_ref[0] = st_ref[...]


def _dir_specs(block, nc, lead=None):
    specs = []
    for d in range(2):
        idx = (lambda b_, c: c) if d == 0 else (lambda b_, c: nc - 1 - c)
        if lead is None:
            specs.append(pl.BlockSpec((1,) + block, functools.partial(
                lambda f, b_, c: (b_, f(b_, c)) + (0,) * (len(block) - 1), idx)))
        else:
            specs.append(pl.BlockSpec((1, 1) + block, functools.partial(
                lambda f, dd, b_, c: (dd, b_, f(b_, c)) + (0,) * (len(block) - 1), idx, d)))
    return specs


def _rwscan(r, k, v, kk, b, lw, s0):
    bsz, t, cdim = r.shape
    nc = t // CHUNK
    nq = cdim // QUAD
    tf, tb = _dir_specs((CHUNK, cdim), nc)
    lf, lb = _dir_specs((CHUNK, cdim), nc, lead=True)
    sspec = pl.BlockSpec((1, 2, nq, QUAD, QUAD), lambda b_, c: (b_, 0, 0, 0, 0))
    return pl.pallas_call(
        functools.partial(_rwscan_kernel, nc),
        out_shape=[jax.ShapeDtypeStruct((bsz, t, cdim), F32)] * 2
                  + [jax.ShapeDtypeStruct((bsz, 2, nq, QUAD, QUAD), F32)],
        grid=(bsz, nc),
        in_specs=[tf] * 5 + [lf] + [tb] * 5 + [lb] + [sspec],
        out_specs=[tf, tb, sspec],
        scratch_shapes=[pltpu.VMEM((2, nq, QUAD, QUAD), F32)],
        compiler_params=pltpu.CompilerParams(dimension_semantics=("parallel", "arbitrary"),
                                             vmem_limit_bytes=VMEM_LIMIT),
        name="rwkv_scan",
    )(r, k, v, kk, b, lw, r, k, v, kk, b, lw, s0)


def _attn_kernel(q_ref, k_ref, v_ref, o_ref):
    n_pairs = q_ref.shape[-1] // LANES
    pairs_per_kv = n_pairs // (k_ref.shape[1] // 2)
    outs = []
    for p in range(n_pairs):
        kv = p // pairs_per_kv
        qp = (q_ref[0, :, p * LANES:(p + 1) * LANES] * (HEAD64 ** -0.5)).astype(BF16)
        acc = None
        for par in range(2):
            kmat = k_ref[0, 2 * kv + par]
            vmat = v_ref[0, 2 * kv + par]
            s = lax.dot_general(qp, kmat, (((1,), (1,)), ((), ())), preferred_element_type=F32)
            m = jnp.max(s, axis=-1, keepdims=True)
            pexp = jnp.exp(s - m)
            l = jnp.sum(pexp, axis=-1, keepdims=True)
            pv = jnp.dot(pexp.astype(BF16), vmat, preferred_element_type=F32) / l
            acc = pv if acc is None else acc + pv
        outs.append(acc)
    o_ref[0] = jnp.concatenate(outs, axis=-1)


def _attention(q, kpad, vpad, tq):
    bsz, t, c = q.shape
    tk = kpad.shape[2]
    kvspec = pl.BlockSpec((1, kpad.shape[1], tk, LANES), lambda b, i: (b, 0, 0, 0))
    return pl.pallas_call(
        _attn_kernel,
        out_shape=jax.ShapeDtypeStruct((bsz, t, c), F32),
        grid=(bsz, t // tq),
        in_specs=[pl.BlockSpec((1, tq, c), lambda b, i: (b, i, 0)), kvspec, kvspec],
        out_specs=pl.BlockSpec((1, tq, c), lambda b, i: (b, i, 0)),
        compiler_params=pltpu.CompilerParams(dimension_semantics=("parallel", "parallel"),
                                             vmem_limit_bytes=VMEM_LIMIT),
        name="attention",
    )(q, kpad, vpad)


def _postnorm_residual(x, out, g_post, gate):
    ms = jnp.mean(out * out, axis=-1, keepdims=True)
    return x + gate * (out * lax.rsqrt(ms + EPS) * g_post)


def _out0_kernel(x_ref, gate_ref, of_ref, ob_ref, r_ref, k_ref, v_ref, rg_ref, oat_ref, ag_ref,
                 gng_ref, gnb_ref, rk_ref, bd_ref, wo_rw_ref, wo_at_ref, gpost_ref, y_ref):
    o = of_ref[0] + ob_ref[0]
    inv = 1.0 / HEAD64
    mu = _seg_sum(o, bd_ref) * inv
    dlt = o - mu
    var = _seg_sum(dlt * dlt, bd_ref) * inv
    gn = dlt * lax.rsqrt(var + GN_EPS) * gng_ref[...] + gnb_ref[...]
    bonus = _seg_sum(r_ref[0] * k_ref[0] * rk_ref[...], bd_ref) * v_ref[0]
    o_rw = (gn + bonus) * _silu(rg_ref[0])
    o_at = oat_ref[0] * _silu(ag_ref[0])
    out = _mm(o_rw, wo_rw_ref[...]) + _mm(o_at, wo_at_ref[...])
    y_ref[0] = _postnorm_residual(x_ref[0], out, gpost_ref[...], gate_ref[0])


def _out0(x, gate, o_f, o_b, r, k, v, rg, o_at, ag, w, tm):
    bsz, t, d = x.shape
    c = 512
    tokd = pl.BlockSpec((1, tm, d), lambda b, i: (b, i, 0))
    tok = pl.BlockSpec((1, tm, c), lambda b, i: (b, i, 0))
    names = ("gn_g", "gn_b", "r_k", "bd", "wo_rw", "wo_at", "g_post")
    return pl.pallas_call(
        _out0_kernel,
        out_shape=jax.ShapeDtypeStruct((bsz, t, d), F32),
        grid=(bsz, t // tm),
        in_specs=[tokd, _mod_spec(gate, d)] + [tok] * 8
                 + [_const_spec(w[n].shape) for n in names],
        out_specs=tokd,
        compiler_params=pltpu.CompilerParams(dimension_semantics=("parallel", "parallel"),
                                             vmem_limit_bytes=VMEM_LIMIT),
        name="out_proj0",
    )(x, gate, o_f, o_b, r, k, v, rg, o_at, ag, *[w[n] for n in names])


def _in1_kernel(x_ref, sc_ref, sh_ref, g_ref, wq_ref, wk_ref, wv_ref, wl_ref, wgg_ref, wd_ref, ws_ref,
                wdg_ref, g2_ref, gb_ref,
                q_ref, k_ref, v_ref, lw_ref, gg_ref, dqkv_ref, small_ref, dg_ref):
    h = _prenorm(x_ref[0], g_ref[...], sc_ref[0], sh_ref[0]).astype(BF16)
    d = functools.partial(jnp.dot, preferred_element_type=F32)
    q_ref[0] = d(h, wq_ref[...]) * (HEAD64 ** -0.5)
    k_ref[0] = d(h, wk_ref[...])
    v_ref[0] = d(h, wv_ref[...])
    gg_ref[0] = d(h, wgg_ref[...])
    dqkv_ref[0] = d(h, wd_ref[...])
    small_ref[0] = d(h, ws_ref[...])
    dg_ref[0] = d(h, wdg_ref[...])
    lora = d(h, wl_ref[...])
    lw = _log_sigmoid(_mm(lora, g2_ref[...]) + gb_ref[...]) * (1.0 / GLA_GATE_NORM)
    half = lw.shape[-1] // 2
    lw_ref[0, 0] = lw[:, :half]
    lw_ref[1, 0] = lw[:, half:]


def _in1(x, scale1p, shift, w, tm):
    bsz, t, d = x.shape
    widths = [256, 256, 512, None, 512, 1536, 128, 512]
    tok = lambda n: pl.BlockSpec((1, tm, n), lambda b, i: (b, i, 0))
    outs, ospecs = [], []
    for n in widths:
        if n is None:
            outs.append(jax.ShapeDtypeStruct((2, bsz, t, 256), F32))
            ospecs.append(pl.BlockSpec((2, 1, tm, 256), lambda b, i: (0, b, i, 0)))
        else:
            outs.append(jax.ShapeDtypeStruct((bsz, t, n), F32))
            ospecs.append(tok(n))
    names = ("q", "k", "v", "lora", "gg", "dqkv", "small", "dg", "g2", "gb")
    return pl.pallas_call(
        _in1_kernel,
        out_shape=outs,
        grid=(bsz, t // tm),
        in_specs=[tok(d), _mod_spec(scale1p, d), _mod_spec(shift, d), _const_spec((1, d))]
                 + [_const_spec(w[n].shape) for n in names],
        out_specs=ospecs,
        compiler_params=pltpu.CompilerParams(dimension_semantics=("parallel", "parallel"),
                                             vmem_limit_bytes=VMEM_LIMIT),
        name="in_proj1",
    )(x, scale1p, shift, w["g_pre"], *[w[n] for n in names])


def _gdnprep_kernel(n_blocks, x_ref, xp_ref, xn_ref, s_ref, cw_ref, alog_ref, dtb_ref,
                    q_ref, k_ref, v_ref, g_ref):
    i = pl.program_id(1)
    x = x_ref[0]
    prev, nxt = _neighbours(x, xp_ref[0], xn_ref[0], i, n_blocks)
    y = _silu(prev * cw_ref[0:1, :] + x * cw_ref[1:2, :] + nxt * cw_ref[2:3, :])
    c = 512
    qs, ks = [], []
    for h in range(c // HEAD128):
        qh = y[:, h * HEAD128:(h + 1) * HEAD128]
        kh = y[:, c + h * HEAD128:c + (h + 1) * HEAD128]
        qs.append(qh * lax.rsqrt(jnp.sum(qh * qh, axis=-1, keepdims=True) + EPS) * (HEAD128 ** -0.5))
        ks.append(kh * lax.rsqrt(jnp.sum(kh * kh, axis=-1, keepdims=True) + EPS))
    q_ref[0] = jnp.concatenate(qs, axis=-1)
    k_ref[0] = jnp.concatenate(ks, axis=-1)
    v_ref[0] = y[:, 2 * c:]
    s = s_ref[0]
    lane = lax.broadcasted_iota(jnp.int32, (1, LANES), 1)
    loga = -jnp.exp(alog_ref[...]) * _softplus(s + dtb_ref[...])
    g_ref[0] = jnp.where(lane < 8, loga, jax.nn.sigmoid(s))


def _gdnprep(dqkv, small, w, tm):
    bsz, t, cs = dqkv.shape
    c = 512
    nb = t // tm
    tok = pl.BlockSpec((1, tm, c), lambda b, i: (b, i, 0))
    tokl = pl.BlockSpec((1, tm, LANES), lambda b, i: (b, i, 0))
    names = ("conv", "alog", "dtb")
    return pl.pallas_call(
        functools.partial(_gdnprep_kernel, nb),
        out_shape=[jax.ShapeDtypeStruct((bsz, t, c), F32)] * 3 + [jax.ShapeDtypeStruct((bsz, t, LANES), F32)],
        grid=(bsz, nb),
        in_specs=_halo_specs(tm, cs, t) + [tokl] + [_const_spec(w[n].shape) for n in names],
        out_specs=[tok] * 3 + [tokl],
        compiler_params=pltpu.CompilerParams(dimension_semantics=("parallel", "parallel"),
                                             vmem_limit_bytes=VMEM_LIMIT),
        name="gdn_prep",
    )(dqkv, dqkv, dqkv, small, *[w[n] for n in names])


def _glascan_kernel(nc, qf, kf, vf, lwf, qb, kb, vb, lwb, s0_ref, of_ref, ob_ref, sf_ref, st_ref):
    c = pl.program_id(1)
    L = CHUNK
    nh = qf.shape[-1] // HEAD64

    @pl.when(c == 0)
    def _():
        st_ref[...] = s0_ref[0]

    hmask = _lane_group_masks(nh * HEAD64, HEAD64, nh)
    rowv = lax.broadcasted_iota(jnp.int32, st_ref.shape[1:], 0)
    colk = lax.broadcasted_iota(jnp.int32, st_ref.shape[1:], 1)
    bd = (rowv >> 7) == (colk >> 6)
    refs = ((qf, kf, vf, lwf, of_ref), (qb, kb, vb, lwb, ob_ref))
    dat = []
    for d in range(2):
        q_, k_, v_, lw_, _ = refs[d]
        _, incl = _order_masks(L, d == 1)
        _, incl4 = _order_masks(L, d == 1, reps=nh)
        dat.append(dict(q=q_[0], k=k_[0], v=v_[0], lw=lw_[0, 0], s=st_ref[d], incl4=incl4,
                        incl_bf=jnp.where(incl, 1.0, 0.0).astype(BF16)))
    cws = [_mm_exact_l(x["incl_bf"], x["lw"]) for x in dat]
    for x, cw in zip(dat, cws):
        c0 = cw[L // 2:L // 2 + 1, :]
        tot = jnp.sum(x["lw"], axis=0, keepdims=True)
        q_rel = x["q"] * jnp.exp(cw - c0)
        k_rel = x["k"] * jnp.exp(c0 - cw)
        x.update(sq=_stack_heads(q_rel, hmask), k_rel=k_rel, q_abs=q_rel * jnp.exp(c0),
                 k_fin=k_rel * jnp.exp(tot - c0), w_tot=jnp.exp(tot))
    for x in dat:
        x["att"] = jnp.where(x["incl4"], _mm_nt(x["sq"], x["k_rel"]), 0.0)
        x["inter"] = _mm_nt(x["q_abs"], x["s"])
    for d, x in enumerate(dat):
        intra = [_mm(x["att"][h * L:(h + 1) * L], x["v"][:, h * HEAD128:(h + 1) * HEAD128]) for h in range(nh)]
        refs[d][4][0] = jnp.concatenate(intra, axis=-1) + x["inter"]
        st_ref[d] = x["s"] * x["w_tot"] + jnp.where(bd, _mm_tn(x["v"], x["k_fin"]), 0.0)

    @pl.when(c == nc - 1)
    def _():
        sf_ref[0] = st_ref[...]


def _glascan(q, k, v, lw, s0):
    bsz, t, ck = q.shape
    cv = v.shape[-1]
    nc = t // CHUNK
    kf, kb = _dir_specs((CHUNK, ck), nc)
    vf, vb = _dir_specs((CHUNK, cv), nc)
    lf, lb = _dir_specs((CHUNK, ck), nc, lead=True)
    sspec = pl.BlockSpec((1, 2, cv, ck), lambda b_, c: (b_, 0, 0, 0))
    return pl.pallas_call(
        functools.partial(_glascan_kernel, nc),
        out_shape=[jax.ShapeDtypeStruct((bsz, t, cv), F32)] * 2 + [jax.ShapeDtypeStruct((bsz, 2, cv, ck), F32)],
        grid=(bsz, nc),
        in_specs=[kf, kf, vf, lf, kb, kb, vb, lb, sspec],
        out_specs=[vf, vb, sspec],
        scratch_shapes=[pltpu.VMEM((2, cv, ck), F32)],
        compiler_params=pltpu.CompilerParams(dimension_semantics=("parallel", "arbitrary"),
                                             vmem_limit_bytes=VMEM_LIMIT),
        name="gla_scan",
    )(q, k, v, lw, q, k, v, lw, s0)


def _gdnscan_kernel(nc, qf, kf, vf, gcf, grf, qb, kb, vb, gcb, grb, s0_ref, of_ref, ob_ref, sf_ref, st_ref):
    c = pl.program_id(1)
    L = CHUNK
    nh = qf.shape[-1] // HEAD128

    @pl.when(c == 0)
    def _():
        st_ref[...] = s0_ref[0]

    refs = ((qf, kf, vf, gcf, grf), (qb, kb, vb, gcb, grb))
    per_dir = []
    for d in range(2):
        strict, incl = _order_masks(L, d == 1)
        incl_bf = jnp.where(incl, 1.0, 0.0).astype(BF16)
        gcol = refs[d][3][0, 0]
        grow = refs[d][4][0, 0, 0]
        per_dir.append(dict(strict=strict, incl=incl, gcol=gcol,
                            cum_c=_mm_exact_l(incl_bf, gcol),
                            cum_r=_mm_exact_nt(grow, incl_bf),
                            tot=jnp.sum(gcol, axis=0, keepdims=True)))
    units = [(d, h) for d in range(2) for h in range(nh)]
    dat = []
    for d, h in units:
        hs = slice(h * HEAD128, (h + 1) * HEAD128)
        pd = per_dir[d]
        q, k, v = refs[d][0][0, :, hs], refs[d][1][0, :, hs], refs[d][2][0, :, hs]
        g = pd["cum_c"][:, h:h + 1]
        beta = pd["gcol"][:, nh + h:nh + h + 1]
        g_last = pd["tot"][:, h:h + 1]
        gam = jnp.exp(g)
        dat.append(dict(q=q, k=k, v=v, s=st_ref[d, h], beta=beta, gam=gam, pd=pd,
                        dec=jnp.exp(jnp.minimum(g - pd["cum_r"][h:h + 1, :], 0.0)),
                        rhs=jnp.concatenate([(beta * gam) * k, beta * v], axis=-1),
                        k_dec=k * jnp.exp(g_last - g), gl=jnp.exp(g_last)))
    for x in dat:
        x["kk"] = _mm_nt(x["k"], x["k"])
        x["qk"] = _mm_nt(x["q"], x["k"])
        x["qs"] = _mm(x["q"] * x["gam"], x["s"])
    tinv = _tri_inv_many([jnp.where(x["pd"]["strict"], x["dec"] * x["kk"] * x["beta"], 0.0) for x in dat], _mm)
    for x, t in zip(dat, tinv):
        x["wu"] = _mm(t, x["rhs"])
    for x in dat:
        x["u"] = x["wu"][:, HEAD128:] - _mm(x["wu"][:, :HEAD128], x["s"])
    for x in dat:
        aqk = jnp.where(x["pd"]["incl"], x["dec"] * x["qk"], 0.0)
        x["o"] = x["qs"] + _mm(aqk, x["u"])
    for (d, h), x in zip(units, dat):
        st_ref[d, h] = x["s"] * x["gl"] + _mm_tn(x["k_dec"], x["u"])
    of_ref[0] = jnp.concatenate([dat[h]["o"] for h in range(nh)], axis=-1)
    ob_ref[0] = jnp.concatenate([dat[nh + h]["o"] for h in range(nh)], axis=-1)

    @pl.when(c == nc - 1)
    def _():
        sf_ref[0] = st_ref[...]


def _gdnscan(q, k, v, gcol, grow, s0):
    bsz, t, cdim = q.shape
    nh = cdim // HEAD128
    nc = t // CHUNK
    tf, tb = _dir_specs((CHUNK, cdim), nc)
    gcf, gcb = _dir_specs((CHUNK, LANES), nc, lead=True)
    grf, grb = _dir_specs((1, 2 * nh, CHUNK), nc, lead=True)
    sspec = pl.BlockSpec((1, 2, nh, HEAD128, HEAD128), lambda b_, c: (b_, 0, 0, 0, 0))
    return pl.pallas_call(
        functools.partial(_gdnscan_kernel, nc),
        out_shape=[jax.ShapeDtypeStruct((bsz, t, cdim), F32)] * 2
                  + [jax.ShapeDtypeStruct((bsz, 2, nh, HEAD128, HEAD128), F32)],
        grid=(bsz, nc),
        in_specs=[tf, tf, tf, gcf, grf, tb, tb, tb, gcb, grb, sspec],
        out_specs=[tf, tb, sspec],
        scratch_shapes=[pltpu.VMEM((2, nh, HEAD128, HEAD128), F32)],
        compiler_params=pltpu.CompilerParams(dimension_semantics=("parallel", "arbitrary"),
                                             vmem_limit_bytes=VMEM_LIMIT),
        name="gdn_scan",
    )(q, k, v, gcol, grow, q, k, v, gcol, grow, s0)


def _head_rmsnorm(o, g):
    parts = []
    for h in range(o.shape[-1] // HEAD128):
        oh = o[:, h * HEAD128:(h + 1) * HEAD128]
        parts.append(oh * lax.rsqrt(jnp.mean(oh * oh, axis=-1, keepdims=True) + EPS) * g)
    return jnp.concatenate(parts, axis=-1)


def _out1_kernel(x_ref, gate_ref, glf_ref, glb_ref, gdf_ref, gdb_ref, gg_ref, dg_ref,
                 gla_g_ref, gdn_g_ref, wo_gla_ref, wo_gdn_ref, gpost_ref, y_ref):
    o_gla = _head_rmsnorm(glf_ref[0] + glb_ref[0], gla_g_ref[...]) * _silu(gg_ref[0])
    o_gdn = _head_rmsnorm(gdf_ref[0] + gdb_ref[0], gdn_g_ref[...]) * _silu(dg_ref[0])
    out = _mm(o_gla, wo_gla_ref[...]) + _mm(o_gdn, wo_gdn_ref[...])
    y_ref[0] = _postnorm_residual(x_ref[0], out, gpost_ref[...], gate_ref[0])


def _out1(x, gate, gla_f, gla_b, gdn_f, gdn_b, gg, dg, w, tm):
    bsz, t, d = x.shape
    c = 512
    tokd = pl.BlockSpec((1, tm, d), lambda b, i: (b, i, 0))
    tok = pl.BlockSpec((1, tm, c), lambda b, i: (b, i, 0))
    names = ("gla_g", "gdn_g", "wo_gla", "wo_gdn", "g_post")
    return pl.pallas_call(
        _out1_kernel,
        out_shape=jax.ShapeDtypeStruct((bsz, t, d), F32),
        grid=(bsz, t // tm),
        in_specs=[tokd, _mod_spec(gate, d)] + [tok] * 6
                 + [_const_spec(w[n].shape) for n in names],
        out_specs=tokd,
        compiler_params=pltpu.CompilerParams(dimension_semantics=("parallel", "parallel"),
                                             vmem_limit_bytes=VMEM_LIMIT),
        name="out_proj1",
    )(x, gate, gla_f, gla_b, gdn_f, gdn_b, gg, dg, *[w[n] for n in names])


def _row(v):
    return v.reshape(1, -1).astype(F32)


def _block_ones(n_groups, width):
    return jnp.asarray(np.kron(np.eye(n_groups, dtype=np.float32), np.ones((width, width), np.float32)), BF16)


def _rope_tables(t):
    rows = t // GRID_W
    row_id = np.repeat(np.arange(rows, dtype=np.float32), GRID_W)
    col_id = np.tile(np.arange(GRID_W, dtype=np.float32), rows)
    nf = HEAD64 // 4
    inv = jnp.asarray(ROPE_THETA, F32) ** (-jnp.arange(nf, dtype=F32) / nf)
    ang_r = jnp.asarray(row_id)[:, None] * inv[None, :]
    ang_c = jnp.asarray(col_id)[:, None] * inv[None, :]
    cos = jnp.concatenate([jnp.cos(ang_r)] * 2 + [jnp.cos(ang_c)] * 2, axis=-1)
    sin = jnp.concatenate([-jnp.sin(ang_r), jnp.sin(ang_r), -jnp.sin(ang_c), jnp.sin(ang_c)], axis=-1)
    return jnp.concatenate([cos] * 2, axis=-1), jnp.concatenate([sin] * 2, axis=-1)


def _pad_cols(w, n):
    return jnp.pad(w, ((0, 0), (0, n - w.shape[1])))


def _layer0_weights(p):
    w_in = p["w_in"]
    rw_w = 512
    slab_w = 3 * rw_w + 192
    cuts = np.cumsum([slab_w, 512, 512, 128, 128, 512])
    slab, rg, q, k, v, ag = jnp.split(w_in, [int(c) for c in cuts[:-1]], axis=1)
    w = {}
    w["slab"] = _pad_cols(slab, 3 * rw_w + 2 * LANES).astype(BF16)
    for n, a in (("rg", rg), ("q", q), ("k", k), ("v", v), ("ag", ag)):
        w[n] = a.astype(BF16)
    w["g_pre"] = _row(p["g_pre"])
    w["g_post"] = _row(p["g_post"])
    w["bdq"] = _block_ones(8, HEAD64)
    w["bdk"] = _block_ones(2, HEAD64)
    w["bd"] = w["bdq"]
    w["gq"] = _row(jnp.tile(p["at_gq"], 8))
    w["gk"] = _row(jnp.tile(p["at_gk"], 2))
    w["mu"] = _row(jnp.pad(p["rw_mu"], (0, LANES - 64)))
    z = jnp.zeros((64, rw_w), F32)
    w["w2"] = jnp.concatenate([jnp.concatenate([p["rw_w2_f"], z], axis=1),
                               jnp.concatenate([z, p["rw_w2_b"]], axis=1)], axis=0).astype(BF16)
    w["w0"] = _row(jnp.concatenate([p["rw_w0_f"], p["rw_w0_b"]]))
    w["a2"] = jnp.concatenate([p["rw_a2"], z], axis=0).astype(BF16)
    w["a0"] = _row(p["rw_a0"])
    w["k_k"] = _row(p["rw_k_k"])
    w["k_a"] = _row(p["rw_k_a"])
    w["r_k"] = _row(p["rw_r_k"])
    w["gn_g"] = _row(p["rw_gn_g"])
    w["gn_b"] = _row(p["rw_gn_b"])
    w["wo_rw"] = p["w_out"][:rw_w].astype(BF16)
    w["wo_at"] = p["w_out"][rw_w:].astype(BF16)
    return w


def _layer1_weights(p):
    w_in = p["w_in"]
    cuts = np.cumsum([256, 256, 512, 16, 16, 512, 1536, 4, 4, 4, 512])
    gq, gk, gv, gf, gb, gg, dqkv, af, ab, be, dg = jnp.split(w_in, [int(c) for c in cuts[:-1]], axis=1)
    w = {}
    for n, a in (("q", gq), ("k", gk), ("v", gv), ("gg", gg), ("dqkv", dqkv), ("dg", dg)):
        w[n] = a.astype(BF16)
    w["lora"] = _pad_cols(jnp.concatenate([gf, gb], axis=1), LANES).astype(BF16)
    w["small"] = _pad_cols(jnp.concatenate([af, ab, be], axis=1), LANES).astype(BF16)
    z = jnp.zeros((16, 256), F32)
    g2 = jnp.concatenate([jnp.concatenate([p["gla_g2_f"], z], axis=1),
                          jnp.concatenate([z, p["gla_g2_b"]], axis=1)], axis=0)
    w["g2"] = jnp.pad(g2, ((0, LANES - 32), (0, 0))).astype(BF16)
    w["gb"] = _row(jnp.concatenate([p["gla_gb_f"], p["gla_gb_b"]]))
    w["g_pre"] = _row(p["g_pre"])
    w["g_post"] = _row(p["g_post"])
    w["conv"] = p["gdn_conv"].astype(F32)
    w["alog"] = _row(jnp.pad(jnp.concatenate([p["gdn_A_log_f"], p["gdn_A_log_b"]]), (0, LANES - 8)))
    w["dtb"] = _row(jnp.pad(jnp.concatenate([p["gdn_dt_bias_f"], p["gdn_dt_bias_b"]]), (0, LANES - 8)))
    w["gla_g"] = _row(p["gla_norm_g"])
    w["gdn_g"] = _row(p["gdn_norm_g"])
    w["wo_gla"] = p["w_out"][:512].astype(BF16)
    w["wo_gdn"] = p["w_out"][512:].astype(BF16)
    return w


def _rw_state_to_bd(s):
    bsz = s.shape[0]
    s = s.reshape(bsz, 2, 4, HEAD64, HEAD64)
    eye = jnp.eye(4, dtype=s.dtype)
    return jnp.einsum("bqhvk,hg->bqhvgk", s, eye).reshape(bsz, 2, QUAD, QUAD)


def _rw_state_from_bd(s):
    bsz = s.shape[0]
    s = s.reshape(bsz, 2, 4, HEAD64, 4, HEAD64)
    return jnp.stack([s[:, :, h, :, h, :] for h in range(4)], axis=2).reshape(bsz, 8, HEAD64, HEAD64)


def _gla_state_to_bd(s):
    bsz = s.shape[0]
    eye = jnp.eye(4, dtype=s.dtype)
    return jnp.einsum("bhkv,hg->bhvgk", s, eye).reshape(bsz, 4 * HEAD128, 4 * HEAD64)


def _gla_state_from_bd(s):
    bsz = s.shape[0]
    s = s.reshape(bsz, 4, HEAD128, 4, HEAD64)
    return jnp.stack([jnp.swapaxes(s[:, h, :, h, :], -1, -2) for h in range(4)], axis=1)


def _pad_parity(x):
    z = jnp.zeros(x.shape[:-1] + (HEAD64,), x.dtype)
    parts = []
    for kv in range(2):
        xk = x[..., kv * HEAD64:(kv + 1) * HEAD64]
        parts += [jnp.concatenate([xk, z], axis=-1), jnp.concatenate([z, xk], axis=-1)]
    return jnp.stack(parts, axis=1).astype(BF16)


def _trunk(x, mods, w0, w1, ctx, tm, tq):
    bsz, t, _ = x.shape
    latent = ctx is not None

    sc, sh, gt = mods[0]
    slab, rg, q, k, v, ag = _in0(x, sc, sh, w0, latent, tm)
    r, kmod, vr, kkn, bvec, lw = _rwprep(slab, w0, tm)
    if latent:
        s0f, s0b, k_ctx, v_ctx = ctx[:4]
        s0 = jnp.stack([_rw_state_to_bd(s0f), _rw_state_to_bd(s0b)], axis=1)
        k_all = jnp.concatenate([k_ctx.reshape(bsz, -1, 2 * HEAD64), k], axis=1)
        v_all = jnp.concatenate([v_ctx.reshape(bsz, -1, 2 * HEAD64), v], axis=1)
    else:
        s0 = jnp.zeros((bsz, 2, 2, QUAD, QUAD), F32)
        k_all, v_all = k, v
    o_rwf, o_rwb, s_rw = _rwscan(r, kmod, vr, kkn, bvec, lw, s0)
    o_at = _attention(q, _pad_parity(k_all), _pad_parity(v_all), tq)
    x1 = _out0(x, gt, o_rwf, o_rwb, r, kmod, vr, rg, o_at, ag, w0, tm)

    sc, sh, gt = mods[1]
    gq, gk, gv, glw, gg, dqkv, small, dg = _in1(x1, sc, sh, w1, tm)
    dq, dk, dv, gsm = _gdnprep(dqkv, small, w1, tm)
    nc = t // CHUNK
    g8 = jnp.stack([jnp.concatenate([gsm[..., 0:4], gsm[..., 8:12]], axis=-1),
                    jnp.concatenate([gsm[..., 4:8], gsm[..., 8:12]], axis=-1)])
    gcol = jnp.pad(g8, ((0, 0), (0, 0), (0, 0), (0, LANES - 8)))
    grow = jnp.swapaxes(g8.reshape(2, bsz, nc, CHUNK, 8), -1, -2)
    if latent:
        sgf, sgb, sdf, sdb = ctx[4:]
        s0_gla = jnp.stack([_gla_state_to_bd(sgf), _gla_state_to_bd(sgb)], axis=1)
        s0_gdn = jnp.stack([sdf, sdb], axis=1)
    else:
        s0_gla = jnp.zeros((bsz, 2, 4 * HEAD128, 4 * HEAD64), F32)
        s0_gdn = jnp.zeros((bsz, 2, 4, HEAD128, HEAD128), F32)
    gla_f, gla_b, s_gla = _glascan(gq, gk, gv, glw, s0_gla)
    gdn_f, gdn_b, s_gdn = _gdnscan(dq, dk, dv, gcol, grow, s0_gdn)
    y = _out1(x1, gt, gla_f, gla_b, gdn_f, gdn_b, gg, dg, w1, tm)

    new = None
    if not latent:
        new = (_rw_state_from_bd(s_rw[:, 0]), _rw_state_from_bd(s_rw[:, 1]),
               k.reshape(bsz, t, 2, HEAD64), v.reshape(bsz, t, 2, HEAD64),
               _gla_state_from_bd(s_gla[:, 0]), _gla_state_from_bd(s_gla[:, 1]), s_gdn[:, 0], s_gdn[:, 1])
    return y, new


def _split_mod(m, d):
    shift, scale, gate = m[..., :d], m[..., d:2 * d], m[..., 2 * d:]
    return (1.0 + scale)[:, None, :], shift[:, None, :], gate[:, None, :]


def kernel(x_prompt, x_sample, state_l0_rwkv_fwd, state_l0_rwkv_bwd, cache_l0_k, cache_l0_v, state_l1_gla_fwd, state_l1_gla_bwd, state_l1_gdn_fwd, state_l1_gdn_bwd, c, c_ctx, l0_mod_w, l0_mod_b, l0_g_pre, l0_g_post, l0_w_in, l0_w_out, l0_rw_mu, l0_rw_w0_f, l0_rw_w2_f, l0_rw_w0_b, l0_rw_w2_b, l0_rw_a0, l0_rw_a2, l0_rw_k_k, l0_rw_k_a, l0_rw_r_k, l0_rw_gn_g, l0_rw_gn_b, l0_at_gq, l0_at_gk, l1_mod_w, l1_mod_b, l1_g_pre, l1_g_post, l1_w_in, l1_w_out, l1_gla_g2_f, l1_gla_gb_f, l1_gla_g2_b, l1_gla_gb_b, l1_gla_norm_g, l1_gdn_conv, l1_gdn_A_log_f, l1_gdn_dt_bias_f, l1_gdn_A_log_b, l1_gdn_dt_bias_b, l1_gdn_norm_g):
    p0 = {"g_pre": l0_g_pre, "g_post": l0_g_post, "w_in": l0_w_in, "w_out": l0_w_out, "rw_mu": l0_rw_mu,
          "rw_w0_f": l0_rw_w0_f, "rw_w2_f": l0_rw_w2_f, "rw_w0_b": l0_rw_w0_b, "rw_w2_b": l0_rw_w2_b,
          "rw_a0": l0_rw_a0, "rw_a2": l0_rw_a2, "rw_k_k": l0_rw_k_k, "rw_k_a": l0_rw_k_a,
          "rw_r_k": l0_rw_r_k, "rw_gn_g": l0_rw_gn_g, "rw_gn_b": l0_rw_gn_b,
          "at_gq": l0_at_gq, "at_gk": l0_at_gk}
    p1 = {"g_pre": l1_g_pre, "g_post": l1_g_post, "w_in": l1_w_in, "w_out": l1_w_out,
          "gla_g2_f": l1_gla_g2_f, "gla_gb_f": l1_gla_gb_f, "gla_g2_b": l1_gla_g2_b,
          "gla_gb_b": l1_gla_gb_b, "gla_norm_g": l1_gla_norm_g, "gdn_conv": l1_gdn_conv,
          "gdn_A_log_f": l1_gdn_A_log_f, "gdn_dt_bias_f": l1_gdn_dt_bias_f,
          "gdn_A_log_b": l1_gdn_A_log_b, "gdn_dt_bias_b": l1_gdn_dt_bias_b,
          "gdn_norm_g": l1_gdn_norm_g}
    d = x_prompt.shape[-1]
    nb = c.shape[0]
    w0 = _layer0_weights(p0)
    w1 = _layer1_weights(p1)
    cos, sin = _rope_tables(x_sample.shape[1])
    w0["cos"], w0["sin"] = cos, sin

    cvec = jnp.concatenate([c, c_ctx[None, :], jnp.zeros((SUBLANES - nb - 1, d), F32)], axis=0)
    m0 = _modulation(cvec, l0_mod_w, l0_mod_b)
    m1 = _modulation(cvec, l1_mod_w, l1_mod_b)
    mods_lat = [_split_mod(m[:nb], d) for m in (m0, m1)]
    mods_ctx = [_split_mod(m[nb:nb + 1], d) for m in (m0, m1)]

    y_prompt, new = _trunk(x_prompt, mods_ctx, w0, w1, None, 256, 256)
    ctx = (state_l0_rwkv_fwd, state_l0_rwkv_bwd, cache_l0_k, cache_l0_v,
           state_l1_gla_fwd, state_l1_gla_bwd, state_l1_gdn_fwd, state_l1_gdn_bwd)
    y_sample, _ = _trunk(x_sample, mods_lat, w0, w1, ctx, 256, 256)
    return (y_prompt, y_sample) + tuple(new)
```

```python
import functools

import numpy as np
import jax
import jax.numpy as jnp
from jax import lax
from jax.experimental import pallas as pl
from jax.experimental.pallas import tpu as pltpu

F32 = jnp.float32
BF16 = jnp.bfloat16

EPS = 1e-6
GN_EPS = 64e-5
CHUNK = 64
GRID_W = 64
ROPE_THETA = 10000.0
RW_DECAY_SCALE = 0.6065306597126334
GLA_GATE_NORM = 16.0
LOG2E = 1.4426950408889634
HEAD64 = 64
HEAD128 = 128
LANES = 128
SUBLANES = 8
VMEM_LIMIT = 56 * 1024 * 1024


def _mm(a, b):
    return jnp.dot(a.astype(BF16), b.astype(BF16), preferred_element_type=F32)


def _mm_nt(a, b):
    return lax.dot_general(a.astype(BF16), b.astype(BF16), (((1,), (1,)), ((), ())),
                           preferred_element_type=F32)


def _mm_tn(a, b):
    return lax.dot_general(a.astype(BF16), b.astype(BF16), (((0,), (0,)), ((), ())),
                           preferred_element_type=F32)


def _split3(x):
    hi = x.astype(BF16)
    r1 = x - hi.astype(F32)
    mid = r1.astype(BF16)
    lo = (r1 - mid.astype(F32)).astype(BF16)
    return hi, mid, lo


def _mm_exact_l(mask_bf16, x):
    hi, mid, lo = _split3(x)
    d = functools.partial(jnp.dot, preferred_element_type=F32)
    return d(mask_bf16, hi) + d(mask_bf16, mid) + d(mask_bf16, lo)


def _mm_exact_r(x, mask_bf16):
    hi, mid, lo = _split3(x)
    d = functools.partial(jnp.dot, preferred_element_type=F32)
    return d(hi, mask_bf16) + d(mid, mask_bf16) + d(lo, mask_bf16)


def _mm_exact_tn(x, mask_bf16):
    hi, mid, lo = _split3(x)
    d = lambda a: lax.dot_general(a, mask_bf16, (((0,), (0,)), ((), ())), preferred_element_type=F32)
    return d(hi) + d(mid) + d(lo)


def _mm3(a, b):
    ah = a.astype(BF16)
    al = (a - ah.astype(F32)).astype(BF16)
    bh = b.astype(BF16)
    bl = (b - bh.astype(F32)).astype(BF16)
    d = functools.partial(jnp.dot, preferred_element_type=F32)
    return d(ah, bh) + d(ah, bl) + d(al, bh)


def _silu(x):
    return x * jax.nn.sigmoid(x)


def _softplus(x):
    return jnp.maximum(x, 0.0) + jnp.log(1.0 + jnp.exp(-jnp.abs(x)))


def _log_sigmoid(x):
    return jnp.minimum(x, 0.0) - jnp.log(1.0 + jnp.exp(-jnp.abs(x)))


def _order_masks(n, reverse, reps=1):
    row = lax.broadcasted_iota(jnp.int32, (reps * n, n), 0) & (n - 1)
    col = lax.broadcasted_iota(jnp.int32, (reps * n, n), 1)
    d = (col - row) if reverse else (row - col)
    return d > 0, d >= 0


def _tri_inv_many(nmats, mm):
    n = nmats[0].shape[0]
    row = lax.broadcasted_iota(jnp.int32, (n, n), 0)
    col = lax.broadcasted_iota(jnp.int32, (n, n), 1)
    x = row ^ col
    eye = jnp.where(row == col, 1.0, 0.0).astype(F32)
    ts = [eye - jnp.where(x == 1, nm, 0.0) for nm in nmats]
    s = 1
    while (2 << s) <= n:
        lvl = (x >> s) == 1
        tcs = [mm(t, jnp.where(lvl, nm, 0.0)) for t, nm in zip(ts, nmats)]
        ts = [t - mm(tc, t) for t, tc in zip(ts, tcs)]
        s += 1
    return ts


def _lane_group_masks(width, group, count):
    lane = lax.broadcasted_iota(jnp.int32, (1, width), 1)
    return [((lane >= g * group) & (lane < (g + 1) * group)) for g in range(count)]


def _stack_heads(x, masks):
    return jnp.concatenate([jnp.where(m, x, 0.0) for m in masks], axis=0)


def _mod_kernel(c_ref, w_ref, b_ref, o_ref):
    s = _silu(c_ref[...])
    o_ref[...] = _mm3(s, w_ref[...]) + b_ref[...]


def _modulation(cvec, mod_w, mod_b):
    rows, d = cvec.shape
    n = mod_w.shape[1]
    tn = 1024
    return pl.pallas_call(
        _mod_kernel,
        out_shape=jax.ShapeDtypeStruct((rows, n), F32),
        grid=(n // tn,),
        in_specs=[pl.BlockSpec((rows, d), lambda j: (0, 0)),
                  pl.BlockSpec((d, tn), lambda j: (0, j)),
                  pl.BlockSpec((1, tn), lambda j: (0, j))],
        out_specs=pl.BlockSpec((rows, tn), lambda j: (0, j)),
        compiler_params=pltpu.CompilerParams(vmem_limit_bytes=VMEM_LIMIT),
        name="modulation",
    )(cvec, mod_w, mod_b.reshape(1, n))


def _prenorm(x, g, scale1p, shift):
    ms = jnp.mean(x * x, axis=-1, keepdims=True)
    return x * lax.rsqrt(ms + EPS) * g * scale1p + shift


def _seg_sum(x, bd_ref):
    return _mm_exact_r(x, bd_ref[...])


def _rope(x, cos, sin_signed):
    w = x.shape[-1]
    lane = lax.broadcasted_iota(jnp.int32, (1, w), 1)
    first = (lane & 31) < 16
    partner = jnp.where(first, pltpu.roll(x, w - 16, axis=1), pltpu.roll(x, 16, axis=1))
    return x * cos + partner * sin_signed


def _tile_lanes(x, reps):
    return jnp.concatenate([x] * reps, axis=-1) if reps > 1 else x


P0_G_PRE, P0_G_POST, P0_MU, P0_W0, P0_A0, P0_KK, P0_KA, P0_RK, P0_GNG, P0_GNB, P0_GQ, P0_GK = range(12)
W0_COLS = (0, 1792, 2304, 2816, 2944, 3072, 3584)


def _prow(p_ref, i, n):
    return p_ref[i:i + 1, 0:n]


def _in0_kernel(use_rope, x_ref, sc_ref, sh_ref, p_ref, w_ref, bdq_ref, bdk_ref, cos_ref, sin_ref,
                slab_ref, rg_ref, q_ref, k_ref, v_ref, ag_ref):
    dm = x_ref.shape[-1]
    h = _prenorm(x_ref[0], _prow(p_ref, P0_G_PRE, dm), sc_ref[0], sh_ref[0]).astype(BF16)
    c = W0_COLS
    proj = lambda j: jnp.dot(h, w_ref[:, c[j]:c[j + 1]], preferred_element_type=F32)
    slab_ref[0] = proj(0)
    rg_ref[0] = proj(1)
    v_ref[0] = proj(4)
    ag_ref[0] = proj(5)
    q = proj(2)
    k = proj(3)
    q = q * lax.rsqrt(_seg_sum(q * q, bdq_ref) * (1.0 / HEAD64) + EPS) * _prow(p_ref, P0_GQ, q.shape[-1])
    k = k * lax.rsqrt(_seg_sum(k * k, bdk_ref) * (1.0 / HEAD64) + EPS) * _prow(p_ref, P0_GK, k.shape[-1])
    if use_rope:
        cos = cos_ref[...]
        sin = sin_ref[...]
        q = _rope(q, _tile_lanes(cos, q.shape[-1] // LANES), _tile_lanes(sin, q.shape[-1] // LANES))
        k = _rope(k, cos, sin)
    q_ref[0] = q
    k_ref[0] = k


def _const_spec(shape, single=False):
    nd = len(shape)
    if single:
        return pl.BlockSpec(shape, lambda *_: (0,) * nd, pipeline_mode=pl.Buffered(1))
    return pl.BlockSpec(shape, lambda *_: (0,) * nd)


def _mod_spec(arr, d):
    if arr.shape[0] == 1:
        return pl.BlockSpec((1, 1, d), lambda b, i: (0, 0, 0))
    return pl.BlockSpec((1, 1, d), lambda b, i: (b, 0, 0))


def _in0(x, scale1p, shift, w, use_rope, tm):
    bsz, t, d = x.shape
    widths = [W0_COLS[j + 1] - W0_COLS[j] for j in range(6)]
    outs = [jax.ShapeDtypeStruct((bsz, t, n), F32) for n in widths]
    tok = lambda n: pl.BlockSpec((1, tm, n), lambda b, i: (b, i, 0))
    in_specs = [tok(d), _mod_spec(scale1p, d), _mod_spec(shift, d), _const_spec(w["p"].shape),
                _const_spec(w["win"].shape, single=True), _const_spec(w["bdq"].shape), _const_spec(w["bdk"].shape)]
    in_specs += [pl.BlockSpec((tm, LANES), lambda b, i: (i, 0))] * 2
    return pl.pallas_call(
        functools.partial(_in0_kernel, use_rope),
        out_shape=outs,
        grid=(bsz, t // tm),
        in_specs=in_specs,
        out_specs=[tok(n) for n in widths],
        compiler_params=pltpu.CompilerParams(dimension_semantics=("parallel", "parallel"),
                                             vmem_limit_bytes=VMEM_LIMIT),
        name="in_proj0",
    )(x, scale1p, shift, w["p"], w["win"], w["bdq"], w["bdk"], w["cos"], w["sin"])


def _neighbours(x, prev_blk, next_blk, i, n_blocks):
    tm = x.shape[0]
    row = lax.broadcasted_iota(jnp.int32, (tm, 1), 0)
    prev_row = jnp.where(i == 0, 0.0, prev_blk[SUBLANES - 1:SUBLANES, :])
    next_row = jnp.where(i == n_blocks - 1, 0.0, next_blk[0:1, :])
    prev = jnp.where(row == 0, prev_row, pltpu.roll(x, 1, axis=0))
    nxt = jnp.where(row == tm - 1, next_row, pltpu.roll(x, tm - 1, axis=0))
    return prev, nxt


def _halo_specs(tm, c, t):
    r = tm // SUBLANES
    last = t // SUBLANES - 1
    main = pl.BlockSpec((1, tm, c), lambda b, i: (b, i, 0))
    prev = pl.BlockSpec((1, SUBLANES, c), lambda b, i: (b, jnp.maximum(i * r - 1, 0), 0))
    nxt = pl.BlockSpec((1, SUBLANES, c), lambda b, i: (b, jnp.minimum((i + 1) * r, last), 0))
    return [main, prev, nxt]


def _rwprep_kernel(n_blocks, s_ref, sp_ref, sn_ref, p_ref, w2_ref, a2_ref, bd_ref,
                   r_ref, k_ref, v_ref, kkn_ref, b_ref, lw_ref):
    i = pl.program_id(1)
    x = s_ref[0]
    prev, nxt = _neighbours(x, sp_ref[0], sn_ref[0], i, n_blocks)
    xs = x + _prow(p_ref, P0_MU, x.shape[-1]) * (0.5 * (prev + nxt) - x)
    c = 512
    r, kr, vr = xs[:, :c], xs[:, c:2 * c], xs[:, 2 * c:3 * c]
    lw_in = xs[:, 3 * c:3 * c + LANES]
    la_in = xs[:, 3 * c + LANES:3 * c + 2 * LANES]
    lor = _mm(jnp.tanh(lw_in), w2_ref[...]) + _prow(p_ref, P0_W0, 2 * c)
    logw = -RW_DECAY_SCALE * jax.nn.sigmoid(lor)
    a = jax.nn.sigmoid(_prow(p_ref, P0_A0, c) + _mm(la_in, a2_ref[...]))
    kkp = kr * _prow(p_ref, P0_KK, c)
    kkn = kkp * lax.rsqrt(_seg_sum(kkp * kkp, bd_ref) + EPS)
    r_ref[0] = r
    k_ref[0] = kr * (1.0 + (a - 1.0) * _prow(p_ref, P0_KA, c))
    v_ref[0] = vr
    kkn_ref[0] = kkn
    b_ref[0] = kkn * a
    lw_ref[0, 0] = logw[:, :c]
    lw_ref[1, 0] = logw[:, c:]


def _rwprep(slab, w, tm):
    bsz, t, cs = slab.shape
    c = 512
    nb = t // tm
    tok = pl.BlockSpec((1, tm, c), lambda b, i: (b, i, 0))
    outs = [jax.ShapeDtypeStruct((bsz, t, c), F32)] * 5 + [jax.ShapeDtypeStruct((2, bsz, t, c), F32)]
    names = ("p", "w2", "a2", "bd")
    return pl.pallas_call(
        functools.partial(_rwprep_kernel, nb),
        out_shape=outs,
        grid=(bsz, nb),
        in_specs=_halo_specs(tm, cs, t) + [_const_spec(w[k].shape) for k in names],
        out_specs=[tok] * 5 + [pl.BlockSpec((2, 1, tm, c), lambda b, i: (0, b, i, 0))],
        compiler_params=pltpu.CompilerParams(dimension_semantics=("parallel", "parallel"),
                                             vmem_limit_bytes=VMEM_LIMIT),
        name="rwkv_prep",
    )(slab, slab, slab, *[w[k] for k in names])


QUAD = 4 * HEAD64


def _sum_head_blocks(x, hmask, L):
    acc = jnp.where(hmask[0], x[0:L], 0.0)
    for h in range(1, len(hmask)):
        acc = acc + jnp.where(hmask[h], x[h * L:(h + 1) * L], 0.0)
    return acc


def _rwscan_kernel(nc, rf, kf, vf, kkf, bf, lwf, rb, kb, vb, kkb, bb, lwb, s0_ref,
                   of_ref, ob_ref, sf_ref, st_ref):
    c = pl.program_id(1)
    L = CHUNK
    nq = rf.shape[-1] // QUAD

    @pl.when(c == 0)
    def _():
        st_ref[...] = s0_ref[0]

    hmask = _lane_group_masks(QUAD, HEAD64, 4)
    rowq = lax.broadcasted_iota(jnp.int32, (QUAD, QUAD), 0)
    colq = lax.broadcasted_iota(jnp.int32, (QUAD, QUAD), 1)
    bd = (rowq >> 6) == (colq >> 6)
    masks = []
    for rev in (False, True):
        _, incl = _order_masks(L, rev)
        strict4, incl4 = _order_masks(L, rev, reps=4)
        masks.append((jnp.where(incl, 1.0, 0.0).astype(BF16), strict4, incl4))
    refs = ((rf, kf, vf, kkf, bf, lwf), (rb, kb, vb, kkb, bb, lwb))

    units = [(d, qd) for d in range(2) for qd in range(nq)]
    dat = []
    for d, qd in units:
        sl = slice(qd * QUAD, (qd + 1) * QUAD)
        r_, k_, v_, kk_, b_, lw_ = refs[d]
        dat.append(dict(r=r_[0, :, sl], k=k_[0, :, sl], v=v_[0, :, sl], kk=kk_[0, :, sl], b=b_[0, :, sl],
                        lw=lw_[0, 0, :, sl], s=st_ref[d, qd]))
    cws = [_mm_exact_l(masks[d][0], x["lw"]) for (d, _), x in zip(units, dat)]
    for x, cw in zip(dat, cws):
        c0 = cw[L // 2:L // 2 + 1, :]
        tot = jnp.sum(x["lw"], axis=0, keepdims=True)
        e_rel = jnp.exp(cw - c0)
        e_inv = jnp.exp(c0 - cw)
        ec0 = jnp.exp(c0)
        e_fin = jnp.exp(tot - c0)
        a_rel = x["kk"] * e_rel * jnp.exp(-x["lw"])
        r_rel = x["r"] * e_rel
        b_rel = x["b"] * e_inv
        k_rel = x["k"] * e_inv
        x.update(a_abs=a_rel * ec0, r_abs=r_rel * ec0, b_rel=b_rel, k_rel=k_rel,
                 bk_fin=jnp.concatenate([b_rel * e_fin, k_rel * e_fin], axis=0), w_tot=jnp.exp(tot),
                 sa=_stack_heads(a_rel, hmask), sr=_stack_heads(r_rel, hmask))
    for (d, _), x in zip(units, dat):
        _, strict4, incl4 = masks[d]
        x["n_ab"] = jnp.where(strict4, _mm_nt(x["sa"], x["b_rel"]), 0.0)
        x["m_ak"] = jnp.where(strict4, _mm_nt(x["sa"], x["k_rel"]), 0.0)
        x["p_rb"] = jnp.where(incl4, _mm_nt(x["sr"], x["b_rel"]), 0.0)
        x["p_rk"] = jnp.where(incl4, _mm_nt(x["sr"], x["k_rel"]), 0.0)
    tinv = _tri_inv_many([x["n_ab"][h * L:(h + 1) * L] for x in dat for h in range(4)], _mm)
    for i, x in enumerate(dat):
        x["rhs"] = -_mm_nt(x["a_abs"], x["s"]) - _sum_head_blocks(_mm(x["m_ak"], x["v"]), hmask, L)
        x["o_base"] = _mm_nt(x["r_abs"], x["s"])
        x["t4"] = jnp.concatenate(tinv[4 * i:4 * i + 4], axis=0)
    for x in dat:
        x["e"] = _sum_head_blocks(_mm(x["t4"], x["rhs"]), hmask, L)
    for x in dat:
        x["o"] = x["o_base"] + _sum_head_blocks(_mm(x["p_rb"], x["e"]) + _mm(x["p_rk"], x["v"]), hmask, L)
    for (d, qd), x in zip(units, dat):
        ev = jnp.concatenate([x["e"], x["v"]], axis=0)
        st_ref[d, qd] = x["s"] * x["w_tot"] + jnp.where(bd, _mm_tn(ev, x["bk_fin"]), 0.0)
    of_ref[0] = jnp.concatenate([dat[qd]["o"] for qd in range(nq)], axis=-1)
    ob_ref[0] = jnp.concatenate([dat[nq + qd]["o"] for qd in range(nq)], axis=-1)

    @pl.when(c == nc - 1)
    def _():
        sf_ref[0] = st_ref[...]


def _dir_specs(block, nc, lead=None):
    specs = []
    for d in range(2):
        idx = (lambda b_, c: c) if d == 0 else (lambda b_, c: nc - 1 - c)
        if lead is None:
            specs.append(pl.BlockSpec((1,) + block, functools.partial(
                lambda f, b_, c: (b_, f(b_, c)) + (0,) * (len(block) - 1), idx)))
        else:
            specs.append(pl.BlockSpec((1, 1) + block, functools.partial(
                lambda f, dd, b_, c: (dd, b_, f(b_, c)) + (0,) * (len(block) - 1), idx, d)))
    return specs


def _rwscan(r, k, v, kk, b, lw, s0):
    bsz, t, cdim = r.shape
    nc = t // CHUNK
    nq = cdim // QUAD
    tf, tb = _dir_specs((CHUNK, cdim), nc)
    lf, lb = _dir_specs((CHUNK, cdim), nc, lead=True)
    sspec = pl.BlockSpec((1, 2, nq, QUAD, QUAD), lambda b_, c: (b_, 0, 0, 0, 0))
    return pl.pallas_call(
        functools.partial(_rwscan_kernel, nc),
        out_shape=[jax.ShapeDtypeStruct((bsz, t, cdim), F32)] * 2
                  + [jax.ShapeDtypeStruct((bsz, 2, nq, QUAD, QUAD), F32)],
        grid=(bsz, nc),
        in_specs=[tf] * 5 + [lf] + [tb] * 5 + [lb] + [sspec],
        out_specs=[tf, tb, sspec],
        scratch_shapes=[pltpu.VMEM((2, nq, QUAD, QUAD), F32)],
        compiler_params=pltpu.CompilerParams(dimension_semantics=("parallel", "arbitrary"),
                                             vmem_limit_bytes=VMEM_LIMIT),
        name="rwkv_scan",
    )(r, k, v, kk, b, lw, r, k, v, kk, b, lw, s0)


def _attn_kernel(n_kv, q_ref, k_ref, vt_ref, o_ref):
    n_heads = q_ref.shape[-1] // HEAD64
    group = n_heads // n_kv
    lane = lax.broadcasted_iota(jnp.int32, (1, LANES), 1)
    kmat = k_ref[0]
    ones = vt_ref[0, n_kv * HEAD64:, :]
    vts = [jnp.concatenate([vt_ref[0, kv * HEAD64:(kv + 1) * HEAD64, :], ones], axis=0) for kv in range(n_kv)]
    outs = []
    for h in range(n_heads):
        kv = h // group
        pair = q_ref[0, :, (h // 2) * LANES:(h // 2 + 1) * LANES] * (HEAD64 ** -0.5 * LOG2E)
        if (h % 2) != kv:
            pair = pltpu.roll(pair, HEAD64, axis=1)
        qh = jnp.where((lane >= kv * HEAD64) & (lane < (kv + 1) * HEAD64), pair, 0.0).astype(BF16)
        st = lax.dot_general(kmat, qh, (((1,), (1,)), ((), ())), preferred_element_type=F32)
        m = jnp.max(st, axis=0, keepdims=True)
        pt = jnp.exp2(st - m).astype(BF16)
        res = jnp.dot(vts[kv], pt, preferred_element_type=F32)
        outs.append(res[0:HEAD64] / res[HEAD64:HEAD64 + 1])
    o_ref[0] = jnp.concatenate(outs, axis=0).T


def _attention(q, k_all, vt_all, tq):
    bsz, t, c = q.shape
    tk = k_all.shape[1]
    n_kv = k_all.shape[-1] // HEAD64
    assert n_kv * HEAD64 == LANES
    return pl.pallas_call(
        functools.partial(_attn_kernel, n_kv),
        out_shape=jax.ShapeDtypeStruct((bsz, t, c), F32),
        grid=(bsz, t // tq),
        in_specs=[pl.BlockSpec((1, tq, c), lambda b, i: (b, i, 0)),
                  pl.BlockSpec((1, tk, LANES), lambda b, i: (b, 0, 0), pipeline_mode=pl.Buffered(1)),
                  pl.BlockSpec((1, vt_all.shape[1], tk), lambda b, i: (b, 0, 0), pipeline_mode=pl.Buffered(1))],
        out_specs=pl.BlockSpec((1, tq, c), lambda b, i: (b, i, 0)),
        compiler_params=pltpu.CompilerParams(dimension_semantics=("parallel", "parallel"),
                                             vmem_limit_bytes=VMEM_LIMIT),
        name="attention",
    )(q, k_all, vt_all)


def _postnorm_residual(x, out, g_post, gate):
    ms = jnp.mean(out * out, axis=-1, keepdims=True)
    return x + gate * (out * lax.rsqrt(ms + EPS) * g_post)


def _out0_kernel(x_ref, gate_ref, of_ref, ob_ref, r_ref, k_ref, v_ref, rg_ref, oat_ref, ag_ref,
                 p_ref, bd_ref, wo_ref, y_ref):
    o = of_ref[0] + ob_ref[0]
    c = o.shape[-1]
    inv = 1.0 / HEAD64
    mu = _seg_sum(o, bd_ref) * inv
    dlt = o - mu
    var = _seg_sum(dlt * dlt, bd_ref) * inv
    gn = dlt * lax.rsqrt(var + GN_EPS) * _prow(p_ref, P0_GNG, c) + _prow(p_ref, P0_GNB, c)
    bonus = _seg_sum(r_ref[0] * k_ref[0] * _prow(p_ref, P0_RK, c), bd_ref) * v_ref[0]
    o_rw = (gn + bonus) * _silu(rg_ref[0])
    o_at = oat_ref[0] * _silu(ag_ref[0])
    out = _mm(o_rw, wo_ref[0:c, :]) + _mm(o_at, wo_ref[c:2 * c, :])
    y_ref[0] = _postnorm_residual(x_ref[0], out, _prow(p_ref, P0_G_POST, x_ref.shape[-1]), gate_ref[0])


def _out0(x, gate, o_f, o_b, r, k, v, rg, o_at, ag, w, tm):
    bsz, t, d = x.shape
    c = 512
    tokd = pl.BlockSpec((1, tm, d), lambda b, i: (b, i, 0))
    tok = pl.BlockSpec((1, tm, c), lambda b, i: (b, i, 0))
    names = ("p", "bd", "wout")
    return pl.pallas_call(
        _out0_kernel,
        out_shape=jax.ShapeDtypeStruct((bsz, t, d), F32),
        grid=(bsz, t // tm),
        in_specs=[tokd, _mod_spec(gate, d)] + [tok] * 8
                 + [_const_spec(w[n].shape) for n in names],
        out_specs=tokd,
        compiler_params=pltpu.CompilerParams(dimension_semantics=("parallel", "parallel"),
                                             vmem_limit_bytes=VMEM_LIMIT),
        name="out_proj0",
    )(x, gate, o_f, o_b, r, k, v, rg, o_at, ag, *[w[n] for n in names])


(P1_G_PRE, P1_G_POST, P1_GB, P1_ALOG_F, P1_ALOG_B, P1_DTB_F, P1_DTB_B, P1_GLA_G, P1_GDN_G, P1_CONV) = range(10)
W1_COLS = (0, 256, 512, 1024, 1152, 1664, 3200, 3328, 3456, 3968)


def _in1_kernel(x_ref, sc_ref, sh_ref, p_ref, w_ref, g2_ref,
                q_ref, k_ref, v_ref, lw_ref, gg_ref, dqkv_ref, small_ref, dg_ref):
    dm = x_ref.shape[-1]
    h = _prenorm(x_ref[0], _prow(p_ref, P1_G_PRE, dm), sc_ref[0], sh_ref[0]).astype(BF16)
    c = W1_COLS
    proj = lambda j: jnp.dot(h, w_ref[:, c[j]:c[j + 1]], preferred_element_type=F32)
    q_ref[0] = proj(0) * (HEAD64 ** -0.5)
    k_ref[0] = proj(1)
    v_ref[0] = proj(2)
    gg_ref[0] = proj(4)
    dqkv_ref[0] = proj(5)
    small_ref[0, 0] = proj(6)
    small_ref[1, 0] = proj(7)
    dg_ref[0] = proj(8)
    lw = _log_sigmoid(_mm(proj(3), g2_ref[...]) + _prow(p_ref, P1_GB, g2_ref.shape[-1])) * (1.0 / GLA_GATE_NORM)
    half = lw.shape[-1] // 2
    lw_ref[0, 0] = lw[:, :half]
    lw_ref[1, 0] = lw[:, half:]


def _in1(x, scale1p, shift, w, tm):
    bsz, t, d = x.shape
    widths = [256, 256, 512, -256, 512, 1536, -LANES, 512]
    tok = lambda n: pl.BlockSpec((1, tm, n), lambda b, i: (b, i, 0))
    outs, ospecs = [], []
    for n in widths:
        if n < 0:
            outs.append(jax.ShapeDtypeStruct((2, bsz, t, -n), F32))
            ospecs.append(pl.BlockSpec((2, 1, tm, -n), lambda b, i: (0, b, i, 0)))
        else:
            outs.append(jax.ShapeDtypeStruct((bsz, t, n), F32))
            ospecs.append(tok(n))
    return pl.pallas_call(
        _in1_kernel,
        out_shape=outs,
        grid=(bsz, t // tm),
        in_specs=[tok(d), _mod_spec(scale1p, d), _mod_spec(shift, d), _const_spec(w["p"].shape),
                  _const_spec(w["win"].shape, single=True), _const_spec(w["g2"].shape)],
        out_specs=ospecs,
        compiler_params=pltpu.CompilerParams(dimension_semantics=("parallel", "parallel"),
                                             vmem_limit_bytes=VMEM_LIMIT),
        name="in_proj1",
    )(x, scale1p, shift, w["p"], w["win"], w["g2"])


def _gdnprep_kernel(n_blocks, x_ref, xp_ref, xn_ref, s_ref, p_ref, q_ref, k_ref, v_ref, g_ref):
    i = pl.program_id(1)
    x = x_ref[0]
    prev, nxt = _neighbours(x, xp_ref[0], xn_ref[0], i, n_blocks)
    cw = lambda j: _prow(p_ref, P1_CONV + j, x.shape[-1])
    y = _silu(prev * cw(0) + x * cw(1) + nxt * cw(2))
    c = 512
    qs, ks = [], []
    for h in range(c // HEAD128):
        qh = y[:, h * HEAD128:(h + 1) * HEAD128]
        kh = y[:, c + h * HEAD128:c + (h + 1) * HEAD128]
        qs.append(qh * lax.rsqrt(jnp.sum(qh * qh, axis=-1, keepdims=True) + EPS) * (HEAD128 ** -0.5))
        ks.append(kh * lax.rsqrt(jnp.sum(kh * kh, axis=-1, keepdims=True) + EPS))
    q_ref[0] = jnp.concatenate(qs, axis=-1)
    k_ref[0] = jnp.concatenate(ks, axis=-1)
    v_ref[0] = y[:, 2 * c:]
    lane = lax.broadcasted_iota(jnp.int32, (1, LANES), 1)
    nh = c // HEAD128
    for d, (ia, ib) in enumerate(((P1_ALOG_F, P1_DTB_F), (P1_ALOG_B, P1_DTB_B))):
        s = s_ref[d, 0]
        loga = -jnp.exp(_prow(p_ref, ia, LANES)) * _softplus(s + _prow(p_ref, ib, LANES))
        g_ref[d, 0] = jnp.where(lane < nh, loga, jnp.where(lane < 2 * nh, jax.nn.sigmoid(s), 0.0))


def _gdnprep(dqkv, small, w, tm):
    bsz, t, cs = dqkv.shape
    c = 512
    nb = t // tm
    tok = pl.BlockSpec((1, tm, c), lambda b, i: (b, i, 0))
    tokl = pl.BlockSpec((2, 1, tm, LANES), lambda b, i: (0, b, i, 0))
    return pl.pallas_call(
        functools.partial(_gdnprep_kernel, nb),
        out_shape=[jax.ShapeDtypeStruct((bsz, t, c), F32)] * 3 + [jax.ShapeDtypeStruct((2, bsz, t, LANES), F32)],
        grid=(bsz, nb),
        in_specs=_halo_specs(tm, cs, t) + [tokl, _const_spec(w["p"].shape)],
        out_specs=[tok] * 3 + [tokl],
        compiler_params=pltpu.CompilerParams(dimension_semantics=("parallel", "parallel"),
                                             vmem_limit_bytes=VMEM_LIMIT),
        name="gdn_prep",
    )(dqkv, dqkv, dqkv, small, w["p"])


def _glascan_kernel(nc, qf, kf, vf, lwf, qb, kb, vb, lwb, s0_ref, of_ref, ob_ref, sf_ref, st_ref):
    c = pl.program_id(1)
    L = CHUNK
    nh = qf.shape[-1] // HEAD64

    @pl.when(c == 0)
    def _():
        st_ref[...] = s0_ref[0]

    hmask = _lane_group_masks(nh * HEAD64, HEAD64, nh)
    rowv = lax.broadcasted_iota(jnp.int32, st_ref.shape[1:], 0)
    colk = lax.broadcasted_iota(jnp.int32, st_ref.shape[1:], 1)
    bd = (rowv >> 7) == (colk >> 6)
    refs = ((qf, kf, vf, lwf, of_ref), (qb, kb, vb, lwb, ob_ref))
    dat = []
    for d in range(2):
        q_, k_, v_, lw_, _ = refs[d]
        _, incl = _order_masks(L, d == 1)
        _, incl4 = _order_masks(L, d == 1, reps=nh)
        dat.append(dict(q=q_[0], k=k_[0], v=v_[0], lw=lw_[0, 0], s=st_ref[d], incl4=incl4,
                        incl_bf=jnp.where(incl, 1.0, 0.0).astype(BF16)))
    cws = [_mm_exact_l(x["incl_bf"], x["lw"]) for x in dat]
    for x, cw in zip(dat, cws):
        c0 = cw[L // 2:L // 2 + 1, :]
        tot = jnp.sum(x["lw"], axis=0, keepdims=True)
        q_rel = x["q"] * jnp.exp(cw - c0)
        k_rel = x["k"] * jnp.exp(c0 - cw)
        x.update(sq=_stack_heads(q_rel, hmask), k_rel=k_rel, q_abs=q_rel * jnp.exp(c0),
                 k_fin=k_rel * jnp.exp(tot - c0), w_tot=jnp.exp(tot))
    for x in dat:
        x["att"] = jnp.where(x["incl4"], _mm_nt(x["sq"], x["k_rel"]), 0.0)
        x["inter"] = _mm_nt(x["q_abs"], x["s"])
    for d, x in enumerate(dat):
        intra = [_mm(x["att"][h * L:(h + 1) * L], x["v"][:, h * HEAD128:(h + 1) * HEAD128]) for h in range(nh)]
        refs[d][4][0] = jnp.concatenate(intra, axis=-1) + x["inter"]
        st_ref[d] = x["s"] * x["w_tot"] + jnp.where(bd, _mm_tn(x["v"], x["k_fin"]), 0.0)

    @pl.when(c == nc - 1)
    def _():
        sf_ref[0] = st_ref[...]


def _glascan(q, k, v, lw, s0):
    bsz, t, ck = q.shape
    cv = v.shape[-1]
    nc = t // CHUNK
    kf, kb = _dir_specs((CHUNK, ck), nc)
    vf, vb = _dir_specs((CHUNK, cv), nc)
    lf, lb = _dir_specs((CHUNK, ck), nc, lead=True)
    sspec = pl.BlockSpec((1, 2, cv, ck), lambda b_, c: (b_, 0, 0, 0))
    return pl.pallas_call(
        functools.partial(_glascan_kernel, nc),
        out_shape=[jax.ShapeDtypeStruct((bsz, t, cv), F32)] * 2 + [jax.ShapeDtypeStruct((bsz, 2, cv, ck), F32)],
        grid=(bsz, nc),
        in_specs=[kf, kf, vf, lf, kb, kb, vb, lb, sspec],
        out_specs=[vf, vb, sspec],
        scratch_shapes=[pltpu.VMEM((2, cv, ck), F32)],
        compiler_params=pltpu.CompilerParams(dimension_semantics=("parallel", "arbitrary"),
                                             vmem_limit_bytes=VMEM_LIMIT),
        name="gla_scan",
    )(q, k, v, lw, q, k, v, lw, s0)


def _gdnscan_kernel(nc, qf, kf, vf, gcf, qb, kb, vb, gcb, s0_ref, of_ref, ob_ref, sf_ref, st_ref):
    c = pl.program_id(1)
    L = CHUNK
    nh = qf.shape[-1] // HEAD128

    @pl.when(c == 0)
    def _():
        st_ref[...] = s0_ref[0]

    refs = ((qf, kf, vf, gcf), (qb, kb, vb, gcb))
    per_dir = []
    for d in range(2):
        strict, incl = _order_masks(L, d == 1)
        _, incl_t = _order_masks(L, d == 0)
        incl_bf = jnp.where(incl, 1.0, 0.0).astype(BF16)
        incl_t_bf = jnp.where(incl_t, 1.0, 0.0).astype(BF16)
        gcol = refs[d][3][0, 0]
        per_dir.append(dict(strict=strict, incl=incl, gcol=gcol,
                            cum_c=_mm_exact_l(incl_bf, gcol),
                            cum_r=_mm_exact_tn(gcol, incl_t_bf),
                            tot=jnp.sum(gcol, axis=0, keepdims=True)))
    units = [(d, h) for d in range(2) for h in range(nh)]
    dat = []
    for d, h in units:
        hs = slice(h * HEAD128, (h + 1) * HEAD128)
        pd = per_dir[d]
        q, k, v = refs[d][0][0, :, hs], refs[d][1][0, :, hs], refs[d][2][0, :, hs]
        g = pd["cum_c"][:, h:h + 1]
        beta = pd["gcol"][:, nh + h:nh + h + 1]
        g_last = pd["tot"][:, h:h + 1]
        gam = jnp.exp(g)
        dat.append(dict(q=q, k=k, v=v, s=st_ref[d, h], beta=beta, gam=gam, pd=pd,
                        dec=jnp.exp(jnp.minimum(g - pd["cum_r"][h:h + 1, :], 0.0)),
                        rhs=jnp.concatenate([(beta * gam) * k, beta * v], axis=-1),
                        k_dec=k * jnp.exp(g_last - g), gl=jnp.exp(g_last)))
    for x in dat:
        x["kk"] = _mm_nt(x["k"], x["k"])
        x["qk"] = _mm_nt(x["q"], x["k"])
        x["qs"] = _mm(x["q"] * x["gam"], x["s"])
    tinv = _tri_inv_many([jnp.where(x["pd"]["strict"], x["dec"] * x["kk"] * x["beta"], 0.0) for x in dat], _mm)
    for x, t in zip(dat, tinv):
        x["wu"] = _mm(t, x["rhs"])
    for x in dat:
        x["u"] = x["wu"][:, HEAD128:] - _mm(x["wu"][:, :HEAD128], x["s"])
    for x in dat:
        aqk = jnp.where(x["pd"]["incl"], x["dec"] * x["qk"], 0.0)
        x["o"] = x["qs"] + _mm(aqk, x["u"])
    for (d, h), x in zip(units, dat):
        st_ref[d, h] = x["s"] * x["gl"] + _mm_tn(x["k_dec"], x["u"])
    of_ref[0] = jnp.concatenate([dat[h]["o"] for h in range(nh)], axis=-1)
    ob_ref[0] = jnp.concatenate([dat[nh + h]["o"] for h in range(nh)], axis=-1)

    @pl.when(c == nc - 1)
    def _():
        sf_ref[0] = st_ref[...]


def _gdnscan(q, k, v, gcol, s0):
    bsz, t, cdim = q.shape
    nh = cdim // HEAD128
    nc = t // CHUNK
    tf, tb = _dir_specs((CHUNK, cdim), nc)
    gcf, gcb = _dir_specs((CHUNK, LANES), nc, lead=True)
    sspec = pl.BlockSpec((1, 2, nh, HEAD128, HEAD128), lambda b_, c: (b_, 0, 0, 0, 0))
    return pl.pallas_call(
        functools.partial(_gdnscan_kernel, nc),
        out_shape=[jax.ShapeDtypeStruct((bsz, t, cdim), F32)] * 2
                  + [jax.ShapeDtypeStruct((bsz, 2, nh, HEAD128, HEAD128), F32)],
        grid=(bsz, nc),
        in_specs=[tf, tf, tf, gcf, tb, tb, tb, gcb, sspec],
        out_specs=[tf, tb, sspec],
        scratch_shapes=[pltpu.VMEM((2, nh, HEAD128, HEAD128), F32)],
        compiler_params=pltpu.CompilerParams(dimension_semantics=("parallel", "arbitrary"),
                                             vmem_limit_bytes=VMEM_LIMIT),
        name="gdn_scan",
    )(q, k, v, gcol, q, k, v, gcol, s0)


def _head_rmsnorm(o, g):
    parts = []
    for h in range(o.shape[-1] // HEAD128):
        oh = o[:, h * HEAD128:(h + 1) * HEAD128]
        parts.append(oh * lax.rsqrt(jnp.mean(oh * oh, axis=-1, keepdims=True) + EPS) * g)
    return jnp.concatenate(parts, axis=-1)


def _out1_kernel(x_ref, gate_ref, glf_ref, glb_ref, gdf_ref, gdb_ref, gg_ref, dg_ref, p_ref, wo_ref, y_ref):
    c = gg_ref.shape[-1]
    o_gla = _head_rmsnorm(glf_ref[0] + glb_ref[0], _prow(p_ref, P1_GLA_G, HEAD128)) * _silu(gg_ref[0])
    o_gdn = _head_rmsnorm(gdf_ref[0] + gdb_ref[0], _prow(p_ref, P1_GDN_G, HEAD128)) * _silu(dg_ref[0])
    out = _mm(o_gla, wo_ref[0:c, :]) + _mm(o_gdn, wo_ref[c:2 * c, :])
    y_ref[0] = _postnorm_residual(x_ref[0], out, _prow(p_ref, P1_G_POST, x_ref.shape[-1]), gate_ref[0])


def _out1(x, gate, gla_f, gla_b, gdn_f, gdn_b, gg, dg, w, tm):
    bsz, t, d = x.shape
    c = 512
    tokd = pl.BlockSpec((1, tm, d), lambda b, i: (b, i, 0))
    tok = pl.BlockSpec((1, tm, c), lambda b, i: (b, i, 0))
    names = ("p", "wout")
    return pl.pallas_call(
        _out1_kernel,
        out_shape=jax.ShapeDtypeStruct((bsz, t, d), F32),
        grid=(bsz, t // tm),
        in_specs=[tokd, _mod_spec(gate, d)] + [tok] * 6
                 + [_const_spec(w[n].shape) for n in names],
        out_specs=tokd,
        compiler_params=pltpu.CompilerParams(dimension_semantics=("parallel", "parallel"),
                                             vmem_limit_bytes=VMEM_LIMIT),
        name="out_proj1",
    )(x, gate, gla_f, gla_b, gdn_f, gdn_b, gg, dg, *[w[n] for n in names])


def _block_ones(n_groups, width):
    return jnp.asarray(np.kron(np.eye(n_groups, dtype=np.float32), np.ones((width, width), np.float32)), BF16)


def _rope_tables(t):
    rows = t // GRID_W
    row_id = np.repeat(np.arange(rows, dtype=np.float32), GRID_W)
    col_id = np.tile(np.arange(GRID_W, dtype=np.float32), rows)
    nf = HEAD64 // 4
    inv = np.float32(ROPE_THETA) ** (-np.arange(nf, dtype=np.float32) / np.float32(nf))
    ang_r = (row_id[:, None] * inv[None, :]).astype(np.float32)
    ang_c = (col_id[:, None] * inv[None, :]).astype(np.float32)
    cos = np.concatenate([np.cos(ang_r)] * 2 + [np.cos(ang_c)] * 2, axis=-1)
    sin = np.concatenate([-np.sin(ang_r), np.sin(ang_r), -np.sin(ang_c), np.sin(ang_c)], axis=-1)
    tile2 = lambda a: jnp.asarray(np.concatenate([a, a], axis=-1).astype(np.float32))
    return tile2(cos), tile2(sin)


def _pack_rows(rows, width):
    return jnp.stack([jnp.pad(r.reshape(-1).astype(F32), (0, width - r.size)) for r in rows])


def _layer0_weights(p):
    w_in = p["w_in"]
    rw_w = 512
    slab_w = 3 * rw_w + 192
    d = w_in.shape[0]
    w = {}
    w["win"] = jnp.concatenate([w_in[:, :slab_w], jnp.zeros((d, W0_COLS[1] - slab_w), F32), w_in[:, slab_w:]],
                               axis=1).astype(BF16)
    w["wout"] = p["w_out"].astype(BF16)
    w["bdq"] = _block_ones(8, HEAD64)
    w["bdk"] = _block_ones(2, HEAD64)
    w["bd"] = w["bdq"]
    w["p"] = _pack_rows([p["g_pre"], p["g_post"], p["rw_mu"], jnp.concatenate([p["rw_w0_f"], p["rw_w0_b"]]),
                         p["rw_a0"], p["rw_k_k"], p["rw_k_a"], p["rw_r_k"], p["rw_gn_g"], p["rw_gn_b"],
                         jnp.tile(p["at_gq"], 8), jnp.tile(p["at_gk"], 2)], W0_COLS[1])
    z = jnp.zeros((64, rw_w), F32)
    w["w2"] = jnp.concatenate([jnp.concatenate([p["rw_w2_f"], z], axis=1),
                               jnp.concatenate([z, p["rw_w2_b"]], axis=1)], axis=0).astype(BF16)
    w["a2"] = jnp.concatenate([p["rw_a2"], z], axis=0).astype(BF16)
    return w


def _layer1_weights(p):
    w_in = p["w_in"]
    d = w_in.shape[0]
    zc = lambda n: jnp.zeros((d, n), F32)
    af, ab, be = w_in[:, 3104:3108], w_in[:, 3108:3112], w_in[:, 3112:3116]
    w = {}
    w["win"] = jnp.concatenate([w_in[:, :1024], w_in[:, 1024:1056], zc(LANES - 32), w_in[:, 1056:3104],
                                af, be, zc(LANES - 8), ab, be, zc(LANES - 8), w_in[:, 3116:]],
                               axis=1).astype(BF16)
    w["wout"] = p["w_out"].astype(BF16)
    z = jnp.zeros((16, 256), F32)
    g2 = jnp.concatenate([jnp.concatenate([p["gla_g2_f"], z], axis=1),
                          jnp.concatenate([z, p["gla_g2_b"]], axis=1)], axis=0)
    w["g2"] = jnp.pad(g2, ((0, LANES - 32), (0, 0))).astype(BF16)
    conv = p["gdn_conv"]
    w["p"] = _pack_rows([p["g_pre"], p["g_post"], jnp.concatenate([p["gla_gb_f"], p["gla_gb_b"]]),
                         p["gdn_A_log_f"], p["gdn_A_log_b"], p["gdn_dt_bias_f"], p["gdn_dt_bias_b"],
                         p["gla_norm_g"], p["gdn_norm_g"], conv[0], conv[1], conv[2]], conv.shape[1])
    return w


def _rw_state_to_bd(s):
    bsz = s.shape[0]
    s = s.reshape(bsz, 2, 4, HEAD64, HEAD64)
    eye = jnp.eye(4, dtype=s.dtype)
    return jnp.einsum("bqhvk,hg->bqhvgk", s, eye).reshape(bsz, 2, QUAD, QUAD)


def _rw_state_from_bd(s):
    bsz = s.shape[0]
    s = s.reshape(bsz, 2, 4, HEAD64, 4, HEAD64)
    return jnp.stack([s[:, :, h, :, h, :] for h in range(4)], axis=2).reshape(bsz, 8, HEAD64, HEAD64)


def _gla_state_to_bd(s):
    bsz = s.shape[0]
    eye = jnp.eye(4, dtype=s.dtype)
    return jnp.einsum("bhkv,hg->bhvgk", s, eye).reshape(bsz, 4 * HEAD128, 4 * HEAD64)


def _gla_state_from_bd(s):
    bsz = s.shape[0]
    s = s.reshape(bsz, 4, HEAD128, 4, HEAD64)
    return jnp.stack([jnp.swapaxes(s[:, h, :, h, :], -1, -2) for h in range(4)], axis=1)


def _trunk(x, mods, w0, w1, ctx, tm, tq):
    bsz, t, dm = x.shape
    latent = ctx is not None
    ts = min(tm, t)
    shared = mods[0][0].shape[0] == 1
    flat = (lambda a: a.reshape(a.shape[:-3] + (1, bsz * t, a.shape[-1]))) if shared else (lambda a: a)
    unflat = lambda a: a.reshape(a.shape[:-3] + (bsz, t, a.shape[-1]))

    sc, sh, gt = mods[0]
    slab, rg, q, k, v, ag = [unflat(a) for a in _in0(flat(x), sc, sh, w0, latent, tm)]
    r, kmod, vr, kkn, bvec, lw = _rwprep(slab, w0, ts)
    if latent:
        s0f, s0b, k_ctx, v_ctx = ctx[:4]
        s0 = jnp.stack([_rw_state_to_bd(s0f), _rw_state_to_bd(s0b)], axis=1)
        k_all = jnp.concatenate([k_ctx.reshape(bsz, -1, 2 * HEAD64), k], axis=1)
        v_all = jnp.concatenate([v_ctx.reshape(bsz, -1, 2 * HEAD64), v], axis=1)
    else:
        s0 = jnp.zeros((bsz, 2, 2, QUAD, QUAD), F32)
        k_all, v_all = k, v
    o_rwf, o_rwb, s_rw = _rwscan(r, kmod, vr, kkn, bvec, lw, s0)
    vt_all = jnp.concatenate([jnp.swapaxes(v_all, 1, 2), jnp.ones((bsz, 16, v_all.shape[1]), F32)], axis=1)
    o_at = _attention(q, k_all.astype(BF16), vt_all.astype(BF16), tq)
    x1 = _out0(flat(x), gt, *[flat(a) for a in (o_rwf, o_rwb, r, kmod, vr, rg, o_at, ag)], w0, tm)

    sc, sh, gt = mods[1]
    gq, gk, gv, glw, gg, dqkv, small, dg = [unflat(a) for a in _in1(x1, sc, sh, w1, tm)]
    dq, dk, dv, gcol = _gdnprep(dqkv, small, w1, ts)
    if latent:
        sgf, sgb, sdf, sdb = ctx[4:]
        s0_gla = jnp.stack([_gla_state_to_bd(sgf), _gla_state_to_bd(sgb)], axis=1)
        s0_gdn = jnp.stack([sdf, sdb], axis=1)
    else:
        s0_gla = jnp.zeros((bsz, 2, 4 * HEAD128, 4 * HEAD64), F32)
        s0_gdn = jnp.zeros((bsz, 2, 4, HEAD128, HEAD128), F32)
    gla_f, gla_b, s_gla = _glascan(gq, gk, gv, glw, s0_gla)
    gdn_f, gdn_b, s_gdn = _gdnscan(dq, dk, dv, gcol, s0_gdn)
    y = unflat(_out1(x1, gt, *[flat(a) for a in (gla_f, gla_b, gdn_f, gdn_b, gg, dg)], w1, tm))

    new = None
    if not latent:
        new = (_rw_state_from_bd(s_rw[:, 0]), _rw_state_from_bd(s_rw[:, 1]),
               k.reshape(bsz, t, 2, HEAD64), v.reshape(bsz, t, 2, HEAD64),
               _gla_state_from_bd(s_gla[:, 0]), _gla_state_from_bd(s_gla[:, 1]), s_gdn[:, 0], s_gdn[:, 1])
    return y, new


def _split_mod(m, d):
    shift, scale, gate = m[..., :d], m[..., d:2 * d], m[..., 2 * d:]
    return (1.0 + scale)[:, None, :], shift[:, None, :], gate[:, None, :]


def kernel(x_prompt, x_sample, state_l0_rwkv_fwd, state_l0_rwkv_bwd, cache_l0_k, cache_l0_v, state_l1_gla_fwd, state_l1_gla_bwd, state_l1_gdn_fwd, state_l1_gdn_bwd, c, c_ctx, l0_mod_w, l0_mod_b, l0_g_pre, l0_g_post, l0_w_in, l0_w_out, l0_rw_mu, l0_rw_w0_f, l0_rw_w2_f, l0_rw_w0_b, l0_rw_w2_b, l0_rw_a0, l0_rw_a2, l0_rw_k_k, l0_rw_k_a, l0_rw_r_k, l0_rw_gn_g, l0_rw_gn_b, l0_at_gq, l0_at_gk, l1_mod_w, l1_mod_b, l1_g_pre, l1_g_post, l1_w_in, l1_w_out, l1_gla_g2_f, l1_gla_gb_f, l1_gla_g2_b, l1_gla_gb_b, l1_gla_norm_g, l1_gdn_conv, l1_gdn_A_log_f, l1_gdn_dt_bias_f, l1_gdn_A_log_b, l1_gdn_dt_bias_b, l1_gdn_norm_g):
    p0 = {"g_pre": l0_g_pre, "g_post": l0_g_post, "w_in": l0_w_in, "w_out": l0_w_out, "rw_mu": l0_rw_mu,
          "rw_w0_f": l0_rw_w0_f, "rw_w2_f": l0_rw_w2_f, "rw_w0_b": l0_rw_w0_b, "rw_w2_b": l0_rw_w2_b,
          "rw_a0": l0_rw_a0, "rw_a2": l0_rw_a2, "rw_k_k": l0_rw_k_k, "rw_k_a": l0_rw_k_a,
          "rw_r_k": l0_rw_r_k, "rw_gn_g": l0_rw_gn_g, "rw_gn_b": l0_rw_gn_b,
          "at_gq": l0_at_gq, "at_gk": l0_at_gk}
    p1 = {"g_pre": l1_g_pre, "g_post": l1_g_post, "w_in": l1_w_in, "w_out": l1_w_out,
          "gla_g2_f": l1_gla_g2_f, "gla_gb_f": l1_gla_gb_f, "gla_g2_b": l1_gla_g2_b,
          "gla_gb_b": l1_gla_gb_b, "gla_norm_g": l1_gla_norm_g, "gdn_conv": l1_gdn_conv,
          "gdn_A_log_f": l1_gdn_A_log_f, "gdn_dt_bias_f": l1_gdn_dt_bias_f,
          "gdn_A_log_b": l1_gdn_A_log_b, "gdn_dt_bias_b": l1_gdn_dt_bias_b,
          "gdn_norm_g": l1_gdn_norm_g}
    d = x_prompt.shape[-1]
    nb = c.shape[0]
    w0 = _layer0_weights(p0)
    w1 = _layer1_weights(p1)
    cos, sin = _rope_tables(x_sample.shape[1])
    w0["cos"], w0["sin"] = cos, sin

    cvec = jnp.concatenate([c, c_ctx[None, :], jnp.zeros((SUBLANES - nb - 1, d), F32)], axis=0)
    m0 = _modulation(cvec, l0_mod_w, l0_mod_b)
    m1 = _modulation(cvec, l1_mod_w, l1_mod_b)
    mods_lat = [_split_mod(m[:nb], d) for m in (m0, m1)]
    mods_ctx = [_split_mod(m[nb:nb + 1], d) for m in (m0, m1)]

    tm = 512
    y_prompt, new = _trunk(x_prompt, mods_ctx, w0, w1, None, tm, min(512, x_prompt.shape[1]))
    ctx = (state_l0_rwkv_fwd, state_l0_rwkv_bwd, cache_l0_k, cache_l0_v,
           state_l1_gla_fwd, state_l1_gla_bwd, state_l1_gdn_fwd, state_l1_gdn_bwd)
    y_sample, _ = _trunk(x_sample, mods_lat, w0, w1, ctx, tm, min(512, x_sample.shape[1]))
    return (y_prompt, y_sample) + tuple(new)
```

```python
import functools

import numpy as np
import jax
import jax.numpy as jnp
from jax import lax
from jax.experimental import pallas as pl
from jax.experimental.pallas import tpu as pltpu

F32 = jnp.float32
BF16 = jnp.bfloat16

EPS = 1e-6
GN_EPS = 64e-5
CHUNK = 64
GRID_W = 64
ROPE_THETA = 10000.0
RW_DECAY_SCALE = 0.6065306597126334
GLA_GATE_NORM = 16.0
LOG2E = 1.4426950408889634
HEAD64 = 64
HEAD128 = 128
LANES = 128
SUBLANES = 8
VMEM_LIMIT = 56 * 1024 * 1024


def _mm(a, b):
    return jnp.dot(a.astype(BF16), b.astype(BF16), preferred_element_type=F32)


def _mm_nt(a, b):
    return lax.dot_general(a.astype(BF16), b.astype(BF16), (((1,), (1,)), ((), ())),
                           preferred_element_type=F32)


def _mm_tn(a, b):
    return lax.dot_general(a.astype(BF16), b.astype(BF16), (((0,), (0,)), ((), ())),
                           preferred_element_type=F32)


def _split3(x):
    hi = x.astype(BF16)
    r1 = x - hi.astype(F32)
    mid = r1.astype(BF16)
    lo = (r1 - mid.astype(F32)).astype(BF16)
    return hi, mid, lo


def _mm_exact_l(mask_bf16, x):
    hi, mid, lo = _split3(x)
    d = functools.partial(jnp.dot, preferred_element_type=F32)
    return d(mask_bf16, hi) + d(mask_bf16, mid) + d(mask_bf16, lo)


def _mm_exact_r(x, mask_bf16):
    hi, mid, lo = _split3(x)
    d = functools.partial(jnp.dot, preferred_element_type=F32)
    return d(hi, mask_bf16) + d(mid, mask_bf16) + d(lo, mask_bf16)


def _mm_exact_tn(x, mask_bf16):
    hi, mid, lo = _split3(x)
    d = lambda a: lax.dot_general(a, mask_bf16, (((0,), (0,)), ((), ())), preferred_element_type=F32)
    return d(hi) + d(mid) + d(lo)


def _mm3(a, b):
    ah = a.astype(BF16)
    al = (a - ah.astype(F32)).astype(BF16)
    bh = b.astype(BF16)
    bl = (b - bh.astype(F32)).astype(BF16)
    d = functools.partial(jnp.dot, preferred_element_type=F32)
    return d(ah, bh) + d(ah, bl) + d(al, bh)


def _silu(x):
    return x * jax.nn.sigmoid(x)


def _softplus(x):
    return jnp.maximum(x, 0.0) + jnp.log(1.0 + jnp.exp(-jnp.abs(x)))


def _log_sigmoid(x):
    return jnp.minimum(x, 0.0) - jnp.log(1.0 + jnp.exp(-jnp.abs(x)))


def _order_masks(n, reverse, reps=1):
    row = lax.broadcasted_iota(jnp.int32, (reps * n, n), 0) & (n - 1)
    col = lax.broadcasted_iota(jnp.int32, (reps * n, n), 1)
    d = (col - row) if reverse else (row - col)
    return d > 0, d >= 0


def _tri_inv_many(nmats, mm):
    n = nmats[0].shape[0]
    row = lax.broadcasted_iota(jnp.int32, (n, n), 0)
    col = lax.broadcasted_iota(jnp.int32, (n, n), 1)
    x = row ^ col
    eye = jnp.where(row == col, 1.0, 0.0).astype(F32)
    ts = [eye - jnp.where(x == 1, nm, 0.0) for nm in nmats]
    s = 1
    while (2 << s) <= n:
        lvl = (x >> s) == 1
        tcs = [mm(t, jnp.where(lvl, nm, 0.0)) for t, nm in zip(ts, nmats)]
        ts = [t - mm(tc, t) for t, tc in zip(ts, tcs)]
        s += 1
    return ts


def _lane_group_masks(width, group, count):
    lane = lax.broadcasted_iota(jnp.int32, (1, width), 1)
    return [((lane >= g * group) & (lane < (g + 1) * group)) for g in range(count)]


def _stack_heads(x, masks):
    return jnp.concatenate([jnp.where(m, x, 0.0) for m in masks], axis=0)


def _mod_kernel(c_ref, w_ref, b_ref, o_ref):
    s = _silu(c_ref[...])
    o_ref[...] = _mm3(s, w_ref[...]) + b_ref[...]


def _modulation(cvec, mod_w, mod_b):
    rows, d = cvec.shape
    n = mod_w.shape[1]
    tn = 1024
    return pl.pallas_call(
        _mod_kernel,
        out_shape=jax.ShapeDtypeStruct((rows, n), F32),
        grid=(n // tn,),
        in_specs=[pl.BlockSpec((rows, d), lambda j: (0, 0)),
                  pl.BlockSpec((d, tn), lambda j: (0, j)),
                  pl.BlockSpec((1, tn), lambda j: (0, j))],
        out_specs=pl.BlockSpec((rows, tn), lambda j: (0, j)),
        compiler_params=pltpu.CompilerParams(vmem_limit_bytes=VMEM_LIMIT),
        name="modulation",
    )(cvec, mod_w, mod_b.reshape(1, n))


def _prenorm(x, g, scale1p, shift):
    ms = jnp.mean(x * x, axis=-1, keepdims=True)
    return x * lax.rsqrt(ms + EPS) * g * scale1p + shift


def _seg_sum(x, bd_ref):
    return _mm_exact_r(x, bd_ref[...])


def _rope(x, cos, sin_signed):
    w = x.shape[-1]
    lane = lax.broadcasted_iota(jnp.int32, (1, w), 1)
    first = (lane & 31) < 16
    partner = jnp.where(first, pltpu.roll(x, w - 16, axis=1), pltpu.roll(x, 16, axis=1))
    return x * cos + partner * sin_signed


def _tile_lanes(x, reps):
    return jnp.concatenate([x] * reps, axis=-1) if reps > 1 else x


P0_G_PRE, P0_G_POST, P0_MU, P0_W0, P0_A0, P0_KK, P0_KA, P0_RK, P0_GNG, P0_GNB, P0_GQ, P0_GK = range(12)
W0_COLS = (0, 1792, 2304, 2816, 2944, 3072, 3584)


def _prow(p_ref, i, n):
    return p_ref[i:i + 1, 0:n]


def _in0_kernel(use_rope, x_ref, sc_ref, sh_ref, p_ref, w_ref, bdq_ref, bdk_ref, cos_ref, sin_ref,
                slab_ref, rg_ref, q_ref, k_ref, v_ref, ag_ref):
    dm = x_ref.shape[-1]
    h = _prenorm(x_ref[0], _prow(p_ref, P0_G_PRE, dm), sc_ref[0], sh_ref[0]).astype(BF16)
    c = W0_COLS
    proj = lambda j: jnp.dot(h, w_ref[:, c[j]:c[j + 1]], preferred_element_type=F32)
    slab_ref[0] = proj(0)
    rg_ref[0] = proj(1)
    v_ref[0] = proj(4)
    ag_ref[0] = proj(5)
    q = proj(2)
    k = proj(3)
    q = q * lax.rsqrt(_seg_sum(q * q, bdq_ref) * (1.0 / HEAD64) + EPS) * _prow(p_ref, P0_GQ, q.shape[-1])
    k = k * lax.rsqrt(_seg_sum(k * k, bdk_ref) * (1.0 / HEAD64) + EPS) * _prow(p_ref, P0_GK, k.shape[-1])
    if use_rope:
        cos = cos_ref[...]
        sin = sin_ref[...]
        q = _rope(q, _tile_lanes(cos, q.shape[-1] // LANES), _tile_lanes(sin, q.shape[-1] // LANES))
        k = _rope(k, cos, sin)
    q_ref[0] = q
    k_ref[0] = k


def _const_spec(shape, single=False):
    nd = len(shape)
    if single:
        return pl.BlockSpec(shape, lambda *_: (0,) * nd, pipeline_mode=pl.Buffered(1))
    return pl.BlockSpec(shape, lambda *_: (0,) * nd)


def _mod_spec(arr, d):
    if arr.shape[0] == 1:
        return pl.BlockSpec((1, 1, d), lambda b, i: (0, 0, 0))
    return pl.BlockSpec((1, 1, d), lambda b, i: (b, 0, 0))


def _in0(x, scale1p, shift, w, use_rope, tm):
    bsz, t, d = x.shape
    widths = [W0_COLS[j + 1] - W0_COLS[j] for j in range(6)]
    outs = [jax.ShapeDtypeStruct((bsz, t, n), F32) for n in widths]
    tok = lambda n: pl.BlockSpec((1, tm, n), lambda b, i: (b, i, 0))
    in_specs = [tok(d), _mod_spec(scale1p, d), _mod_spec(shift, d), _const_spec(w["p"].shape),
                _const_spec(w["win"].shape, single=True), _const_spec(w["bdq"].shape), _const_spec(w["bdk"].shape)]
    in_specs += [pl.BlockSpec((tm, LANES), lambda b, i: (i, 0))] * 2
    return pl.pallas_call(
        functools.partial(_in0_kernel, use_rope),
        out_shape=outs,
        grid=(bsz, t // tm),
        in_specs=in_specs,
        out_specs=[tok(n) for n in widths],
        compiler_params=pltpu.CompilerParams(dimension_semantics=("parallel", "parallel"),
                                             vmem_limit_bytes=VMEM_LIMIT),
        name="in_proj0",
    )(x, scale1p, shift, w["p"], w["win"], w["bdq"], w["bdk"], w["cos"], w["sin"])


def _neighbours(x, prev_blk, next_blk, i, n_blocks):
    tm = x.shape[0]
    row = lax.broadcasted_iota(jnp.int32, (tm, 1), 0)
    prev_row = jnp.where(i == 0, 0.0, prev_blk[SUBLANES - 1:SUBLANES, :])
    next_row = jnp.where(i == n_blocks - 1, 0.0, next_blk[0:1, :])
    prev = jnp.where(row == 0, prev_row, pltpu.roll(x, 1, axis=0))
    nxt = jnp.where(row == tm - 1, next_row, pltpu.roll(x, tm - 1, axis=0))
    return prev, nxt


def _halo_specs(tm, c, t):
    r = tm // SUBLANES
    last = t // SUBLANES - 1
    main = pl.BlockSpec((1, tm, c), lambda b, i: (b, i, 0))
    prev = pl.BlockSpec((1, SUBLANES, c), lambda b, i: (b, jnp.maximum(i * r - 1, 0), 0))
    nxt = pl.BlockSpec((1, SUBLANES, c), lambda b, i: (b, jnp.minimum((i + 1) * r, last), 0))
    return [main, prev, nxt]


def _rwprep_kernel(n_blocks, s_ref, sp_ref, sn_ref, p_ref, w2_ref, a2_ref, bd_ref,
                   r_ref, k_ref, v_ref, kkn_ref, b_ref, lw_ref):
    i = pl.program_id(1)
    x = s_ref[0]
    prev, nxt = _neighbours(x, sp_ref[0], sn_ref[0], i, n_blocks)
    xs = x + _prow(p_ref, P0_MU, x.shape[-1]) * (0.5 * (prev + nxt) - x)
    c = 512
    r, kr, vr = xs[:, :c], xs[:, c:2 * c], xs[:, 2 * c:3 * c]
    lw_in = xs[:, 3 * c:3 * c + LANES]
    la_in = xs[:, 3 * c + LANES:3 * c + 2 * LANES]
    lor = _mm(jnp.tanh(lw_in), w2_ref[...]) + _prow(p_ref, P0_W0, 2 * c)
    logw = -RW_DECAY_SCALE * jax.nn.sigmoid(lor)
    a = jax.nn.sigmoid(_prow(p_ref, P0_A0, c) + _mm(la_in, a2_ref[...]))
    kkp = kr * _prow(p_ref, P0_KK, c)
    kkn = kkp * lax.rsqrt(_seg_sum(kkp * kkp, bd_ref) + EPS)
    r_ref[0] = r
    k_ref[0] = kr * (1.0 + (a - 1.0) * _prow(p_ref, P0_KA, c))
    v_ref[0] = vr
    kkn_ref[0] = kkn
    b_ref[0] = kkn * a
    lw_ref[0, 0] = logw[:, :c]
    lw_ref[1, 0] = logw[:, c:]


def _rwprep(slab, w, tm):
    bsz, t, cs = slab.shape
    c = 512
    nb = t // tm
    tok = pl.BlockSpec((1, tm, c), lambda b, i: (b, i, 0))
    outs = [jax.ShapeDtypeStruct((bsz, t, c), F32)] * 5 + [jax.ShapeDtypeStruct((2, bsz, t, c), F32)]
    names = ("p", "w2", "a2", "bd")
    return pl.pallas_call(
        functools.partial(_rwprep_kernel, nb),
        out_shape=outs,
        grid=(bsz, nb),
        in_specs=_halo_specs(tm, cs, t) + [_const_spec(w[k].shape) for k in names],
        out_specs=[tok] * 5 + [pl.BlockSpec((2, 1, tm, c), lambda b, i: (0, b, i, 0))],
        compiler_params=pltpu.CompilerParams(dimension_semantics=("parallel", "parallel"),
                                             vmem_limit_bytes=VMEM_LIMIT),
        name="rwkv_prep",
    )(slab, slab, slab, *[w[k] for k in names])


QUAD = 4 * HEAD64


def _sum_head_blocks(x, hmask, L):
    acc = jnp.where(hmask[0], x[0:L], 0.0)
    for h in range(1, len(hmask)):
        acc = acc + jnp.where(hmask[h], x[h * L:(h + 1) * L], 0.0)
    return acc


SCAN_SUB = 2


def _rwscan_kernel(nc, nsub, rf, kf, vf, kkf, bf, lwf, rb, kb, vb, kkb, bb, lwb, s0_ref,
                   of_ref, ob_ref, sf_ref, st_ref):
    c = pl.program_id(1)
    L = CHUNK
    nq = rf.shape[-1] // QUAD

    @pl.when(c == 0)
    def _():
        st_ref[...] = s0_ref[0]

    hmask = _lane_group_masks(QUAD, HEAD64, 4)
    rowq = lax.broadcasted_iota(jnp.int32, (QUAD, QUAD), 0)
    colq = lax.broadcasted_iota(jnp.int32, (QUAD, QUAD), 1)
    bd = (rowq >> 6) == (colq >> 6)
    masks = []
    for rev in (False, True):
        _, incl = _order_masks(L, rev)
        strict4, incl4 = _order_masks(L, rev, reps=4)
        masks.append((jnp.where(incl, 1.0, 0.0).astype(BF16), strict4, incl4))
    refs = ((rf, kf, vf, kkf, bf, lwf), (rb, kb, vb, kkb, bb, lwb))

    rows = lambda d, s: slice(s * L, (s + 1) * L) if d == 0 else slice((nsub - 1 - s) * L, (nsub - s) * L)
    units = [(s, d, qd) for s in range(nsub) for d in range(2) for qd in range(nq)]

    dat = []
    for s, d, qd in units:
        sl = slice(qd * QUAD, (qd + 1) * QUAD)
        rw = rows(d, s)
        r_, k_, v_, kk_, b_, lw_ = refs[d]
        dat.append(dict(r=r_[0, rw, sl], k=k_[0, rw, sl], v=v_[0, rw, sl], kk=kk_[0, rw, sl], b=b_[0, rw, sl],
                        lw=lw_[0, 0, rw, sl]))
    cws = [_mm_exact_l(masks[d][0], x["lw"]) for (_, d, _), x in zip(units, dat)]
    for x, cw in zip(dat, cws):
        c0 = cw[L // 2:L // 2 + 1, :]
        tot = jnp.sum(x["lw"], axis=0, keepdims=True)
        e_rel = jnp.exp(cw - c0)
        e_inv = jnp.exp(c0 - cw)
        ec0 = jnp.exp(c0)
        e_fin = jnp.exp(tot - c0)
        a_rel = x["kk"] * e_rel * jnp.exp(-x["lw"])
        r_rel = x["r"] * e_rel
        b_rel = x["b"] * e_inv
        k_rel = x["k"] * e_inv
        x.update(ar_abs=jnp.concatenate([a_rel * ec0, r_rel * ec0], axis=0), b_rel=b_rel, k_rel=k_rel,
                 b_fin=b_rel * e_fin, k_fin=k_rel * e_fin, w_tot=jnp.exp(tot),
                 sa=_stack_heads(a_rel, hmask), sr=_stack_heads(r_rel, hmask))
    for (_, d, _), x in zip(units, dat):
        _, strict4, incl4 = masks[d]
        x["n_ab"] = jnp.where(strict4, _mm_nt(x["sa"], x["b_rel"]), 0.0)
        x["m_ak"] = jnp.where(strict4, _mm_nt(x["sa"], x["k_rel"]), 0.0)
        x["p_rb"] = jnp.where(incl4, _mm_nt(x["sr"], x["b_rel"]), 0.0)
        x["p_rk"] = jnp.where(incl4, _mm_nt(x["sr"], x["k_rel"]), 0.0)
    tinv = _tri_inv_many([x["n_ab"][h * L:(h + 1) * L] for x in dat for h in range(4)], _mm)
    for i, x in enumerate(dat):
        x["t4"] = jnp.concatenate(tinv[4 * i:4 * i + 4], axis=0)
        x["mv"] = _sum_head_blocks(_mm(x["m_ak"], x["v"]), hmask, L)
        x["pv"] = _sum_head_blocks(_mm(x["p_rk"], x["v"]), hmask, L)
        x["vk"] = _mm_tn(x["v"], x["k_fin"])

    state = {(d, qd): st_ref[d, qd] for d in range(2) for qd in range(nq)}
    for s in range(nsub):
        cur = [(d, qd, dat[(s * 2 + d) * nq + qd]) for d in range(2) for qd in range(nq)]
        for d, qd, x in cur:
            x["ars"] = _mm_nt(x["ar_abs"], state[d, qd])
        for d, qd, x in cur:
            x["e"] = _sum_head_blocks(_mm(x["t4"], -x["ars"][0:L] - x["mv"]), hmask, L)
        for d, qd, x in cur:
            x["o"] = x["ars"][L:2 * L] + x["pv"] + _sum_head_blocks(_mm(x["p_rb"], x["e"]), hmask, L)
            state[d, qd] = state[d, qd] * x["w_tot"] + jnp.where(bd, _mm_tn(x["e"], x["b_fin"]) + x["vk"], 0.0)
        for d, o_ref in ((0, of_ref), (1, ob_ref)):
            o_ref[0, rows(d, s), :] = jnp.concatenate([x["o"] for dd, _, x in cur if dd == d], axis=-1)
    for (d, qd), val in state.items():
        st_ref[d, qd] = val

    @pl.when(c == nc - 1)
    def _():
        sf_ref[0] = st_ref[...]


def _dir_specs(block, nc, lead=None):
    specs = []
    for d in range(2):
        idx = (lambda b_, c: c) if d == 0 else (lambda b_, c: nc - 1 - c)
        if lead is None:
            specs.append(pl.BlockSpec((1,) + block, functools.partial(
                lambda f, b_, c: (b_, f(b_, c)) + (0,) * (len(block) - 1), idx)))
        else:
            specs.append(pl.BlockSpec((1, 1) + block, functools.partial(
                lambda f, dd, b_, c: (dd, b_, f(b_, c)) + (0,) * (len(block) - 1), idx, d)))
    return specs


def _rwscan(r, k, v, kk, b, lw, s0):
    bsz, t, cdim = r.shape
    blk = SCAN_SUB * CHUNK
    nc = t // blk
    nq = cdim // QUAD
    tf, tb = _dir_specs((blk, cdim), nc)
    lf, lb = _dir_specs((blk, cdim), nc, lead=True)
    sspec = pl.BlockSpec((1, 2, nq, QUAD, QUAD), lambda b_, c: (b_, 0, 0, 0, 0))
    return pl.pallas_call(
        functools.partial(_rwscan_kernel, nc, SCAN_SUB),
        out_shape=[jax.ShapeDtypeStruct((bsz, t, cdim), F32)] * 2
                  + [jax.ShapeDtypeStruct((bsz, 2, nq, QUAD, QUAD), F32)],
        grid=(bsz, nc),
        in_specs=[tf] * 5 + [lf] + [tb] * 5 + [lb] + [sspec],
        out_specs=[tf, tb, sspec],
        scratch_shapes=[pltpu.VMEM((2, nq, QUAD, QUAD), F32)],
        compiler_params=pltpu.CompilerParams(dimension_semantics=("parallel", "arbitrary"),
                                             vmem_limit_bytes=VMEM_LIMIT),
        name="rwkv_scan",
    )(r, k, v, kk, b, lw, r, k, v, kk, b, lw, s0)


def _attn_kernel(n_kv, q_ref, k_ref, vt_ref, o_ref):
    n_heads = q_ref.shape[-1] // HEAD64
    group = n_heads // n_kv
    lane = lax.broadcasted_iota(jnp.int32, (1, LANES), 1)
    kmat = k_ref[0]
    ones = vt_ref[0, n_kv * HEAD64:, :]
    vts = [jnp.concatenate([vt_ref[0, kv * HEAD64:(kv + 1) * HEAD64, :], ones], axis=0) for kv in range(n_kv)]
    outs = []
    for h in range(n_heads):
        kv = h // group
        pair = q_ref[0, :, (h // 2) * LANES:(h // 2 + 1) * LANES] * (HEAD64 ** -0.5 * LOG2E)
        if (h % 2) != kv:
            pair = pltpu.roll(pair, HEAD64, axis=1)
        qh = jnp.where((lane >= kv * HEAD64) & (lane < (kv + 1) * HEAD64), pair, 0.0).astype(BF16)
        st = lax.dot_general(kmat, qh, (((1,), (1,)), ((), ())), preferred_element_type=F32)
        m = jnp.max(st, axis=0, keepdims=True)
        pt = jnp.exp2(st - m).astype(BF16)
        res = jnp.dot(vts[kv], pt, preferred_element_type=F32)
        outs.append(res[0:HEAD64] / res[HEAD64:HEAD64 + 1])
    o_ref[0] = jnp.concatenate(outs, axis=0).T


def _attention(q, k_all, vt_all, tq):
    bsz, t, c = q.shape
    tk = k_all.shape[1]
    n_kv = k_all.shape[-1] // HEAD64
    assert n_kv * HEAD64 == LANES
    return pl.pallas_call(
        functools.partial(_attn_kernel, n_kv),
        out_shape=jax.ShapeDtypeStruct((bsz, t, c), F32),
        grid=(bsz, t // tq),
        in_specs=[pl.BlockSpec((1, tq, c), lambda b, i: (b, i, 0)),
                  pl.BlockSpec((1, tk, LANES), lambda b, i: (b, 0, 0), pipeline_mode=pl.Buffered(1)),
                  pl.BlockSpec((1, vt_all.shape[1], tk), lambda b, i: (b, 0, 0), pipeline_mode=pl.Buffered(1))],
        out_specs=pl.BlockSpec((1, tq, c), lambda b, i: (b, i, 0)),
        compiler_params=pltpu.CompilerParams(dimension_semantics=("parallel", "parallel"),
                                             vmem_limit_bytes=VMEM_LIMIT),
        name="attention",
    )(q, k_all, vt_all)


def _postnorm_residual(x, out, g_post, gate):
    ms = jnp.mean(out * out, axis=-1, keepdims=True)
    return x + gate * (out * lax.rsqrt(ms + EPS) * g_post)


def _out0_kernel(x_ref, gate_ref, of_ref, ob_ref, r_ref, k_ref, v_ref, rg_ref, oat_ref, ag_ref,
                 p_ref, bd_ref, wo_ref, y_ref):
    o = of_ref[0] + ob_ref[0]
    c = o.shape[-1]
    inv = 1.0 / HEAD64
    mu = _seg_sum(o, bd_ref) * inv
    dlt = o - mu
    var = _seg_sum(dlt * dlt, bd_ref) * inv
    gn = dlt * lax.rsqrt(var + GN_EPS) * _prow(p_ref, P0_GNG, c) + _prow(p_ref, P0_GNB, c)
    bonus = _seg_sum(r_ref[0] * k_ref[0] * _prow(p_ref, P0_RK, c), bd_ref) * v_ref[0]
    o_rw = (gn + bonus) * _silu(rg_ref[0])
    o_at = oat_ref[0] * _silu(ag_ref[0])
    out = _mm(o_rw, wo_ref[0:c, :]) + _mm(o_at, wo_ref[c:2 * c, :])
    y_ref[0] = _postnorm_residual(x_ref[0], out, _prow(p_ref, P0_G_POST, x_ref.shape[-1]), gate_ref[0])


def _out0(x, gate, o_f, o_b, r, k, v, rg, o_at, ag, w, tm):
    bsz, t, d = x.shape
    c = 512
    tokd = pl.BlockSpec((1, tm, d), lambda b, i: (b, i, 0))
    tok = pl.BlockSpec((1, tm, c), lambda b, i: (b, i, 0))
    names = ("p", "bd", "wout")
    return pl.pallas_call(
        _out0_kernel,
        out_shape=jax.ShapeDtypeStruct((bsz, t, d), F32),
        grid=(bsz, t // tm),
        in_specs=[tokd, _mod_spec(gate, d)] + [tok] * 8
                 + [_const_spec(w[n].shape) for n in names],
        out_specs=tokd,
        compiler_params=pltpu.CompilerParams(dimension_semantics=("parallel", "parallel"),
                                             vmem_limit_bytes=VMEM_LIMIT),
        name="out_proj0",
    )(x, gate, o_f, o_b, r, k, v, rg, o_at, ag, *[w[n] for n in names])


(P1_G_PRE, P1_G_POST, P1_GB, P1_ALOG_F, P1_ALOG_B, P1_DTB_F, P1_DTB_B, P1_GLA_G, P1_GDN_G, P1_CONV) = range(10)
W1_COLS = (0, 256, 512, 1024, 1152, 1664, 3200, 3328, 3456, 3968)


def _in1_kernel(x_ref, sc_ref, sh_ref, p_ref, w_ref, g2_ref,
                q_ref, k_ref, v_ref, lw_ref, gg_ref, dqkv_ref, small_ref, dg_ref):
    dm = x_ref.shape[-1]
    h = _prenorm(x_ref[0], _prow(p_ref, P1_G_PRE, dm), sc_ref[0], sh_ref[0]).astype(BF16)
    c = W1_COLS
    proj = lambda j: jnp.dot(h, w_ref[:, c[j]:c[j + 1]], preferred_element_type=F32)
    q_ref[0] = proj(0) * (HEAD64 ** -0.5)
    k_ref[0] = proj(1)
    v_ref[0] = proj(2)
    gg_ref[0] = proj(4)
    dqkv_ref[0] = proj(5)
    small_ref[0, 0] = proj(6)
    small_ref[1, 0] = proj(7)
    dg_ref[0] = proj(8)
    lw = _log_sigmoid(_mm(proj(3), g2_ref[...]) + _prow(p_ref, P1_GB, g2_ref.shape[-1])) * (1.0 / GLA_GATE_NORM)
    half = lw.shape[-1] // 2
    lw_ref[0, 0] = lw[:, :half]
    lw_ref[1, 0] = lw[:, half:]


def _in1(x, scale1p, shift, w, tm):
    bsz, t, d = x.shape
    widths = [256, 256, 512, -256, 512, 1536, -LANES, 512]
    tok = lambda n: pl.BlockSpec((1, tm, n), lambda b, i: (b, i, 0))
    outs, ospecs = [], []
    for n in widths:
        if n < 0:
            outs.append(jax.ShapeDtypeStruct((2, bsz, t, -n), F32))
            ospecs.append(pl.BlockSpec((2, 1, tm, -n), lambda b, i: (0, b, i, 0)))
        else:
            outs.append(jax.ShapeDtypeStruct((bsz, t, n), F32))
            ospecs.append(tok(n))
    return pl.pallas_call(
        _in1_kernel,
        out_shape=outs,
        grid=(bsz, t // tm),
        in_specs=[tok(d), _mod_spec(scale1p, d), _mod_spec(shift, d), _const_spec(w["p"].shape),
                  _const_spec(w["win"].shape, single=True), _const_spec(w["g2"].shape)],
        out_specs=ospecs,
        compiler_params=pltpu.CompilerParams(dimension_semantics=("parallel", "parallel"),
                                             vmem_limit_bytes=VMEM_LIMIT),
        name="in_proj1",
    )(x, scale1p, shift, w["p"], w["win"], w["g2"])


def _gdnprep_kernel(n_blocks, x_ref, xp_ref, xn_ref, s_ref, p_ref, q_ref, k_ref, v_ref, g_ref):
    i = pl.program_id(1)
    x = x_ref[0]
    prev, nxt = _neighbours(x, xp_ref[0], xn_ref[0], i, n_blocks)
    cw = lambda j: _prow(p_ref, P1_CONV + j, x.shape[-1])
    y = _silu(prev * cw(0) + x * cw(1) + nxt * cw(2))
    c = 512
    qs, ks = [], []
    for h in range(c // HEAD128):
        qh = y[:, h * HEAD128:(h + 1) * HEAD128]
        kh = y[:, c + h * HEAD128:c + (h + 1) * HEAD128]
        qs.append(qh * lax.rsqrt(jnp.sum(qh * qh, axis=-1, keepdims=True) + EPS) * (HEAD128 ** -0.5))
        ks.append(kh * lax.rsqrt(jnp.sum(kh * kh, axis=-1, keepdims=True) + EPS))
    q_ref[0] = jnp.concatenate(qs, axis=-1)
    k_ref[0] = jnp.concatenate(ks, axis=-1)
    v_ref[0] = y[:, 2 * c:]
    lane = lax.broadcasted_iota(jnp.int32, (1, LANES), 1)
    nh = c // HEAD128
    for d, (ia, ib) in enumerate(((P1_ALOG_F, P1_DTB_F), (P1_ALOG_B, P1_DTB_B))):
        s = s_ref[d, 0]
        loga = -jnp.exp(_prow(p_ref, ia, LANES)) * _softplus(s + _prow(p_ref, ib, LANES))
        g_ref[d, 0] = jnp.where(lane < nh, loga, jnp.where(lane < 2 * nh, jax.nn.sigmoid(s), 0.0))


def _gdnprep(dqkv, small, w, tm):
    bsz, t, cs = dqkv.shape
    c = 512
    nb = t // tm
    tok = pl.BlockSpec((1, tm, c), lambda b, i: (b, i, 0))
    tokl = pl.BlockSpec((2, 1, tm, LANES), lambda b, i: (0, b, i, 0))
    return pl.pallas_call(
        functools.partial(_gdnprep_kernel, nb),
        out_shape=[jax.ShapeDtypeStruct((bsz, t, c), F32)] * 3 + [jax.ShapeDtypeStruct((2, bsz, t, LANES), F32)],
        grid=(bsz, nb),
        in_specs=_halo_specs(tm, cs, t) + [tokl, _const_spec(w["p"].shape)],
        out_specs=[tok] * 3 + [tokl],
        compiler_params=pltpu.CompilerParams(dimension_semantics=("parallel", "parallel"),
                                             vmem_limit_bytes=VMEM_LIMIT),
        name="gdn_prep",
    )(dqkv, dqkv, dqkv, small, w["p"])


def _glascan_kernel(nc, qf, kf, vf, lwf, qb, kb, vb, lwb, s0_ref, of_ref, ob_ref, sf_ref, st_ref):
    c = pl.program_id(1)
    L = CHUNK
    nh = qf.shape[-1] // HEAD64

    @pl.when(c == 0)
    def _():
        st_ref[...] = s0_ref[0]

    hmask = _lane_group_masks(nh * HEAD64, HEAD64, nh)
    rowv = lax.broadcasted_iota(jnp.int32, st_ref.shape[1:], 0)
    colk = lax.broadcasted_iota(jnp.int32, st_ref.shape[1:], 1)
    bd = (rowv >> 7) == (colk >> 6)
    refs = ((qf, kf, vf, lwf, of_ref), (qb, kb, vb, lwb, ob_ref))
    dat = []
    for d in range(2):
        q_, k_, v_, lw_, _ = refs[d]
        _, incl = _order_masks(L, d == 1)
        _, incl4 = _order_masks(L, d == 1, reps=nh)
        dat.append(dict(q=q_[0], k=k_[0], v=v_[0], lw=lw_[0, 0], s=st_ref[d], incl4=incl4,
                        incl_bf=jnp.where(incl, 1.0, 0.0).astype(BF16)))
    cws = [_mm_exact_l(x["incl_bf"], x["lw"]) for x in dat]
    for x, cw in zip(dat, cws):
        c0 = cw[L // 2:L // 2 + 1, :]
        tot = jnp.sum(x["lw"], axis=0, keepdims=True)
        q_rel = x["q"] * jnp.exp(cw - c0)
        k_rel = x["k"] * jnp.exp(c0 - cw)
        x.update(sq=_stack_heads(q_rel, hmask), k_rel=k_rel, q_abs=q_rel * jnp.exp(c0),
                 k_fin=k_rel * jnp.exp(tot - c0), w_tot=jnp.exp(tot))
    for x in dat:
        x["att"] = jnp.where(x["incl4"], _mm_nt(x["sq"], x["k_rel"]), 0.0)
        x["inter"] = _mm_nt(x["q_abs"], x["s"])
    for d, x in enumerate(dat):
        intra = [_mm(x["att"][h * L:(h + 1) * L], x["v"][:, h * HEAD128:(h + 1) * HEAD128]) for h in range(nh)]
        refs[d][4][0] = jnp.concatenate(intra, axis=-1) + x["inter"]
        st_ref[d] = x["s"] * x["w_tot"] + jnp.where(bd, _mm_tn(x["v"], x["k_fin"]), 0.0)

    @pl.when(c == nc - 1)
    def _():
        sf_ref[0] = st_ref[...]


def _glascan(q, k, v, lw, s0):
    bsz, t, ck = q.shape
    cv = v.shape[-1]
    nc = t // CHUNK
    kf, kb = _dir_specs((CHUNK, ck), nc)
    vf, vb = _dir_specs((CHUNK, cv), nc)
    lf, lb = _dir_specs((CHUNK, ck), nc, lead=True)
    sspec = pl.BlockSpec((1, 2, cv, ck), lambda b_, c: (b_, 0, 0, 0))
    return pl.pallas_call(
        functools.partial(_glascan_kernel, nc),
        out_shape=[jax.ShapeDtypeStruct((bsz, t, cv), F32)] * 2 + [jax.ShapeDtypeStruct((bsz, 2, cv, ck), F32)],
        grid=(bsz, nc),
        in_specs=[kf, kf, vf, lf, kb, kb, vb, lb, sspec],
        out_specs=[vf, vb, sspec],
        scratch_shapes=[pltpu.VMEM((2, cv, ck), F32)],
        compiler_params=pltpu.CompilerParams(dimension_semantics=("parallel", "arbitrary"),
                                             vmem_limit_bytes=VMEM_LIMIT),
        name="gla_scan",
    )(q, k, v, lw, q, k, v, lw, s0)


def _gdnscan_kernel(nc, nsub, qf, kf, vf, gcf, qb, kb, vb, gcb, s0_ref, of_ref, ob_ref, sf_ref, st_ref):
    c = pl.program_id(1)
    L = CHUNK
    nh = qf.shape[-1] // HEAD128

    @pl.when(c == 0)
    def _():
        st_ref[...] = s0_ref[0]

    refs = ((qf, kf, vf, gcf), (qb, kb, vb, gcb))
    rows = lambda d, s: slice(s * L, (s + 1) * L) if d == 0 else slice((nsub - 1 - s) * L, (nsub - s) * L)
    per_dir = {}
    for d in range(2):
        strict, incl = _order_masks(L, d == 1)
        _, incl_t = _order_masks(L, d == 0)
        incl_bf = jnp.where(incl, 1.0, 0.0).astype(BF16)
        incl_t_bf = jnp.where(incl_t, 1.0, 0.0).astype(BF16)
        for s in range(nsub):
            gcol = refs[d][3][0, 0, rows(d, s), :]
            per_dir[d, s] = dict(strict=strict, incl=incl, gcol=gcol,
                                 cum_c=_mm_exact_l(incl_bf, gcol),
                                 cum_r=_mm_exact_tn(gcol, incl_t_bf),
                                 tot=jnp.sum(gcol, axis=0, keepdims=True))

    units = [(s, d, h) for s in range(nsub) for d in range(2) for h in range(nh)]
    dat = []
    for s, d, h in units:
        hs = slice(h * HEAD128, (h + 1) * HEAD128)
        pd = per_dir[d, s]
        rw = rows(d, s)
        q, k, v = refs[d][0][0, rw, hs], refs[d][1][0, rw, hs], refs[d][2][0, rw, hs]
        g = pd["cum_c"][:, h:h + 1]
        beta = pd["gcol"][:, nh + h:nh + h + 1]
        g_last = pd["tot"][:, h:h + 1]
        gam = jnp.exp(g)
        dat.append(dict(q=q, k=k, qg=q * gam, beta=beta, pd=pd,
                        dec=jnp.exp(jnp.minimum(g - pd["cum_r"][h:h + 1, :], 0.0)),
                        rhs=jnp.concatenate([(beta * gam) * k, beta * v], axis=-1),
                        k_dec=k * jnp.exp(g_last - g), gl=jnp.exp(g_last)))
    for x in dat:
        x["kk"] = _mm_nt(x["k"], x["k"])
        x["aqk"] = jnp.where(x["pd"]["incl"], x["dec"] * _mm_nt(x["q"], x["k"]), 0.0)
    tinv = _tri_inv_many([jnp.where(x["pd"]["strict"], x["dec"] * x["kk"] * x["beta"], 0.0) for x in dat], _mm)
    for x, t in zip(dat, tinv):
        x["wu"] = _mm(t, x["rhs"])
        x["wq"] = jnp.concatenate([x["wu"][:, :HEAD128], x["qg"]], axis=0)

    state = {(d, h): st_ref[d, h] for d in range(2) for h in range(nh)}
    for s in range(nsub):
        cur = [(d, h, dat[(s * 2 + d) * nh + h]) for d in range(2) for h in range(nh)]
        for d, h, x in cur:
            x["ws"] = _mm(x["wq"], state[d, h])
        for d, h, x in cur:
            x["u"] = x["wu"][:, HEAD128:] - x["ws"][0:L]
        for d, h, x in cur:
            x["o"] = x["ws"][L:2 * L] + _mm(x["aqk"], x["u"])
            state[d, h] = state[d, h] * x["gl"] + _mm_tn(x["k_dec"], x["u"])
        for d, o_ref in ((0, of_ref), (1, ob_ref)):
            o_ref[0, rows(d, s), :] = jnp.concatenate([x["o"] for dd, _, x in cur if dd == d], axis=-1)
    for (d, h), val in state.items():
        st_ref[d, h] = val

    @pl.when(c == nc - 1)
    def _():
        sf_ref[0] = st_ref[...]


def _gdnscan(q, k, v, gcol, s0):
    bsz, t, cdim = q.shape
    nh = cdim // HEAD128
    blk = SCAN_SUB * CHUNK
    nc = t // blk
    tf, tb = _dir_specs((blk, cdim), nc)
    gcf, gcb = _dir_specs((blk, LANES), nc, lead=True)
    sspec = pl.BlockSpec((1, 2, nh, HEAD128, HEAD128), lambda b_, c: (b_, 0, 0, 0, 0))
    return pl.pallas_call(
        functools.partial(_gdnscan_kernel, nc, SCAN_SUB),
        out_shape=[jax.ShapeDtypeStruct((bsz, t, cdim), F32)] * 2
                  + [jax.ShapeDtypeStruct((bsz, 2, nh, HEAD128, HEAD128), F32)],
        grid=(bsz, nc),
        in_specs=[tf, tf, tf, gcf, tb, tb, tb, gcb, sspec],
        out_specs=[tf, tb, sspec],
        scratch_shapes=[pltpu.VMEM((2, nh, HEAD128, HEAD128), F32)],
        compiler_params=pltpu.CompilerParams(dimension_semantics=("parallel", "arbitrary"),
                                             vmem_limit_bytes=VMEM_LIMIT),
        name="gdn_scan",
    )(q, k, v, gcol, q, k, v, gcol, s0)


def _head_rmsnorm(o, g):
    parts = []
    for h in range(o.shape[-1] // HEAD128):
        oh = o[:, h * HEAD128:(h + 1) * HEAD128]
        parts.append(oh * lax.rsqrt(jnp.mean(oh * oh, axis=-1, keepdims=True) + EPS) * g)
    return jnp.concatenate(parts, axis=-1)


def _out1_kernel(x_ref, gate_ref, glf_ref, glb_ref, gdf_ref, gdb_ref, gg_ref, dg_ref, p_ref, wo_ref, y_ref):
    c = gg_ref.shape[-1]
    o_gla = _head_rmsnorm(glf_ref[0] + glb_ref[0], _prow(p_ref, P1_GLA_G, HEAD128)) * _silu(gg_ref[0])
    o_gdn = _head_rmsnorm(gdf_ref[0] + gdb_ref[0], _prow(p_ref, P1_GDN_G, HEAD128)) * _silu(dg_ref[0])
    out = _mm(o_gla, wo_ref[0:c, :]) + _mm(o_gdn, wo_ref[c:2 * c, :])
    y_ref[0] = _postnorm_residual(x_ref[0], out, _prow(p_ref, P1_G_POST, x_ref.shape[-1]), gate_ref[0])


def _out1(x, gate, gla_f, gla_b, gdn_f, gdn_b, gg, dg, w, tm):
    bsz, t, d = x.shape
    c = 512
    tokd = pl.BlockSpec((1, tm, d), lambda b, i: (b, i, 0))
    tok = pl.BlockSpec((1, tm, c), lambda b, i: (b, i, 0))
    names = ("p", "wout")
    return pl.pallas_call(
        _out1_kernel,
        out_shape=jax.ShapeDtypeStruct((bsz, t, d), F32),
        grid=(bsz, t // tm),
        in_specs=[tokd, _mod_spec(gate, d)] + [tok] * 6
                 + [_const_spec(w[n].shape) for n in names],
        out_specs=tokd,
        compiler_params=pltpu.CompilerParams(dimension_semantics=("parallel", "parallel"),
                                             vmem_limit_bytes=VMEM_LIMIT),
        name="out_proj1",
    )(x, gate, gla_f, gla_b, gdn_f, gdn_b, gg, dg, *[w[n] for n in names])


def _block_ones(n_groups, width):
    return jnp.asarray(np.kron(np.eye(n_groups, dtype=np.float32), np.ones((width, width), np.float32)), BF16)


def _rope_tables(t):
    rows = t // GRID_W
    row_id = np.repeat(np.arange(rows, dtype=np.float32), GRID_W)
    col_id = np.tile(np.arange(GRID_W, dtype=np.float32), rows)
    nf = HEAD64 // 4
    inv = np.float32(ROPE_THETA) ** (-np.arange(nf, dtype=np.float32) / np.float32(nf))
    ang_r = (row_id[:, None] * inv[None, :]).astype(np.float32)
    ang_c = (col_id[:, None] * inv[None, :]).astype(np.float32)
    cos = np.concatenate([np.cos(ang_r)] * 2 + [np.cos(ang_c)] * 2, axis=-1)
    sin = np.concatenate([-np.sin(ang_r), np.sin(ang_r), -np.sin(ang_c), np.sin(ang_c)], axis=-1)
    tile2 = lambda a: jnp.asarray(np.concatenate([a, a], axis=-1).astype(np.float32))
    return tile2(cos), tile2(sin)


def _pack_rows(rows, width):
    return jnp.stack([jnp.pad(r.reshape(-1).astype(F32), (0, width - r.size)) for r in rows])


def _layer0_weights(p):
    w_in = p["w_in"]
    rw_w = 512
    slab_w = 3 * rw_w + 192
    d = w_in.shape[0]
    w = {}
    w["win"] = jnp.concatenate([w_in[:, :slab_w], jnp.zeros((d, W0_COLS[1] - slab_w), F32), w_in[:, slab_w:]],
                               axis=1).astype(BF16)
    w["wout"] = p["w_out"].astype(BF16)
    w["bdq"] = _block_ones(8, HEAD64)
    w["bdk"] = _block_ones(2, HEAD64)
    w["bd"] = w["bdq"]
    w["p"] = _pack_rows([p["g_pre"], p["g_post"], p["rw_mu"], jnp.concatenate([p["rw_w0_f"], p["rw_w0_b"]]),
                         p["rw_a0"], p["rw_k_k"], p["rw_k_a"], p["rw_r_k"], p["rw_gn_g"], p["rw_gn_b"],
                         jnp.tile(p["at_gq"], 8), jnp.tile(p["at_gk"], 2)], W0_COLS[1])
    z = jnp.zeros((64, rw_w), F32)
    w["w2"] = jnp.concatenate([jnp.concatenate([p["rw_w2_f"], z], axis=1),
                               jnp.concatenate([z, p["rw_w2_b"]], axis=1)], axis=0).astype(BF16)
    w["a2"] = jnp.concatenate([p["rw_a2"], z], axis=0).astype(BF16)
    return w


def _layer1_weights(p):
    w_in = p["w_in"]
    d = w_in.shape[0]
    zc = lambda n: jnp.zeros((d, n), F32)
    af, ab, be = w_in[:, 3104:3108], w_in[:, 3108:3112], w_in[:, 3112:3116]
    w = {}
    w["win"] = jnp.concatenate([w_in[:, :1024], w_in[:, 1024:1056], zc(LANES - 32), w_in[:, 1056:3104],
                                af, be, zc(LANES - 8), ab, be, zc(LANES - 8), w_in[:, 3116:]],
                               axis=1).astype(BF16)
    w["wout"] = p["w_out"].astype(BF16)
    z = jnp.zeros((16, 256), F32)
    g2 = jnp.concatenate([jnp.concatenate([p["gla_g2_f"], z], axis=1),
                          jnp.concatenate([z, p["gla_g2_b"]], axis=1)], axis=0)
    w["g2"] = jnp.pad(g2, ((0, LANES - 32), (0, 0))).astype(BF16)
    conv = p["gdn_conv"]
    w["p"] = _pack_rows([p["g_pre"], p["g_post"], jnp.concatenate([p["gla_gb_f"], p["gla_gb_b"]]),
                         p["gdn_A_log_f"], p["gdn_A_log_b"], p["gdn_dt_bias_f"], p["gdn_dt_bias_b"],
                         p["gla_norm_g"], p["gdn_norm_g"], conv[0], conv[1], conv[2]], conv.shape[1])
    return w


def _rw_state_to_bd(s):
    bsz = s.shape[0]
    s = s.reshape(bsz, 2, 4, HEAD64, HEAD64)
    eye = jnp.eye(4, dtype=s.dtype)
    return jnp.einsum("bqhvk,hg->bqhvgk", s, eye).reshape(bsz, 2, QUAD, QUAD)


def _rw_state_from_bd(s):
    bsz = s.shape[0]
    s = s.reshape(bsz, 2, 4, HEAD64, 4, HEAD64)
    return jnp.stack([s[:, :, h, :, h, :] for h in range(4)], axis=2).reshape(bsz, 8, HEAD64, HEAD64)


def _gla_state_to_bd(s):
    bsz = s.shape[0]
    eye = jnp.eye(4, dtype=s.dtype)
    return jnp.einsum("bhkv,hg->bhvgk", s, eye).reshape(bsz, 4 * HEAD128, 4 * HEAD64)


def _gla_state_from_bd(s):
    bsz = s.shape[0]
    s = s.reshape(bsz, 4, HEAD128, 4, HEAD64)
    return jnp.stack([jnp.swapaxes(s[:, h, :, h, :], -1, -2) for h in range(4)], axis=1)


def _trunk(x, mods, w0, w1, ctx, tm, tq):
    bsz, t, dm = x.shape
    latent = ctx is not None
    ts = min(tm, t)
    shared = mods[0][0].shape[0] == 1
    flat = (lambda a: a.reshape(a.shape[:-3] + (1, bsz * t, a.shape[-1]))) if shared else (lambda a: a)
    unflat = lambda a: a.reshape(a.shape[:-3] + (bsz, t, a.shape[-1]))

    sc, sh, gt = mods[0]
    slab, rg, q, k, v, ag = [unflat(a) for a in _in0(flat(x), sc, sh, w0, latent, tm)]
    r, kmod, vr, kkn, bvec, lw = _rwprep(slab, w0, ts)
    if latent:
        s0f, s0b, k_ctx, v_ctx = ctx[:4]
        s0 = jnp.stack([_rw_state_to_bd(s0f), _rw_state_to_bd(s0b)], axis=1)
        k_all = jnp.concatenate([k_ctx.reshape(bsz, -1, 2 * HEAD64), k], axis=1)
        v_all = jnp.concatenate([v_ctx.reshape(bsz, -1, 2 * HEAD64), v], axis=1)
    else:
        s0 = jnp.zeros((bsz, 2, 2, QUAD, QUAD), F32)
        k_all, v_all = k, v
    o_rwf, o_rwb, s_rw = _rwscan(r, kmod, vr, kkn, bvec, lw, s0)
    vt_all = jnp.concatenate([jnp.swapaxes(v_all, 1, 2), jnp.ones((bsz, 16, v_all.shape[1]), F32)], axis=1)
    o_at = _attention(q, k_all.astype(BF16), vt_all.astype(BF16), tq)
    x1 = _out0(flat(x), gt, *[flat(a) for a in (o_rwf, o_rwb, r, kmod, vr, rg, o_at, ag)], w0, tm)

    sc, sh, gt = mods[1]
    gq, gk, gv, glw, gg, dqkv, small, dg = [unflat(a) for a in _in1(x1, sc, sh, w1, tm)]
    dq, dk, dv, gcol = _gdnprep(dqkv, small, w1, ts)
    if latent:
        sgf, sgb, sdf, sdb = ctx[4:]
        s0_gla = jnp.stack([_gla_state_to_bd(sgf), _gla_state_to_bd(sgb)], axis=1)
        s0_gdn = jnp.stack([sdf, sdb], axis=1)
    else:
        s0_gla = jnp.zeros((bsz, 2, 4 * HEAD128, 4 * HEAD64), F32)
        s0_gdn = jnp.zeros((bsz, 2, 4, HEAD128, HEAD128), F32)
    gla_f, gla_b, s_gla = _glascan(gq, gk, gv, glw, s0_gla)
    gdn_f, gdn_b, s_gdn = _gdnscan(dq, dk, dv, gcol, s0_gdn)
    y = unflat(_out1(x1, gt, *[flat(a) for a in (gla_f, gla_b, gdn_f, gdn_b, gg, dg)], w1, tm))

    new = None
    if not latent:
        new = (_rw_state_from_bd(s_rw[:, 0]), _rw_state_from_bd(s_rw[:, 1]),
               k.reshape(bsz, t, 2, HEAD64), v.reshape(bsz, t, 2, HEAD64),
               _gla_state_from_bd(s_gla[:, 0]), _gla_state_from_bd(s_gla[:, 1]), s_gdn[:, 0], s_gdn[:, 1])
    return y, new


def _split_mod(m, d):
    shift, scale, gate = m[..., :d], m[..., d:2 * d], m[..., 2 * d:]
    return (1.0 + scale)[:, None, :], shift[:, None, :], gate[:, None, :]


def kernel(x_prompt, x_sample, state_l0_rwkv_fwd, state_l0_rwkv_bwd, cache_l0_k, cache_l0_v, state_l1_gla_fwd, state_l1_gla_bwd, state_l1_gdn_fwd, state_l1_gdn_bwd, c, c_ctx, l0_mod_w, l0_mod_b, l0_g_pre, l0_g_post, l0_w_in, l0_w_out, l0_rw_mu, l0_rw_w0_f, l0_rw_w2_f, l0_rw_w0_b, l0_rw_w2_b, l0_rw_a0, l0_rw_a2, l0_rw_k_k, l0_rw_k_a, l0_rw_r_k, l0_rw_gn_g, l0_rw_gn_b, l0_at_gq, l0_at_gk, l1_mod_w, l1_mod_b, l1_g_pre, l1_g_post, l1_w_in, l1_w_out, l1_gla_g2_f, l1_gla_gb_f, l1_gla_g2_b, l1_gla_gb_b, l1_gla_norm_g, l1_gdn_conv, l1_gdn_A_log_f, l1_gdn_dt_bias_f, l1_gdn_A_log_b, l1_gdn_dt_bias_b, l1_gdn_norm_g):
    p0 = {"g_pre": l0_g_pre, "g_post": l0_g_post, "w_in": l0_w_in, "w_out": l0_w_out, "rw_mu": l0_rw_mu,
          "rw_w0_f": l0_rw_w0_f, "rw_w2_f": l0_rw_w2_f, "rw_w0_b": l0_rw_w0_b, "rw_w2_b": l0_rw_w2_b,
          "rw_a0": l0_rw_a0, "rw_a2": l0_rw_a2, "rw_k_k": l0_rw_k_k, "rw_k_a": l0_rw_k_a,
          "rw_r_k": l0_rw_r_k, "rw_gn_g": l0_rw_gn_g, "rw_gn_b": l0_rw_gn_b,
          "at_gq": l0_at_gq, "at_gk": l0_at_gk}
    p1 = {"g_pre": l1_g_pre, "g_post": l1_g_post, "w_in": l1_w_in, "w_out": l1_w_out,
          "gla_g2_f": l1_gla_g2_f, "gla_gb_f": l1_gla_gb_f, "gla_g2_b": l1_gla_g2_b,
          "gla_gb_b": l1_gla_gb_b, "gla_norm_g": l1_gla_norm_g, "gdn_conv": l1_gdn_conv,
          "gdn_A_log_f": l1_gdn_A_log_f, "gdn_dt_bias_f": l1_gdn_dt_bias_f,
          "gdn_A_log_b": l1_gdn_A_log_b, "gdn_dt_bias_b": l1_gdn_dt_bias_b,
          "gdn_norm_g": l1_gdn_norm_g}
    d = x_prompt.shape[-1]
    nb = c.shape[0]
    w0 = _layer0_weights(p0)
    w1 = _layer1_weights(p1)
    cos, sin = _rope_tables(x_sample.shape[1])
    w0["cos"], w0["sin"] = cos, sin

    cvec = jnp.concatenate([c, c_ctx[None, :], jnp.zeros((SUBLANES - nb - 1, d), F32)], axis=0)
    m0 = _modulation(cvec, l0_mod_w, l0_mod_b)
    m1 = _modulation(cvec, l1_mod_w, l1_mod_b)
    mods_lat = [_split_mod(m[:nb], d) for m in (m0, m1)]
    mods_ctx = [_split_mod(m[nb:nb + 1], d) for m in (m0, m1)]

    tm = 512
    y_prompt, new = _trunk(x_prompt, mods_ctx, w0, w1, None, tm, min(512, x_prompt.shape[1]))
    ctx = (state_l0_rwkv_fwd, state_l0_rwkv_bwd, cache_l0_k, cache_l0_v,
           state_l1_gla_fwd, state_l1_gla_bwd, state_l1_gdn_fwd, state_l1_gdn_bwd)
    y_sample, _ = _trunk(x_sample, mods_lat, w0, w1, ctx, tm, min(512, x_sample.shape[1]))
    return (y_prompt, y_sample) + tuple(new)
```

```python
import functools

import numpy as np
import jax
import jax.numpy as jnp
from jax import lax
from jax.experimental import pallas as pl
from jax.experimental.pallas import tpu as pltpu

F32 = jnp.float32
BF16 = jnp.bfloat16

EPS = 1e-6
GN_EPS = 64e-5
CHUNK = 64
GRID_W = 64
ROPE_THETA = 10000.0
RW_DECAY_SCALE = 0.6065306597126334
GLA_GATE_NORM = 16.0
LOG2E = 1.4426950408889634
HEAD64 = 64
HEAD128 = 128
LANES = 128
SUBLANES = 8
VMEM_LIMIT = 56 * 1024 * 1024


def _mm(a, b):
    return jnp.dot(a.astype(BF16), b.astype(BF16), preferred_element_type=F32)


def _mm_nt(a, b):
    return lax.dot_general(a.astype(BF16), b.astype(BF16), (((1,), (1,)), ((), ())),
                           preferred_element_type=F32)


def _mm_tn(a, b):
    return lax.dot_general(a.astype(BF16), b.astype(BF16), (((0,), (0,)), ((), ())),
                           preferred_element_type=F32)


def _split3(x):
    hi = x.astype(BF16)
    r1 = x - hi.astype(F32)
    mid = r1.astype(BF16)
    lo = (r1 - mid.astype(F32)).astype(BF16)
    return hi, mid, lo


def _mm_exact_l(mask_bf16, x):
    hi, mid, lo = _split3(x)
    d = functools.partial(jnp.dot, preferred_element_type=F32)
    return d(mask_bf16, hi) + d(mask_bf16, mid) + d(mask_bf16, lo)


def _mm_exact_tn(x, mask_bf16):
    hi, mid, lo = _split3(x)
    d = lambda a: lax.dot_general(a, mask_bf16, (((0,), (0,)), ((), ())), preferred_element_type=F32)
    return d(hi) + d(mid) + d(lo)


def _mm3(a, b):
    ah = a.astype(BF16)
    al = (a - ah.astype(F32)).astype(BF16)
    bh = b.astype(BF16)
    bl = (b - bh.astype(F32)).astype(BF16)
    d = functools.partial(jnp.dot, preferred_element_type=F32)
    return d(ah, bh) + d(ah, bl) + d(al, bh)


def _silu(x):
    return x * jax.nn.sigmoid(x)


def _softplus(x):
    return jnp.maximum(x, 0.0) + jnp.log(1.0 + jnp.exp(-jnp.abs(x)))


def _log_sigmoid(x):
    return jnp.minimum(x, 0.0) - jnp.log(1.0 + jnp.exp(-jnp.abs(x)))


def _order_masks(n, reverse, reps=1):
    row = lax.broadcasted_iota(jnp.int32, (reps * n, n), 0) & (n - 1)
    col = lax.broadcasted_iota(jnp.int32, (reps * n, n), 1)
    d = (col - row) if reverse else (row - col)
    return d > 0, d >= 0


def _tri_inv_many(nmats, mm):
    n = nmats[0].shape[0]
    row = lax.broadcasted_iota(jnp.int32, (n, n), 0)
    col = lax.broadcasted_iota(jnp.int32, (n, n), 1)
    x = row ^ col
    eye = jnp.where(row == col, 1.0, 0.0).astype(F32)
    ts = [eye - jnp.where(x == 1, nm, 0.0) for nm in nmats]
    s = 1
    while (2 << s) <= n:
        lvl = (x >> s) == 1
        tcs = [mm(t, jnp.where(lvl, nm, 0.0)) for t, nm in zip(ts, nmats)]
        ts = [t - mm(tc, t) for t, tc in zip(ts, tcs)]
        s += 1
    return ts


def _lane_group_masks(width, group, count):
    lane = lax.broadcasted_iota(jnp.int32, (1, width), 1)
    return [((lane >= g * group) & (lane < (g + 1) * group)) for g in range(count)]


def _stack_heads(x, masks):
    return jnp.concatenate([jnp.where(m, x, 0.0) for m in masks], axis=0)


def _mod_kernel(c_ref, w_ref, b_ref, o_ref):
    s = _silu(c_ref[...])
    o_ref[...] = _mm3(s, w_ref[...]) + b_ref[...]


def _modulation(cvec, mod_w, mod_b):
    rows, d = cvec.shape
    n = mod_w.shape[1]
    tn = 1024
    return pl.pallas_call(
        _mod_kernel,
        out_shape=jax.ShapeDtypeStruct((rows, n), F32),
        grid=(n // tn,),
        in_specs=[pl.BlockSpec((rows, d), lambda j: (0, 0)),
                  pl.BlockSpec((d, tn), lambda j: (0, j)),
                  pl.BlockSpec((1, tn), lambda j: (0, j))],
        out_specs=pl.BlockSpec((rows, tn), lambda j: (0, j)),
        compiler_params=pltpu.CompilerParams(vmem_limit_bytes=VMEM_LIMIT),
        name="modulation",
    )(cvec, mod_w, mod_b.reshape(1, n))


def _prenorm(x, g, scale1p, shift):
    ms = jnp.mean(x * x, axis=-1, keepdims=True)
    return x * lax.rsqrt(ms + EPS) * g * scale1p + shift


def _seg_sum(x, bd_ref):
    hi = x.astype(BF16)
    lo = (x - hi.astype(F32)).astype(BF16)
    bd = bd_ref[...]
    return jnp.dot(hi, bd, preferred_element_type=F32) + jnp.dot(lo, bd, preferred_element_type=F32)


def _rope(x, cos, sin_signed):
    w = x.shape[-1]
    lane = lax.broadcasted_iota(jnp.int32, (1, w), 1)
    first = (lane & 31) < 16
    partner = jnp.where(first, pltpu.roll(x, w - 16, axis=1), pltpu.roll(x, 16, axis=1))
    return x * cos + partner * sin_signed


def _tile_lanes(x, reps):
    return jnp.concatenate([x] * reps, axis=-1) if reps > 1 else x


P0_G_PRE, P0_G_POST, P0_MU, P0_W0, P0_A0, P0_KK, P0_KA, P0_RK, P0_GNG, P0_GNB, P0_GQ, P0_GK = range(12)
W0_COLS = (0, 1792, 2304, 2816, 2944, 3072, 3584)


def _prow(p_ref, i, n):
    return p_ref[i:i + 1, 0:n]


def _in0_kernel(use_rope, x_ref, sc_ref, sh_ref, p_ref, w_ref, bdq_ref, bdk_ref, cos_ref, sin_ref,
                slab_ref, rg_ref, q_ref, k_ref, v_ref, ag_ref):
    dm = x_ref.shape[-1]
    h = _prenorm(x_ref[0], _prow(p_ref, P0_G_PRE, dm), sc_ref[0], sh_ref[0]).astype(BF16)
    c = W0_COLS
    proj = lambda j: jnp.dot(h, w_ref[:, c[j]:c[j + 1]], preferred_element_type=F32)
    slab_ref[0] = proj(0)
    rg_ref[0] = proj(1)
    v_ref[0] = proj(4)
    ag_ref[0] = proj(5)
    q = proj(2)
    k = proj(3)
    q = q * lax.rsqrt(_seg_sum(q * q, bdq_ref) * (1.0 / HEAD64) + EPS) * _prow(p_ref, P0_GQ, q.shape[-1])
    k = k * lax.rsqrt(_seg_sum(k * k, bdk_ref) * (1.0 / HEAD64) + EPS) * _prow(p_ref, P0_GK, k.shape[-1])
    if use_rope:
        cos = cos_ref[...]
        sin = sin_ref[...]
        q = _rope(q, _tile_lanes(cos, q.shape[-1] // LANES), _tile_lanes(sin, q.shape[-1] // LANES))
        k = _rope(k, cos, sin)
    q_ref[0] = q
    k_ref[0] = k


def _const_spec(shape, single=False):
    nd = len(shape)
    if single:
        return pl.BlockSpec(shape, lambda *_: (0,) * nd, pipeline_mode=pl.Buffered(1))
    return pl.BlockSpec(shape, lambda *_: (0,) * nd)


def _mod_spec(arr, d):
    if arr.shape[0] == 1:
        return pl.BlockSpec((1, 1, d), lambda b, i: (0, 0, 0))
    return pl.BlockSpec((1, 1, d), lambda b, i: (b, 0, 0))


def _in0(x, scale1p, shift, w, use_rope, tm):
    bsz, t, d = x.shape
    widths = [W0_COLS[j + 1] - W0_COLS[j] for j in range(6)]
    outs = [jax.ShapeDtypeStruct((bsz, t, n), F32) for n in widths]
    tok = lambda n: pl.BlockSpec((1, tm, n), lambda b, i: (b, i, 0))
    in_specs = [tok(d), _mod_spec(scale1p, d), _mod_spec(shift, d), _const_spec(w["p"].shape),
                _const_spec(w["win"].shape, single=True), _const_spec(w["bdq"].shape), _const_spec(w["bdk"].shape)]
    in_specs += [pl.BlockSpec((tm, LANES), lambda b, i: (i, 0))] * 2
    return pl.pallas_call(
        functools.partial(_in0_kernel, use_rope),
        out_shape=outs,
        grid=(bsz, t // tm),
        in_specs=in_specs,
        out_specs=[tok(n) for n in widths],
        compiler_params=pltpu.CompilerParams(dimension_semantics=("parallel", "parallel"),
                                             vmem_limit_bytes=VMEM_LIMIT),
        name="in_proj0",
    )(x, scale1p, shift, w["p"], w["win"], w["bdq"], w["bdk"], w["cos"], w["sin"])


def _neighbours(x, prev_blk, next_blk, i, n_blocks):
    tm = x.shape[0]
    row = lax.broadcasted_iota(jnp.int32, (tm, 1), 0)
    prev_row = jnp.where(i == 0, 0.0, prev_blk[SUBLANES - 1:SUBLANES, :])
    next_row = jnp.where(i == n_blocks - 1, 0.0, next_blk[0:1, :])
    prev = jnp.where(row == 0, prev_row, pltpu.roll(x, 1, axis=0))
    nxt = jnp.where(row == tm - 1, next_row, pltpu.roll(x, tm - 1, axis=0))
    return prev, nxt


def _halo_specs(tm, c, t):
    r = tm // SUBLANES
    last = t // SUBLANES - 1
    main = pl.BlockSpec((1, tm, c), lambda b, i: (b, i, 0))
    prev = pl.BlockSpec((1, SUBLANES, c), lambda b, i: (b, jnp.maximum(i * r - 1, 0), 0))
    nxt = pl.BlockSpec((1, SUBLANES, c), lambda b, i: (b, jnp.minimum((i + 1) * r, last), 0))
    return [main, prev, nxt]


def _rwprep_kernel(n_blocks, s_ref, sp_ref, sn_ref, p_ref, w2_ref, a2_ref, bd_ref,
                   r_ref, k_ref, v_ref, kkn_ref, b_ref, lw_ref):
    i = pl.program_id(1)
    x = s_ref[0]
    prev, nxt = _neighbours(x, sp_ref[0], sn_ref[0], i, n_blocks)
    xs = x + _prow(p_ref, P0_MU, x.shape[-1]) * (0.5 * (prev + nxt) - x)
    c = 512
    r, kr, vr = xs[:, :c], xs[:, c:2 * c], xs[:, 2 * c:3 * c]
    lw_in = xs[:, 3 * c:3 * c + LANES]
    la_in = xs[:, 3 * c + LANES:3 * c + 2 * LANES]
    lor = _mm(jnp.tanh(lw_in), w2_ref[...]) + _prow(p_ref, P0_W0, 2 * c)
    logw = -RW_DECAY_SCALE * jax.nn.sigmoid(lor)
    a = jax.nn.sigmoid(_prow(p_ref, P0_A0, c) + _mm(la_in, a2_ref[...]))
    kkp = kr * _prow(p_ref, P0_KK, c)
    kkn = kkp * lax.rsqrt(_seg_sum(kkp * kkp, bd_ref) + EPS)
    r_ref[0] = r
    k_ref[0] = kr * (1.0 + (a - 1.0) * _prow(p_ref, P0_KA, c))
    v_ref[0] = vr
    kkn_ref[0] = kkn
    b_ref[0] = kkn * a
    lw_ref[0, 0] = logw[:, :c]
    lw_ref[1, 0] = logw[:, c:]


def _rwprep(slab, w, tm):
    bsz, t, cs = slab.shape
    c = 512
    nb = t // tm
    tok = pl.BlockSpec((1, tm, c), lambda b, i: (b, i, 0))
    outs = [jax.ShapeDtypeStruct((bsz, t, c), F32)] * 5 + [jax.ShapeDtypeStruct((2, bsz, t, c), F32)]
    names = ("p", "w2", "a2", "bd")
    return pl.pallas_call(
        functools.partial(_rwprep_kernel, nb),
        out_shape=outs,
        grid=(bsz, nb),
        in_specs=_halo_specs(tm, cs, t) + [_const_spec(w[k].shape) for k in names],
        out_specs=[tok] * 5 + [pl.BlockSpec((2, 1, tm, c), lambda b, i: (0, b, i, 0))],
        compiler_params=pltpu.CompilerParams(dimension_semantics=("parallel", "parallel"),
                                             vmem_limit_bytes=VMEM_LIMIT),
        name="rwkv_prep",
    )(slab, slab, slab, *[w[k] for k in names])


QUAD = 4 * HEAD64


def _sum_head_blocks(x, hmask, L):
    acc = jnp.where(hmask[0], x[0:L], 0.0)
    for h in range(1, len(hmask)):
        acc = acc + jnp.where(hmask[h], x[h * L:(h + 1) * L], 0.0)
    return acc


RW_SUB = 2
GDN_SUB = 4
GLA_SUB = 4


def _rwscan_kernel(nc, nsub, rf, kf, vf, kkf, bf, lwf, rb, kb, vb, kkb, bb, lwb, s0_ref,
                   of_ref, ob_ref, sf_ref, st_ref):
    c = pl.program_id(1)
    L = CHUNK
    nq = rf.shape[-1] // QUAD

    @pl.when(c == 0)
    def _():
        st_ref[...] = s0_ref[0]

    hmask = _lane_group_masks(QUAD, HEAD64, 4)
    rowq = lax.broadcasted_iota(jnp.int32, (QUAD, QUAD), 0)
    colq = lax.broadcasted_iota(jnp.int32, (QUAD, QUAD), 1)
    bd = (rowq >> 6) == (colq >> 6)
    masks = []
    for rev in (False, True):
        _, incl = _order_masks(L, rev)
        strict4, incl4 = _order_masks(L, rev, reps=4)
        masks.append((jnp.where(incl, 1.0, 0.0).astype(BF16), strict4, incl4))
    refs = ((rf, kf, vf, kkf, bf, lwf), (rb, kb, vb, kkb, bb, lwb))

    rows = lambda d, s: slice(s * L, (s + 1) * L) if d == 0 else slice((nsub - 1 - s) * L, (nsub - s) * L)
    units = [(s, d, qd) for s in range(nsub) for d in range(2) for qd in range(nq)]

    dat = []
    for s, d, qd in units:
        sl = slice(qd * QUAD, (qd + 1) * QUAD)
        rw = rows(d, s)
        r_, k_, v_, kk_, b_, lw_ = refs[d]
        dat.append(dict(r=r_[0, rw, sl], k=k_[0, rw, sl], v=v_[0, rw, sl], kk=kk_[0, rw, sl], b=b_[0, rw, sl],
                        lw=lw_[0, 0, rw, sl]))
    cws = [_mm_exact_l(masks[d][0], x["lw"]) for (_, d, _), x in zip(units, dat)]
    for x, cw in zip(dat, cws):
        c0 = cw[L // 2:L // 2 + 1, :]
        tot = jnp.sum(x["lw"], axis=0, keepdims=True)
        e_rel = jnp.exp(cw - c0)
        e_inv = jnp.exp(c0 - cw)
        ec0 = jnp.exp(c0)
        e_fin = jnp.exp(tot - c0)
        a_rel = x["kk"] * e_rel * jnp.exp(-x["lw"])
        r_rel = x["r"] * e_rel
        b_rel = x["b"] * e_inv
        k_rel = x["k"] * e_inv
        x.update(ar_abs=jnp.concatenate([a_rel * ec0, r_rel * ec0], axis=0), b_rel=b_rel, k_rel=k_rel,
                 b_fin=b_rel * e_fin, k_fin=k_rel * e_fin, w_tot=jnp.exp(tot),
                 sa=_stack_heads(a_rel, hmask), sr=_stack_heads(r_rel, hmask))
    for (_, d, _), x in zip(units, dat):
        _, strict4, incl4 = masks[d]
        x["n_ab"] = jnp.where(strict4, _mm_nt(x["sa"], x["b_rel"]), 0.0)
        x["m_ak"] = jnp.where(strict4, _mm_nt(x["sa"], x["k_rel"]), 0.0)
        x["p_rb"] = jnp.where(incl4, _mm_nt(x["sr"], x["b_rel"]), 0.0)
        x["p_rk"] = jnp.where(incl4, _mm_nt(x["sr"], x["k_rel"]), 0.0)
    tinv = _tri_inv_many([x["n_ab"][h * L:(h + 1) * L] for x in dat for h in range(4)], _mm)
    for i, x in enumerate(dat):
        x["t4"] = jnp.concatenate(tinv[4 * i:4 * i + 4], axis=0)
        x["mv"] = _sum_head_blocks(_mm(x["m_ak"], x["v"]), hmask, L)
        x["pv"] = _sum_head_blocks(_mm(x["p_rk"], x["v"]), hmask, L)
        x["vk"] = _mm_tn(x["v"], x["k_fin"])

    state = {(d, qd): st_ref[d, qd] for d in range(2) for qd in range(nq)}
    for s in range(nsub):
        cur = [(d, qd, dat[(s * 2 + d) * nq + qd]) for d in range(2) for qd in range(nq)]
        for d, qd, x in cur:
            x["ars"] = _mm_nt(x["ar_abs"], state[d, qd])
        for d, qd, x in cur:
            x["e"] = _sum_head_blocks(_mm(x["t4"], -x["ars"][0:L] - x["mv"]), hmask, L)
        for d, qd, x in cur:
            x["o"] = x["ars"][L:2 * L] + x["pv"] + _sum_head_blocks(_mm(x["p_rb"], x["e"]), hmask, L)
            state[d, qd] = state[d, qd] * x["w_tot"] + jnp.where(bd, _mm_tn(x["e"], x["b_fin"]) + x["vk"], 0.0)
        for d, o_ref in ((0, of_ref), (1, ob_ref)):
            o_ref[0, rows(d, s), :] = jnp.concatenate([x["o"] for dd, _, x in cur if dd == d], axis=-1)
    for (d, qd), val in state.items():
        st_ref[d, qd] = val

    @pl.when(c == nc - 1)
    def _():
        sf_ref[0] = st_ref[...]


def _dir_specs(block, nc, lead=None):
    specs = []
    for d in range(2):
        idx = (lambda b_, c: c) if d == 0 else (lambda b_, c: nc - 1 - c)
        if lead is None:
            specs.append(pl.BlockSpec((1,) + block, functools.partial(
                lambda f, b_, c: (b_, f(b_, c)) + (0,) * (len(block) - 1), idx)))
        else:
            specs.append(pl.BlockSpec((1, 1) + block, functools.partial(
                lambda f, dd, b_, c: (dd, b_, f(b_, c)) + (0,) * (len(block) - 1), idx, d)))
    return specs


def _rwscan(r, k, v, kk, b, lw, s0):
    bsz, t, cdim = r.shape
    blk = RW_SUB * CHUNK
    nc = t // blk
    nq = cdim // QUAD
    tf, tb = _dir_specs((blk, cdim), nc)
    lf, lb = _dir_specs((blk, cdim), nc, lead=True)
    sspec = pl.BlockSpec((1, 2, nq, QUAD, QUAD), lambda b_, c: (b_, 0, 0, 0, 0))
    return pl.pallas_call(
        functools.partial(_rwscan_kernel, nc, RW_SUB),
        out_shape=[jax.ShapeDtypeStruct((bsz, t, cdim), F32)] * 2
                  + [jax.ShapeDtypeStruct((bsz, 2, nq, QUAD, QUAD), F32)],
        grid=(bsz, nc),
        in_specs=[tf] * 5 + [lf] + [tb] * 5 + [lb] + [sspec],
        out_specs=[tf, tb, sspec],
        scratch_shapes=[pltpu.VMEM((2, nq, QUAD, QUAD), F32)],
        compiler_params=pltpu.CompilerParams(dimension_semantics=("parallel", "arbitrary"),
                                             vmem_limit_bytes=VMEM_LIMIT),
        name="rwkv_scan",
    )(r, k, v, kk, b, lw, r, k, v, kk, b, lw, s0)


def _attn_kernel(n_kv, q_ref, k_ref, vt_ref, o_ref):
    n_heads = q_ref.shape[-1] // HEAD64
    group = n_heads // n_kv
    lane = lax.broadcasted_iota(jnp.int32, (1, LANES), 1)
    kmat = k_ref[0]
    ones = vt_ref[0, n_kv * HEAD64:, :]
    vts = [jnp.concatenate([vt_ref[0, kv * HEAD64:(kv + 1) * HEAD64, :], ones], axis=0) for kv in range(n_kv)]
    outs = []
    for h in range(n_heads):
        kv = h // group
        pair = q_ref[0, :, (h // 2) * LANES:(h // 2 + 1) * LANES] * (HEAD64 ** -0.5 * LOG2E)
        if (h % 2) != kv:
            pair = pltpu.roll(pair, HEAD64, axis=1)
        qh = jnp.where((lane >= kv * HEAD64) & (lane < (kv + 1) * HEAD64), pair, 0.0).astype(BF16)
        st = lax.dot_general(kmat, qh, (((1,), (1,)), ((), ())), preferred_element_type=F32)
        m = jnp.max(st, axis=0, keepdims=True)
        pt = jnp.exp2(st - m).astype(BF16)
        res = jnp.dot(vts[kv], pt, preferred_element_type=F32)
        outs.append(res[0:HEAD64] / res[HEAD64:HEAD64 + 1])
    o_ref[0] = jnp.concatenate(outs, axis=0).T


def _attention(q, k_all, vt_all, tq):
    bsz, t, c = q.shape
    tk = k_all.shape[1]
    n_kv = k_all.shape[-1] // HEAD64
    assert n_kv * HEAD64 == LANES
    return pl.pallas_call(
        functools.partial(_attn_kernel, n_kv),
        out_shape=jax.ShapeDtypeStruct((bsz, t, c), F32),
        grid=(bsz, t // tq),
        in_specs=[pl.BlockSpec((1, tq, c), lambda b, i: (b, i, 0)),
                  pl.BlockSpec((1, tk, LANES), lambda b, i: (b, 0, 0), pipeline_mode=pl.Buffered(1)),
                  pl.BlockSpec((1, vt_all.shape[1], tk), lambda b, i: (b, 0, 0), pipeline_mode=pl.Buffered(1))],
        out_specs=pl.BlockSpec((1, tq, c), lambda b, i: (b, i, 0)),
        compiler_params=pltpu.CompilerParams(dimension_semantics=("parallel", "parallel"),
                                             vmem_limit_bytes=VMEM_LIMIT),
        name="attention",
    )(q, k_all, vt_all)


def _postnorm_residual(x, out, g_post, gate):
    ms = jnp.mean(out * out, axis=-1, keepdims=True)
    return x + gate * (out * lax.rsqrt(ms + EPS) * g_post)


def _out0_kernel(x_ref, gate_ref, of_ref, ob_ref, r_ref, k_ref, v_ref, rg_ref, oat_ref, ag_ref,
                 p_ref, bd_ref, wo_ref, y_ref):
    o = of_ref[0] + ob_ref[0]
    c = o.shape[-1]
    inv = 1.0 / HEAD64
    mu = _seg_sum(o, bd_ref) * inv
    dlt = o - mu
    var = _seg_sum(dlt * dlt, bd_ref) * inv
    gn = dlt * lax.rsqrt(var + GN_EPS) * _prow(p_ref, P0_GNG, c) + _prow(p_ref, P0_GNB, c)
    bonus = _seg_sum(r_ref[0] * k_ref[0] * _prow(p_ref, P0_RK, c), bd_ref) * v_ref[0]
    o_rw = (gn + bonus) * _silu(rg_ref[0])
    o_at = oat_ref[0] * _silu(ag_ref[0])
    out = _mm(o_rw, wo_ref[0:c, :]) + _mm(o_at, wo_ref[c:2 * c, :])
    y_ref[0] = _postnorm_residual(x_ref[0], out, _prow(p_ref, P0_G_POST, x_ref.shape[-1]), gate_ref[0])


def _out0(x, gate, o_f, o_b, r, k, v, rg, o_at, ag, w, tm):
    bsz, t, d = x.shape
    c = 512
    tokd = pl.BlockSpec((1, tm, d), lambda b, i: (b, i, 0))
    tok = pl.BlockSpec((1, tm, c), lambda b, i: (b, i, 0))
    names = ("p", "bd", "wout")
    return pl.pallas_call(
        _out0_kernel,
        out_shape=jax.ShapeDtypeStruct((bsz, t, d), F32),
        grid=(bsz, t // tm),
        in_specs=[tokd, _mod_spec(gate, d)] + [tok] * 8
                 + [_const_spec(w[n].shape) for n in names],
        out_specs=tokd,
        compiler_params=pltpu.CompilerParams(dimension_semantics=("parallel", "parallel"),
                                             vmem_limit_bytes=VMEM_LIMIT),
        name="out_proj0",
    )(x, gate, o_f, o_b, r, k, v, rg, o_at, ag, *[w[n] for n in names])


(P1_G_PRE, P1_G_POST, P1_GB, P1_ALOG_F, P1_ALOG_B, P1_DTB_F, P1_DTB_B, P1_GLA_G, P1_GDN_G, P1_CONV) = range(10)
W1_COLS = (0, 256, 512, 1024, 1152, 1664, 3200, 3328, 3456, 3968)


def _in1_kernel(x_ref, sc_ref, sh_ref, p_ref, w_ref, g2_ref,
                q_ref, k_ref, v_ref, lw_ref, gg_ref, dqkv_ref, small_ref, dg_ref):
    dm = x_ref.shape[-1]
    h = _prenorm(x_ref[0], _prow(p_ref, P1_G_PRE, dm), sc_ref[0], sh_ref[0]).astype(BF16)
    c = W1_COLS
    proj = lambda j: jnp.dot(h, w_ref[:, c[j]:c[j + 1]], preferred_element_type=F32)
    q_ref[0] = proj(0) * (HEAD64 ** -0.5)
    k_ref[0] = proj(1)
    v_ref[0] = proj(2)
    gg_ref[0] = proj(4)
    dqkv_ref[0] = proj(5)
    small_ref[0, 0] = proj(6)
    small_ref[1, 0] = proj(7)
    dg_ref[0] = proj(8)
    lw = _log_sigmoid(_mm(proj(3), g2_ref[...]) + _prow(p_ref, P1_GB, g2_ref.shape[-1])) * (1.0 / GLA_GATE_NORM)
    half = lw.shape[-1] // 2
    lw_ref[0, 0] = lw[:, :half]
    lw_ref[1, 0] = lw[:, half:]


def _in1(x, scale1p, shift, w, tm):
    bsz, t, d = x.shape
    widths = [256, 256, 512, -256, 512, 1536, -LANES, 512]
    tok = lambda n: pl.BlockSpec((1, tm, n), lambda b, i: (b, i, 0))
    outs, ospecs = [], []
    for n in widths:
        if n < 0:
            outs.append(jax.ShapeDtypeStruct((2, bsz, t, -n), F32))
            ospecs.append(pl.BlockSpec((2, 1, tm, -n), lambda b, i: (0, b, i, 0)))
        else:
            outs.append(jax.ShapeDtypeStruct((bsz, t, n), F32))
            ospecs.append(tok(n))
    return pl.pallas_call(
        _in1_kernel,
        out_shape=outs,
        grid=(bsz, t // tm),
        in_specs=[tok(d), _mod_spec(scale1p, d), _mod_spec(shift, d), _const_spec(w["p"].shape),
                  _const_spec(w["win"].shape, single=True), _const_spec(w["g2"].shape)],
        out_specs=ospecs,
        compiler_params=pltpu.CompilerParams(dimension_semantics=("parallel", "parallel"),
                                             vmem_limit_bytes=VMEM_LIMIT),
        name="in_proj1",
    )(x, scale1p, shift, w["p"], w["win"], w["g2"])


def _gdnprep_kernel(n_blocks, x_ref, xp_ref, xn_ref, s_ref, p_ref, q_ref, k_ref, v_ref, g_ref):
    i = pl.program_id(1)
    x = x_ref[0]
    prev, nxt = _neighbours(x, xp_ref[0], xn_ref[0], i, n_blocks)
    cw = lambda j: _prow(p_ref, P1_CONV + j, x.shape[-1])
    y = _silu(prev * cw(0) + x * cw(1) + nxt * cw(2))
    c = 512
    qs, ks = [], []
    for h in range(c // HEAD128):
        qh = y[:, h * HEAD128:(h + 1) * HEAD128]
        kh = y[:, c + h * HEAD128:c + (h + 1) * HEAD128]
        qs.append(qh * lax.rsqrt(jnp.sum(qh * qh, axis=-1, keepdims=True) + EPS) * (HEAD128 ** -0.5))
        ks.append(kh * lax.rsqrt(jnp.sum(kh * kh, axis=-1, keepdims=True) + EPS))
    q_ref[0] = jnp.concatenate(qs, axis=-1)
    k_ref[0] = jnp.concatenate(ks, axis=-1)
    v_ref[0] = y[:, 2 * c:]
    lane = lax.broadcasted_iota(jnp.int32, (1, LANES), 1)
    nh = c // HEAD128
    for d, (ia, ib) in enumerate(((P1_ALOG_F, P1_DTB_F), (P1_ALOG_B, P1_DTB_B))):
        s = s_ref[d, 0]
        loga = -jnp.exp(_prow(p_ref, ia, LANES)) * _softplus(s + _prow(p_ref, ib, LANES))
        g_ref[d, 0] = jnp.where(lane < nh, loga, jnp.where(lane < 2 * nh, jax.nn.sigmoid(s), 0.0))


def _gdnprep(dqkv, small, w, tm):
    bsz, t, cs = dqkv.shape
    c = 512
    nb = t // tm
    tok = pl.BlockSpec((1, tm, c), lambda b, i: (b, i, 0))
    tokl = pl.BlockSpec((2, 1, tm, LANES), lambda b, i: (0, b, i, 0))
    return pl.pallas_call(
        functools.partial(_gdnprep_kernel, nb),
        out_shape=[jax.ShapeDtypeStruct((bsz, t, c), F32)] * 3 + [jax.ShapeDtypeStruct((2, bsz, t, LANES), F32)],
        grid=(bsz, nb),
        in_specs=_halo_specs(tm, cs, t) + [tokl, _const_spec(w["p"].shape)],
        out_specs=[tok] * 3 + [tokl],
        compiler_params=pltpu.CompilerParams(dimension_semantics=("parallel", "parallel"),
                                             vmem_limit_bytes=VMEM_LIMIT),
        name="gdn_prep",
    )(dqkv, dqkv, dqkv, small, w["p"])


def _glascan_kernel(nc, nsub, qf, kf, vf, lwf, qb, kb, vb, lwb, s0_ref, of_ref, ob_ref, sf_ref, st_ref):
    c = pl.program_id(1)
    L = CHUNK
    nh = qf.shape[-1] // HEAD64

    @pl.when(c == 0)
    def _():
        st_ref[...] = s0_ref[0]

    hmask = _lane_group_masks(nh * HEAD64, HEAD64, nh)
    rowv = lax.broadcasted_iota(jnp.int32, st_ref.shape[1:], 0)
    colk = lax.broadcasted_iota(jnp.int32, st_ref.shape[1:], 1)
    bd = (rowv >> 7) == (colk >> 6)
    refs = ((qf, kf, vf, lwf, of_ref), (qb, kb, vb, lwb, ob_ref))
    rows = lambda d, s: slice(s * L, (s + 1) * L) if d == 0 else slice((nsub - 1 - s) * L, (nsub - s) * L)
    masks = []
    for d in range(2):
        _, incl = _order_masks(L, d == 1)
        _, incl4 = _order_masks(L, d == 1, reps=nh)
        masks.append((jnp.where(incl, 1.0, 0.0).astype(BF16), incl4))

    units = [(s, d) for s in range(nsub) for d in range(2)]
    dat = []
    for s, d in units:
        q_, k_, v_, lw_, _ = refs[d]
        rw = rows(d, s)
        dat.append(dict(q=q_[0, rw, :], k=k_[0, rw, :], v=v_[0, rw, :], lw=lw_[0, 0, rw, :]))
    cws = [_mm_exact_l(masks[d][0], x["lw"]) for (_, d), x in zip(units, dat)]
    for x, cw in zip(dat, cws):
        c0 = cw[L // 2:L // 2 + 1, :]
        tot = jnp.sum(x["lw"], axis=0, keepdims=True)
        q_rel = x["q"] * jnp.exp(cw - c0)
        k_rel = x["k"] * jnp.exp(c0 - cw)
        x.update(sq=_stack_heads(q_rel, hmask), k_rel=k_rel, q_abs=q_rel * jnp.exp(c0),
                 k_fin=k_rel * jnp.exp(tot - c0), w_tot=jnp.exp(tot))
    for (_, d), x in zip(units, dat):
        x["att"] = jnp.where(masks[d][1], _mm_nt(x["sq"], x["k_rel"]), 0.0)
        x["vk"] = jnp.where(bd, _mm_tn(x["v"], x["k_fin"]), 0.0)
    for x in dat:
        x["intra"] = jnp.concatenate(
            [_mm(x["att"][h * L:(h + 1) * L], x["v"][:, h * HEAD128:(h + 1) * HEAD128]) for h in range(nh)], axis=-1)

    state = [st_ref[0], st_ref[1]]
    for s in range(nsub):
        for d in range(2):
            x = dat[s * 2 + d]
            refs[d][4][0, rows(d, s), :] = x["intra"] + _mm_nt(x["q_abs"], state[d])
            state[d] = state[d] * x["w_tot"] + x["vk"]
    st_ref[0] = state[0]
    st_ref[1] = state[1]

    @pl.when(c == nc - 1)
    def _():
        sf_ref[0] = st_ref[...]


def _glascan(q, k, v, lw, s0):
    bsz, t, ck = q.shape
    cv = v.shape[-1]
    blk = GLA_SUB * CHUNK
    nc = t // blk
    kf, kb = _dir_specs((blk, ck), nc)
    vf, vb = _dir_specs((blk, cv), nc)
    lf, lb = _dir_specs((blk, ck), nc, lead=True)
    sspec = pl.BlockSpec((1, 2, cv, ck), lambda b_, c: (b_, 0, 0, 0))
    return pl.pallas_call(
        functools.partial(_glascan_kernel, nc, GLA_SUB),
        out_shape=[jax.ShapeDtypeStruct((bsz, t, cv), F32)] * 2 + [jax.ShapeDtypeStruct((bsz, 2, cv, ck), F32)],
        grid=(bsz, nc),
        in_specs=[kf, kf, vf, lf, kb, kb, vb, lb, sspec],
        out_specs=[vf, vb, sspec],
        scratch_shapes=[pltpu.VMEM((2, cv, ck), F32)],
        compiler_params=pltpu.CompilerParams(dimension_semantics=("parallel", "arbitrary"),
                                             vmem_limit_bytes=VMEM_LIMIT),
        name="gla_scan",
    )(q, k, v, lw, q, k, v, lw, s0)


def _gdnscan_kernel(nc, nsub, qf, kf, vf, gcf, qb, kb, vb, gcb, s0_ref, of_ref, ob_ref, sf_ref, st_ref):
    c = pl.program_id(1)
    L = CHUNK
    nh = qf.shape[-1] // HEAD128

    @pl.when(c == 0)
    def _():
        st_ref[...] = s0_ref[0]

    refs = ((qf, kf, vf, gcf), (qb, kb, vb, gcb))
    rows = lambda d, s: slice(s * L, (s + 1) * L) if d == 0 else slice((nsub - 1 - s) * L, (nsub - s) * L)
    per_dir = {}
    for d in range(2):
        strict, incl = _order_masks(L, d == 1)
        _, incl_t = _order_masks(L, d == 0)
        incl_bf = jnp.where(incl, 1.0, 0.0).astype(BF16)
        incl_t_bf = jnp.where(incl_t, 1.0, 0.0).astype(BF16)
        for s in range(nsub):
            gcol = refs[d][3][0, 0, rows(d, s), :]
            per_dir[d, s] = dict(strict=strict, incl=incl, gcol=gcol,
                                 cum_c=_mm_exact_l(incl_bf, gcol),
                                 cum_r=_mm_exact_tn(gcol, incl_t_bf),
                                 tot=jnp.sum(gcol, axis=0, keepdims=True))

    units = [(s, d, h) for s in range(nsub) for d in range(2) for h in range(nh)]
    dat = []
    for s, d, h in units:
        hs = slice(h * HEAD128, (h + 1) * HEAD128)
        pd = per_dir[d, s]
        rw = rows(d, s)
        q, k, v = refs[d][0][0, rw, hs], refs[d][1][0, rw, hs], refs[d][2][0, rw, hs]
        g = pd["cum_c"][:, h:h + 1]
        beta = pd["gcol"][:, nh + h:nh + h + 1]
        g_last = pd["tot"][:, h:h + 1]
        gam = jnp.exp(g)
        dat.append(dict(q=q, k=k, qg=q * gam, beta=beta, pd=pd,
                        dec=jnp.exp(jnp.minimum(g - pd["cum_r"][h:h + 1, :], 0.0)),
                        rhs=jnp.concatenate([(beta * gam) * k, beta * v], axis=-1),
                        k_dec=k * jnp.exp(g_last - g), gl=jnp.exp(g_last)))
    for x in dat:
        x["kk"] = _mm_nt(x["k"], x["k"])
        x["aqk"] = jnp.where(x["pd"]["incl"], x["dec"] * _mm_nt(x["q"], x["k"]), 0.0)
    tinv = _tri_inv_many([jnp.where(x["pd"]["strict"], x["dec"] * x["kk"] * x["beta"], 0.0) for x in dat], _mm)
    for x, t in zip(dat, tinv):
        x["wu"] = _mm(t, x["rhs"])
        x["wq"] = jnp.concatenate([x["wu"][:, :HEAD128], x["qg"]], axis=0)

    state = {(d, h): st_ref[d, h] for d in range(2) for h in range(nh)}
    for s in range(nsub):
        cur = [(d, h, dat[(s * 2 + d) * nh + h]) for d in range(2) for h in range(nh)]
        for d, h, x in cur:
            x["ws"] = _mm(x["wq"], state[d, h])
        for d, h, x in cur:
            x["u"] = x["wu"][:, HEAD128:] - x["ws"][0:L]
        for d, h, x in cur:
            x["o"] = x["ws"][L:2 * L] + _mm(x["aqk"], x["u"])
            state[d, h] = state[d, h] * x["gl"] + _mm_tn(x["k_dec"], x["u"])
        for d, o_ref in ((0, of_ref), (1, ob_ref)):
            o_ref[0, rows(d, s), :] = jnp.concatenate([x["o"] for dd, _, x in cur if dd == d], axis=-1)
    for (d, h), val in state.items():
        st_ref[d, h] = val

    @pl.when(c == nc - 1)
    def _():
        sf_ref[0] = st_ref[...]


def _gdnscan(q, k, v, gcol, s0):
    bsz, t, cdim = q.shape
    nh = cdim // HEAD128
    blk = GDN_SUB * CHUNK
    nc = t // blk
    tf, tb = _dir_specs((blk, cdim), nc)
    gcf, gcb = _dir_specs((blk, LANES), nc, lead=True)
    sspec = pl.BlockSpec((1, 2, nh, HEAD128, HEAD128), lambda b_, c: (b_, 0, 0, 0, 0))
    return pl.pallas_call(
        functools.partial(_gdnscan_kernel, nc, GDN_SUB),
        out_shape=[jax.ShapeDtypeStruct((bsz, t, cdim), F32)] * 2
                  + [jax.ShapeDtypeStruct((bsz, 2, nh, HEAD128, HEAD128), F32)],
        grid=(bsz, nc),
        in_specs=[tf, tf, tf, gcf, tb, tb, tb, gcb, sspec],
        out_specs=[tf, tb, sspec],
        scratch_shapes=[pltpu.VMEM((2, nh, HEAD128, HEAD128), F32)],
        compiler_params=pltpu.CompilerParams(dimension_semantics=("parallel", "arbitrary"),
                                             vmem_limit_bytes=VMEM_LIMIT),
        name="gdn_scan",
    )(q, k, v, gcol, q, k, v, gcol, s0)


def _head_rmsnorm(o, g):
    parts = []
    for h in range(o.shape[-1] // HEAD128):
        oh = o[:, h * HEAD128:(h + 1) * HEAD128]
        parts.append(oh * lax.rsqrt(jnp.mean(oh * oh, axis=-1, keepdims=True) + EPS) * g)
    return jnp.concatenate(parts, axis=-1)


def _out1_kernel(x_ref, gate_ref, glf_ref, glb_ref, gdf_ref, gdb_ref, gg_ref, dg_ref, p_ref, wo_ref, y_ref):
    c = gg_ref.shape[-1]
    o_gla = _head_rmsnorm(glf_ref[0] + glb_ref[0], _prow(p_ref, P1_GLA_G, HEAD128)) * _silu(gg_ref[0])
    o_gdn = _head_rmsnorm(gdf_ref[0] + gdb_ref[0], _prow(p_ref, P1_GDN_G, HEAD128)) * _silu(dg_ref[0])
    out = _mm(o_gla, wo_ref[0:c, :]) + _mm(o_gdn, wo_ref[c:2 * c, :])
    y_ref[0] = _postnorm_residual(x_ref[0], out, _prow(p_ref, P1_G_POST, x_ref.shape[-1]), gate_ref[0])


def _out1(x, gate, gla_f, gla_b, gdn_f, gdn_b, gg, dg, w, tm):
    bsz, t, d = x.shape
    c = 512
    tokd = pl.BlockSpec((1, tm, d), lambda b, i: (b, i, 0))
    tok = pl.BlockSpec((1, tm, c), lambda b, i: (b, i, 0))
    names = ("p", "wout")
    return pl.pallas_call(
        _out1_kernel,
        out_shape=jax.ShapeDtypeStruct((bsz, t, d), F32),
        grid=(bsz, t // tm),
        in_specs=[tokd, _mod_spec(gate, d)] + [tok] * 6
                 + [_const_spec(w[n].shape) for n in names],
        out_specs=tokd,
        compiler_params=pltpu.CompilerParams(dimension_semantics=("parallel", "parallel"),
                                             vmem_limit_bytes=VMEM_LIMIT),
        name="out_proj1",
    )(x, gate, gla_f, gla_b, gdn_f, gdn_b, gg, dg, *[w[n] for n in names])


def _block_ones(n_groups, width):
    return jnp.asarray(np.kron(np.eye(n_groups, dtype=np.float32), np.ones((width, width), np.float32)), BF16)


def _rope_tables(t):
    rows = t // GRID_W
    row_id = np.repeat(np.arange(rows, dtype=np.float32), GRID_W)
    col_id = np.tile(np.arange(GRID_W, dtype=np.float32), rows)
    nf = HEAD64 // 4
    inv = np.float32(ROPE_THETA) ** (-np.arange(nf, dtype=np.float32) / np.float32(nf))
    ang_r = (row_id[:, None] * inv[None, :]).astype(np.float32)
    ang_c = (col_id[:, None] * inv[None, :]).astype(np.float32)
    cos = np.concatenate([np.cos(ang_r)] * 2 + [np.cos(ang_c)] * 2, axis=-1)
    sin = np.concatenate([-np.sin(ang_r), np.sin(ang_r), -np.sin(ang_c), np.sin(ang_c)], axis=-1)
    tile2 = lambda a: jnp.asarray(np.concatenate([a, a], axis=-1).astype(np.float32))
    return tile2(cos), tile2(sin)


def _pack_rows(rows, width):
    return jnp.stack([jnp.pad(r.reshape(-1).astype(F32), (0, width - r.size)) for r in rows])


def _layer0_weights(p):
    w_in = p["w_in"]
    rw_w = 512
    slab_w = 3 * rw_w + 192
    d = w_in.shape[0]
    w = {}
    w["win"] = jnp.concatenate([w_in[:, :slab_w], jnp.zeros((d, W0_COLS[1] - slab_w), F32), w_in[:, slab_w:]],
                               axis=1).astype(BF16)
    w["wout"] = p["w_out"].astype(BF16)
    w["bdq"] = _block_ones(8, HEAD64)
    w["bdk"] = _block_ones(2, HEAD64)
    w["bd"] = w["bdq"]
    w["p"] = _pack_rows([p["g_pre"], p["g_post"], p["rw_mu"], jnp.concatenate([p["rw_w0_f"], p["rw_w0_b"]]),
                         p["rw_a0"], p["rw_k_k"], p["rw_k_a"], p["rw_r_k"], p["rw_gn_g"], p["rw_gn_b"],
                         jnp.tile(p["at_gq"], 8), jnp.tile(p["at_gk"], 2)], W0_COLS[1])
    z = jnp.zeros((64, rw_w), F32)
    w["w2"] = jnp.concatenate([jnp.concatenate([p["rw_w2_f"], z], axis=1),
                               jnp.concatenate([z, p["rw_w2_b"]], axis=1)], axis=0).astype(BF16)
    w["a2"] = jnp.concatenate([p["rw_a2"], z], axis=0).astype(BF16)
    return w


def _layer1_weights(p):
    w_in = p["w_in"]
    d = w_in.shape[0]
    zc = lambda n: jnp.zeros((d, n), F32)
    af, ab, be = w_in[:, 3104:3108], w_in[:, 3108:3112], w_in[:, 3112:3116]
    w = {}
    w["win"] = jnp.concatenate([w_in[:, :1024], w_in[:, 1024:1056], zc(LANES - 32), w_in[:, 1056:3104],
                                af, be, zc(LANES - 8), ab, be, zc(LANES - 8), w_in[:, 3116:]],
                               axis=1).astype(BF16)
    w["wout"] = p["w_out"].astype(BF16)
    z = jnp.zeros((16, 256), F32)
    g2 = jnp.concatenate([jnp.concatenate([p["gla_g2_f"], z], axis=1),
                          jnp.concatenate([z, p["gla_g2_b"]], axis=1)], axis=0)
    w["g2"] = jnp.pad(g2, ((0, LANES - 32), (0, 0))).astype(BF16)
    conv = p["gdn_conv"]
    w["p"] = _pack_rows([p["g_pre"], p["g_post"], jnp.concatenate([p["gla_gb_f"], p["gla_gb_b"]]),
                         p["gdn_A_log_f"], p["gdn_A_log_b"], p["gdn_dt_bias_f"], p["gdn_dt_bias_b"],
                         p["gla_norm_g"], p["gdn_norm_g"], conv[0], conv[1], conv[2]], conv.shape[1])
    return w


def _rw_state_to_bd(s):
    bsz = s.shape[0]
    s = s.reshape(bsz, 2, 4, HEAD64, HEAD64)
    eye = jnp.eye(4, dtype=s.dtype)
    return jnp.einsum("bqhvk,hg->bqhvgk", s, eye).reshape(bsz, 2, QUAD, QUAD)


def _rw_state_from_bd(s):
    bsz = s.shape[0]
    s = s.reshape(bsz, 2, 4, HEAD64, 4, HEAD64)
    return jnp.stack([s[:, :, h, :, h, :] for h in range(4)], axis=2).reshape(bsz, 8, HEAD64, HEAD64)


def _gla_state_to_bd(s):
    bsz = s.shape[0]
    eye = jnp.eye(4, dtype=s.dtype)
    return jnp.einsum("bhkv,hg->bhvgk", s, eye).reshape(bsz, 4 * HEAD128, 4 * HEAD64)


def _gla_state_from_bd(s):
    bsz = s.shape[0]
    s = s.reshape(bsz, 4, HEAD128, 4, HEAD64)
    return jnp.stack([jnp.swapaxes(s[:, h, :, h, :], -1, -2) for h in range(4)], axis=1)


def _trunk(x, mods, w0, w1, ctx, tm, tq):
    bsz, t, dm = x.shape
    latent = ctx is not None
    ts = min(tm, t)
    shared = mods[0][0].shape[0] == 1
    flat = (lambda a: a.reshape(a.shape[:-3] + (1, bsz * t, a.shape[-1]))) if shared else (lambda a: a)
    unflat = lambda a: a.reshape(a.shape[:-3] + (bsz, t, a.shape[-1]))

    sc, sh, gt = mods[0]
    slab, rg, q, k, v, ag = [unflat(a) for a in _in0(flat(x), sc, sh, w0, latent, tm)]
    r, kmod, vr, kkn, bvec, lw = _rwprep(slab, w0, ts)
    if latent:
        s0f, s0b, k_ctx, v_ctx = ctx[:4]
        s0 = jnp.stack([_rw_state_to_bd(s0f), _rw_state_to_bd(s0b)], axis=1)
        k_all = jnp.concatenate([k_ctx.reshape(bsz, -1, 2 * HEAD64), k], axis=1)
        v_all = jnp.concatenate([v_ctx.reshape(bsz, -1, 2 * HEAD64), v], axis=1)
    else:
        s0 = jnp.zeros((bsz, 2, 2, QUAD, QUAD), F32)
        k_all, v_all = k, v
    o_rwf, o_rwb, s_rw = _rwscan(r, kmod, vr, kkn, bvec, lw, s0)
    vt_all = jnp.concatenate([jnp.swapaxes(v_all, 1, 2), jnp.ones((bsz, 16, v_all.shape[1]), F32)], axis=1)
    o_at = _attention(q, k_all.astype(BF16), vt_all.astype(BF16), tq)
    x1 = _out0(flat(x), gt, *[flat(a) for a in (o_rwf, o_rwb, r, kmod, vr, rg, o_at, ag)], w0, tm)

    sc, sh, gt = mods[1]
    gq, gk, gv, glw, gg, dqkv, small, dg = [unflat(a) for a in _in1(x1, sc, sh, w1, tm)]
    dq, dk, dv, gcol = _gdnprep(dqkv, small, w1, ts)
    if latent:
        sgf, sgb, sdf, sdb = ctx[4:]
        s0_gla = jnp.stack([_gla_state_to_bd(sgf), _gla_state_to_bd(sgb)], axis=1)
        s0_gdn = jnp.stack([sdf, sdb], axis=1)
    else:
        s0_gla = jnp.zeros((bsz, 2, 4 * HEAD128, 4 * HEAD64), F32)
        s0_gdn = jnp.zeros((bsz, 2, 4, HEAD128, HEAD128), F32)
    gla_f, gla_b, s_gla = _glascan(gq, gk, gv, glw, s0_gla)
    gdn_f, gdn_b, s_gdn = _gdnscan(dq, dk, dv, gcol, s0_gdn)
    y = unflat(_out1(x1, gt, *[flat(a) for a in (gla_f, gla_b, gdn_f, gdn_b, gg, dg)], w1, tm))

    new = None
    if not latent:
        new = (_rw_state_from_bd(s_rw[:, 0]), _rw_state_from_bd(s_rw[:, 1]),
               k.reshape(bsz, t, 2, HEAD64), v.reshape(bsz, t, 2, HEAD64),
               _gla_state_from_bd(s_gla[:, 0]), _gla_state_from_bd(s_gla[:, 1]), s_gdn[:, 0], s_gdn[:, 1])
    return y, new


def _split_mod(m, d):
    shift, scale, gate = m[..., :d], m[..., d:2 * d], m[..., 2 * d:]
    return (1.0 + scale)[:, None, :], shift[:, None, :], gate[:, None, :]


def kernel(x_prompt, x_sample, state_l0_rwkv_fwd, state_l0_rwkv_bwd, cache_l0_k, cache_l0_v, state_l1_gla_fwd, state_l1_gla_bwd, state_l1_gdn_fwd, state_l1_gdn_bwd, c, c_ctx, l0_mod_w, l0_mod_b, l0_g_pre, l0_g_post, l0_w_in, l0_w_out, l0_rw_mu, l0_rw_w0_f, l0_rw_w2_f, l0_rw_w0_b, l0_rw_w2_b, l0_rw_a0, l0_rw_a2, l0_rw_k_k, l0_rw_k_a, l0_rw_r_k, l0_rw_gn_g, l0_rw_gn_b, l0_at_gq, l0_at_gk, l1_mod_w, l1_mod_b, l1_g_pre, l1_g_post, l1_w_in, l1_w_out, l1_gla_g2_f, l1_gla_gb_f, l1_gla_g2_b, l1_gla_gb_b, l1_gla_norm_g, l1_gdn_conv, l1_gdn_A_log_f, l1_gdn_dt_bias_f, l1_gdn_A_log_b, l1_gdn_dt_bias_b, l1_gdn_norm_g):
    p0 = {"g_pre": l0_g_pre, "g_post": l0_g_post, "w_in": l0_w_in, "w_out": l0_w_out, "rw_mu": l0_rw_mu,
          "rw_w0_f": l0_rw_w0_f, "rw_w2_f": l0_rw_w2_f, "rw_w0_b": l0_rw_w0_b, "rw_w2_b": l0_rw_w2_b,
          "rw_a0": l0_rw_a0, "rw_a2": l0_rw_a2, "rw_k_k": l0_rw_k_k, "rw_k_a": l0_rw_k_a,
          "rw_r_k": l0_rw_r_k, "rw_gn_g": l0_rw_gn_g, "rw_gn_b": l0_rw_gn_b,
          "at_gq": l0_at_gq, "at_gk": l0_at_gk}
    p1 = {"g_pre": l1_g_pre, "g_post": l1_g_post, "w_in": l1_w_in, "w_out": l1_w_out,
          "gla_g2_f": l1_gla_g2_f, "gla_gb_f": l1_gla_gb_f, "gla_g2_b": l1_gla_g2_b,
          "gla_gb_b": l1_gla_gb_b, "gla_norm_g": l1_gla_norm_g, "gdn_conv": l1_gdn_conv,
          "gdn_A_log_f": l1_gdn_A_log_f, "gdn_dt_bias_f": l1_gdn_dt_bias_f,
          "gdn_A_log_b": l1_gdn_A_log_b, "gdn_dt_bias_b": l1_gdn_dt_bias_b,
          "gdn_norm_g": l1_gdn_norm_g}
    d = x_prompt.shape[-1]
    nb = c.shape[0]
    w0 = _layer0_weights(p0)
    w1 = _layer1_weights(p1)
    cos, sin = _rope_tables(x_sample.shape[1])
    w0["cos"], w0["sin"] = cos, sin

    cvec = jnp.concatenate([c, c_ctx[None, :], jnp.zeros((SUBLANES - nb - 1, d), F32)], axis=0)
    m0 = _modulation(cvec, l0_mod_w, l0_mod_b)
    m1 = _modulation(cvec, l1_mod_w, l1_mod_b)
    mods_lat = [_split_mod(m[:nb], d) for m in (m0, m1)]
    mods_ctx = [_split_mod(m[nb:nb + 1], d) for m in (m0, m1)]

    tm = 512
    y_prompt, new = _trunk(x_prompt, mods_ctx, w0, w1, None, tm, min(512, x_prompt.shape[1]))
    ctx = (state_l0_rwkv_fwd, state_l0_rwkv_bwd, cache_l0_k, cache_l0_v,
           state_l1_gla_fwd, state_l1_gla_bwd, state_l1_gdn_fwd, state_l1_gdn_bwd)
    y_sample, _ = _trunk(x_sample, mods_lat, w0, w1, ctx, tm, min(512, x_sample.shape[1]))
    return (y_prompt, y_sample) + tuple(new)
```

```python
import functools

import numpy as np
import jax
import jax.numpy as jnp
from jax import lax
from jax.experimental import pallas as pl
from jax.experimental.pallas import tpu as pltpu

F32 = jnp.float32
BF16 = jnp.bfloat16

EPS = 1e-6
GN_EPS = 64e-5
CHUNK = 64
GRID_W = 64
ROPE_THETA = 10000.0
RW_DECAY_SCALE = 0.6065306597126334
GLA_GATE_NORM = 16.0
LOG2E = 1.4426950408889634
HEAD64 = 64
HEAD128 = 128
LANES = 128
SUBLANES = 8
VMEM_LIMIT = 56 * 1024 * 1024


def _mm(a, b):
    return jnp.dot(a.astype(BF16), b.astype(BF16), preferred_element_type=F32)


def _mm_nt(a, b):
    return lax.dot_general(a.astype(BF16), b.astype(BF16), (((1,), (1,)), ((), ())),
                           preferred_element_type=F32)


def _mm_tn(a, b):
    return lax.dot_general(a.astype(BF16), b.astype(BF16), (((0,), (0,)), ((), ())),
                           preferred_element_type=F32)


def _split3(x):
    hi = x.astype(BF16)
    r1 = x - hi.astype(F32)
    mid = r1.astype(BF16)
    lo = (r1 - mid.astype(F32)).astype(BF16)
    return hi, mid, lo


def _mm_exact_l(mask_bf16, x):
    hi, mid, lo = _split3(x)
    d = functools.partial(jnp.dot, preferred_element_type=F32)
    return d(mask_bf16, hi) + d(mask_bf16, mid) + d(mask_bf16, lo)


def _mm_exact_tn(x, mask_bf16):
    hi, mid, lo = _split3(x)
    d = lambda a: lax.dot_general(a, mask_bf16, (((0,), (0,)), ((), ())), preferred_element_type=F32)
    return d(hi) + d(mid) + d(lo)


def _mm3(a, b):
    ah = a.astype(BF16)
    al = (a - ah.astype(F32)).astype(BF16)
    bh = b.astype(BF16)
    bl = (b - bh.astype(F32)).astype(BF16)
    d = functools.partial(jnp.dot, preferred_element_type=F32)
    return d(ah, bh) + d(ah, bl) + d(al, bh)


def _silu(x):
    return x * jax.nn.sigmoid(x)


def _softplus(x):
    return jnp.maximum(x, 0.0) + jnp.log(1.0 + jnp.exp(-jnp.abs(x)))


def _log_sigmoid(x):
    return jnp.minimum(x, 0.0) - jnp.log(1.0 + jnp.exp(-jnp.abs(x)))


def _order_masks(n, reverse, reps=1):
    row = lax.broadcasted_iota(jnp.int32, (reps * n, n), 0) & (n - 1)
    col = lax.broadcasted_iota(jnp.int32, (reps * n, n), 1)
    d = (col - row) if reverse else (row - col)
    return d > 0, d >= 0


def _tri_inv_many(nmats, mm):
    n = nmats[0].shape[0]
    row = lax.broadcasted_iota(jnp.int32, (n, n), 0)
    col = lax.broadcasted_iota(jnp.int32, (n, n), 1)
    x = row ^ col
    eye = jnp.where(row == col, 1.0, 0.0).astype(F32)
    ts = [eye - jnp.where(x == 1, nm, 0.0) for nm in nmats]
    s = 1
    while (2 << s) <= n:
        lvl = (x >> s) == 1
        tcs = [mm(t, jnp.where(lvl, nm, 0.0)) for t, nm in zip(ts, nmats)]
        ts = [t - mm(tc, t) for t, tc in zip(ts, tcs)]
        s += 1
    return ts


def _lane_group_masks(width, group, count):
    lane = lax.broadcasted_iota(jnp.int32, (1, width), 1)
    return [((lane >= g * group) & (lane < (g + 1) * group)) for g in range(count)]


def _stack_heads(x, masks):
    return jnp.concatenate([jnp.where(m, x, 0.0) for m in masks], axis=0)


def _mod_kernel(c_ref, w_ref, b_ref, o_ref):
    s = _silu(c_ref[...])
    o_ref[...] = _mm3(s, w_ref[...]) + b_ref[...]


def _modulation(cvec, mod_w, mod_b):
    rows, d = cvec.shape
    n = mod_w.shape[1]
    tn = 1024
    return pl.pallas_call(
        _mod_kernel,
        out_shape=jax.ShapeDtypeStruct((rows, n), F32),
        grid=(n // tn,),
        in_specs=[pl.BlockSpec((rows, d), lambda j: (0, 0)),
                  pl.BlockSpec((d, tn), lambda j: (0, j)),
                  pl.BlockSpec((1, tn), lambda j: (0, j))],
        out_specs=pl.BlockSpec((rows, tn), lambda j: (0, j)),
        compiler_params=pltpu.CompilerParams(vmem_limit_bytes=VMEM_LIMIT),
        name="modulation",
    )(cvec, mod_w, mod_b.reshape(1, n))


def _prenorm(x, g, scale1p, shift):
    ms = jnp.mean(x * x, axis=-1, keepdims=True)
    return x * lax.rsqrt(ms + EPS) * g * scale1p + shift


def _seg_sum(x, bd_ref):
    hi = x.astype(BF16)
    lo = (x - hi.astype(F32)).astype(BF16)
    bd = bd_ref[...]
    return jnp.dot(hi, bd, preferred_element_type=F32) + jnp.dot(lo, bd, preferred_element_type=F32)


def _rope(x, cos, sin_signed):
    w = x.shape[-1]
    lane = lax.broadcasted_iota(jnp.int32, (1, w), 1)
    first = (lane & 31) < 16
    partner = jnp.where(first, pltpu.roll(x, w - 16, axis=1), pltpu.roll(x, 16, axis=1))
    return x * cos + partner * sin_signed


def _tile_lanes(x, reps):
    return jnp.concatenate([x] * reps, axis=-1) if reps > 1 else x


P0_G_PRE, P0_G_POST, P0_MU, P0_W0, P0_A0, P0_KK, P0_KA, P0_RK, P0_GNG, P0_GNB, P0_GQ, P0_GK = range(12)
W0_COLS = (0, 1792, 2304, 2816, 2944, 3072, 3584)


def _prow(p_ref, i, n):
    return p_ref[i:i + 1, 0:n]


def _in0_kernel(use_rope, x_ref, sc_ref, sh_ref, p_ref, w_ref, bdq_ref, bdk_ref, cos_ref, sin_ref,
                slab_ref, rg_ref, q_ref, k_ref, v_ref, ag_ref):
    dm = x_ref.shape[-1]
    h = _prenorm(x_ref[0], _prow(p_ref, P0_G_PRE, dm), sc_ref[0], sh_ref[0]).astype(BF16)
    c = W0_COLS
    proj = lambda j: jnp.dot(h, w_ref[:, c[j]:c[j + 1]], preferred_element_type=F32)
    slab_ref[0] = proj(0)
    rg_ref[0] = proj(1)
    v_ref[0] = proj(4)
    ag_ref[0] = proj(5)
    q = proj(2)
    k = proj(3)
    q = q * lax.rsqrt(_seg_sum(q * q, bdq_ref) * (1.0 / HEAD64) + EPS) * _prow(p_ref, P0_GQ, q.shape[-1])
    k = k * lax.rsqrt(_seg_sum(k * k, bdk_ref) * (1.0 / HEAD64) + EPS) * _prow(p_ref, P0_GK, k.shape[-1])
    if use_rope:
        cos = cos_ref[...]
        sin = sin_ref[...]
        q = _rope(q, _tile_lanes(cos, q.shape[-1] // LANES), _tile_lanes(sin, q.shape[-1] // LANES))
        k = _rope(k, cos, sin)
    q_ref[0] = q
    k_ref[0] = k


def _const_spec(shape, single=False):
    nd = len(shape)
    if single:
        return pl.BlockSpec(shape, lambda *_: (0,) * nd, pipeline_mode=pl.Buffered(1))
    return pl.BlockSpec(shape, lambda *_: (0,) * nd)


def _mod_spec(arr, d):
    if arr.shape[0] == 1:
        return pl.BlockSpec((1, 1, d), lambda b, i: (0, 0, 0))
    return pl.BlockSpec((1, 1, d), lambda b, i: (b, 0, 0))


def _in0(x, scale1p, shift, w, use_rope, tm):
    bsz, t, d = x.shape
    widths = [W0_COLS[j + 1] - W0_COLS[j] for j in range(6)]
    outs = [jax.ShapeDtypeStruct((bsz, t, n), F32) for n in widths]
    tok = lambda n: pl.BlockSpec((1, tm, n), lambda b, i: (b, i, 0))
    in_specs = [tok(d), _mod_spec(scale1p, d), _mod_spec(shift, d), _const_spec(w["p"].shape),
                _const_spec(w["win"].shape, single=True), _const_spec(w["bdq"].shape), _const_spec(w["bdk"].shape)]
    in_specs += [pl.BlockSpec((tm, LANES), lambda b, i: (i, 0))] * 2
    return pl.pallas_call(
        functools.partial(_in0_kernel, use_rope),
        out_shape=outs,
        grid=(bsz, t // tm),
        in_specs=in_specs,
        out_specs=[tok(n) for n in widths],
        compiler_params=pltpu.CompilerParams(dimension_semantics=("parallel", "parallel"),
                                             vmem_limit_bytes=VMEM_LIMIT),
        name="in_proj0",
    )(x, scale1p, shift, w["p"], w["win"], w["bdq"], w["bdk"], w["cos"], w["sin"])


def _neighbours(x, prev_blk, next_blk, i, n_blocks):
    tm = x.shape[0]
    row = lax.broadcasted_iota(jnp.int32, (tm, 1), 0)
    prev_row = jnp.where(i == 0, 0.0, prev_blk[SUBLANES - 1:SUBLANES, :])
    next_row = jnp.where(i == n_blocks - 1, 0.0, next_blk[0:1, :])
    prev = jnp.where(row == 0, prev_row, pltpu.roll(x, 1, axis=0))
    nxt = jnp.where(row == tm - 1, next_row, pltpu.roll(x, tm - 1, axis=0))
    return prev, nxt


def _halo_specs(tm, c, t):
    r = tm // SUBLANES
    last = t // SUBLANES - 1
    main = pl.BlockSpec((1, tm, c), lambda b, i: (b, i, 0))
    prev = pl.BlockSpec((1, SUBLANES, c), lambda b, i: (b, jnp.maximum(i * r - 1, 0), 0))
    nxt = pl.BlockSpec((1, SUBLANES, c), lambda b, i: (b, jnp.minimum((i + 1) * r, last), 0))
    return [main, prev, nxt]


def _rwprep_kernel(n_blocks, s_ref, sp_ref, sn_ref, p_ref, w2_ref, a2_ref, bd_ref,
                   r_ref, k_ref, v_ref, kkn_ref, b_ref, lw_ref):
    i = pl.program_id(1)
    x = s_ref[0]
    prev, nxt = _neighbours(x, sp_ref[0], sn_ref[0], i, n_blocks)
    xs = x + _prow(p_ref, P0_MU, x.shape[-1]) * (0.5 * (prev + nxt) - x)
    c = 512
    r, kr, vr = xs[:, :c], xs[:, c:2 * c], xs[:, 2 * c:3 * c]
    lw_in = xs[:, 3 * c:3 * c + LANES]
    la_in = xs[:, 3 * c + LANES:3 * c + 2 * LANES]
    lor = _mm(jnp.tanh(lw_in), w2_ref[...]) + _prow(p_ref, P0_W0, 2 * c)
    logw = -RW_DECAY_SCALE * jax.nn.sigmoid(lor)
    a = jax.nn.sigmoid(_prow(p_ref, P0_A0, c) + _mm(la_in, a2_ref[...]))
    kkp = kr * _prow(p_ref, P0_KK, c)
    kkn = kkp * lax.rsqrt(_seg_sum(kkp * kkp, bd_ref) + EPS)
    r_ref[0] = r
    k_ref[0] = kr * (1.0 + (a - 1.0) * _prow(p_ref, P0_KA, c))
    v_ref[0] = vr
    kkn_ref[0] = kkn
    b_ref[0] = kkn * a
    lw_ref[0, 0] = logw[:, :c]
    lw_ref[1, 0] = logw[:, c:]


def _rwprep(slab, w, tm):
    bsz, t, cs = slab.shape
    c = 512
    nb = t // tm
    tok = pl.BlockSpec((1, tm, c), lambda b, i: (b, i, 0))
    outs = [jax.ShapeDtypeStruct((bsz, t, c), F32)] * 5 + [jax.ShapeDtypeStruct((2, bsz, t, c), F32)]
    names = ("p", "w2", "a2", "bd")
    return pl.pallas_call(
        functools.partial(_rwprep_kernel, nb),
        out_shape=outs,
        grid=(bsz, nb),
        in_specs=_halo_specs(tm, cs, t) + [_const_spec(w[k].shape) for k in names],
        out_specs=[tok] * 5 + [pl.BlockSpec((2, 1, tm, c), lambda b, i: (0, b, i, 0))],
        compiler_params=pltpu.CompilerParams(dimension_semantics=("parallel", "parallel"),
                                             vmem_limit_bytes=VMEM_LIMIT),
        name="rwkv_prep",
    )(slab, slab, slab, *[w[k] for k in names])


QUAD = 4 * HEAD64


def _sum_head_blocks(x, hmask, L):
    acc = jnp.where(hmask[0], x[0:L], 0.0)
    for h in range(1, len(hmask)):
        acc = acc + jnp.where(hmask[h], x[h * L:(h + 1) * L], 0.0)
    return acc


RW_SUB = 2
GDN_SUB = 4
GLA_SUB = 4


def _rwscan_kernel(nc, nsub, rf, kf, vf, kkf, bf, lwf, rb, kb, vb, kkb, bb, lwb, s0_ref,
                   of_ref, ob_ref, sf_ref, st_ref):
    c = pl.program_id(1)
    L = CHUNK
    nq = rf.shape[-1] // QUAD

    @pl.when(c == 0)
    def _():
        st_ref[...] = s0_ref[0]

    hmask = _lane_group_masks(QUAD, HEAD64, 4)
    rowq = lax.broadcasted_iota(jnp.int32, (QUAD, QUAD), 0)
    colq = lax.broadcasted_iota(jnp.int32, (QUAD, QUAD), 1)
    bd = (rowq >> 6) == (colq >> 6)
    masks = []
    rowc = lax.broadcasted_iota(jnp.int32, (L, QUAD), 0)
    colc = lax.broadcasted_iota(jnp.int32, (L, QUAD), 1) & (L - 1)
    for rev in (False, True):
        _, incl = _order_masks(L, rev)
        dlt = (colc - rowc) if rev else (rowc - colc)
        masks.append((jnp.where(incl, 1.0, 0.0).astype(BF16), dlt > 0, dlt >= 0))
    refs = ((rf, kf, vf, kkf, bf, lwf), (rb, kb, vb, kkb, bb, lwb))

    rows = lambda d, s: slice(s * L, (s + 1) * L) if d == 0 else slice((nsub - 1 - s) * L, (nsub - s) * L)
    units = [(s, d, qd) for s in range(nsub) for d in range(2) for qd in range(nq)]

    dat = []
    for s, d, qd in units:
        sl = slice(qd * QUAD, (qd + 1) * QUAD)
        rw = rows(d, s)
        r_, k_, v_, kk_, b_, lw_ = refs[d]
        dat.append(dict(r=r_[0, rw, sl], k=k_[0, rw, sl], v=v_[0, rw, sl], kk=kk_[0, rw, sl], b=b_[0, rw, sl],
                        lw=lw_[0, 0, rw, sl]))
    cws = [_mm_exact_l(masks[d][0], x["lw"]) for (_, d, _), x in zip(units, dat)]
    for x, cw in zip(dat, cws):
        c0 = cw[L // 2:L // 2 + 1, :]
        tot = jnp.sum(x["lw"], axis=0, keepdims=True)
        e_rel = jnp.exp(cw - c0)
        e_inv = jnp.exp(c0 - cw)
        ec0 = jnp.exp(c0)
        e_fin = jnp.exp(tot - c0)
        a_rel = x["kk"] * e_rel * jnp.exp(-x["lw"])
        r_rel = x["r"] * e_rel
        b_rel = x["b"] * e_inv
        k_rel = x["k"] * e_inv
        x.update(ar_abs=jnp.concatenate([a_rel * ec0, r_rel * ec0], axis=0),
                 ar_rel=jnp.concatenate([a_rel, r_rel], axis=0),
                 sbk=jnp.concatenate([_stack_heads(b_rel, hmask), _stack_heads(k_rel, hmask)], axis=0),
                 bk_fin=jnp.concatenate([b_rel * e_fin, k_rel * e_fin], axis=0), w_tot=jnp.exp(tot),
                 sv=_stack_heads(x["v"], hmask))
    for (_, d, _), x in zip(units, dat):
        _, strict_c, incl_c = masks[d]
        xx = _mm_nt(x["ar_rel"], x["sbk"])
        x["n_c"] = jnp.where(strict_c, xx[0:L, 0:QUAD], 0.0)
        x["mp"] = jnp.concatenate([jnp.where(strict_c, xx[0:L, QUAD:], 0.0),
                                   jnp.where(incl_c, xx[L:, QUAD:], 0.0)], axis=0)
        x["p_rb"] = jnp.where(incl_c, xx[L:, 0:QUAD], 0.0)
    tinv = _tri_inv_many([x["n_c"][:, h * L:(h + 1) * L] for x in dat for h in range(4)], _mm)
    for i, x in enumerate(dat):
        x["t_c"] = jnp.concatenate(tinv[4 * i:4 * i + 4], axis=-1)
        x["mpv"] = _mm(x["mp"], x["sv"])

    state = {(d, qd): st_ref[d, qd] for d in range(2) for qd in range(nq)}
    for s in range(nsub):
        cur = [(d, qd, dat[(s * 2 + d) * nq + qd]) for d in range(2) for qd in range(nq)]
        for d, qd, x in cur:
            x["ars"] = _mm_nt(x["ar_abs"], state[d, qd])
        for d, qd, x in cur:
            rhs = -x["ars"][0:L] - x["mpv"][0:L]
            x["e"] = _mm(x["t_c"], _stack_heads(rhs, hmask))
        for d, qd, x in cur:
            x["o"] = x["ars"][L:2 * L] + x["mpv"][L:2 * L] + _mm(x["p_rb"], _stack_heads(x["e"], hmask))
            ev = jnp.concatenate([x["e"], x["v"]], axis=0)
            state[d, qd] = state[d, qd] * x["w_tot"] + jnp.where(bd, _mm_tn(ev, x["bk_fin"]), 0.0)
        for d, o_ref in ((0, of_ref), (1, ob_ref)):
            o_ref[0, rows(d, s), :] = jnp.concatenate([x["o"] for dd, _, x in cur if dd == d], axis=-1)
    for (d, qd), val in state.items():
        st_ref[d, qd] = val

    @pl.when(c == nc - 1)
    def _():
        sf_ref[0] = st_ref[...]


def _dir_specs(block, nc, lead=None):
    specs = []
    for d in range(2):
        idx = (lambda b_, c: c) if d == 0 else (lambda b_, c: nc - 1 - c)
        if lead is None:
            specs.append(pl.BlockSpec((1,) + block, functools.partial(
                lambda f, b_, c: (b_, f(b_, c)) + (0,) * (len(block) - 1), idx)))
        else:
            specs.append(pl.BlockSpec((1, 1) + block, functools.partial(
                lambda f, dd, b_, c: (dd, b_, f(b_, c)) + (0,) * (len(block) - 1), idx, d)))
    return specs


def _rwscan(r, k, v, kk, b, lw, s0):
    bsz, t, cdim = r.shape
    blk = RW_SUB * CHUNK
    nc = t // blk
    nq = cdim // QUAD
    tf, tb = _dir_specs((blk, cdim), nc)
    lf, lb = _dir_specs((blk, cdim), nc, lead=True)
    sspec = pl.BlockSpec((1, 2, nq, QUAD, QUAD), lambda b_, c: (b_, 0, 0, 0, 0))
    return pl.pallas_call(
        functools.partial(_rwscan_kernel, nc, RW_SUB),
        out_shape=[jax.ShapeDtypeStruct((bsz, t, cdim), F32)] * 2
                  + [jax.ShapeDtypeStruct((bsz, 2, nq, QUAD, QUAD), F32)],
        grid=(bsz, nc),
        in_specs=[tf] * 5 + [lf] + [tb] * 5 + [lb] + [sspec],
        out_specs=[tf, tb, sspec],
        scratch_shapes=[pltpu.VMEM((2, nq, QUAD, QUAD), F32)],
        compiler_params=pltpu.CompilerParams(dimension_semantics=("parallel", "arbitrary"),
                                             vmem_limit_bytes=VMEM_LIMIT),
        name="rwkv_scan",
    )(r, k, v, kk, b, lw, r, k, v, kk, b, lw, s0)


def _attn_kernel(n_kv, q_ref, k_ref, vt_ref, o_ref):
    n_heads = q_ref.shape[-1] // HEAD64
    group = n_heads // n_kv
    lane = lax.broadcasted_iota(jnp.int32, (1, LANES), 1)
    kmat = k_ref[0]
    ones = vt_ref[0, n_kv * HEAD64:, :]
    vts = [jnp.concatenate([vt_ref[0, kv * HEAD64:(kv + 1) * HEAD64, :], ones], axis=0) for kv in range(n_kv)]
    outs = []
    for h in range(n_heads):
        kv = h // group
        pair = q_ref[0, :, (h // 2) * LANES:(h // 2 + 1) * LANES] * (HEAD64 ** -0.5 * LOG2E)
        if (h % 2) != kv:
            pair = pltpu.roll(pair, HEAD64, axis=1)
        qh = jnp.where((lane >= kv * HEAD64) & (lane < (kv + 1) * HEAD64), pair, 0.0).astype(BF16)
        st = lax.dot_general(kmat, qh, (((1,), (1,)), ((), ())), preferred_element_type=F32)
        m = jnp.max(st, axis=0, keepdims=True)
        pt = jnp.exp2(st - m).astype(BF16)
        res = jnp.dot(vts[kv], pt, preferred_element_type=F32)
        outs.append(res[0:HEAD64] / res[HEAD64:HEAD64 + 1])
    o_ref[0] = jnp.concatenate(outs, axis=0).T


def _attention(q, k_all, vt_all, tq):
    bsz, t, c = q.shape
    tk = k_all.shape[1]
    n_kv = k_all.shape[-1] // HEAD64
    assert n_kv * HEAD64 == LANES
    return pl.pallas_call(
        functools.partial(_attn_kernel, n_kv),
        out_shape=jax.ShapeDtypeStruct((bsz, t, c), F32),
        grid=(bsz, t // tq),
        in_specs=[pl.BlockSpec((1, tq, c), lambda b, i: (b, i, 0)),
                  pl.BlockSpec((1, tk, LANES), lambda b, i: (b, 0, 0), pipeline_mode=pl.Buffered(1)),
                  pl.BlockSpec((1, vt_all.shape[1], tk), lambda b, i: (b, 0, 0), pipeline_mode=pl.Buffered(1))],
        out_specs=pl.BlockSpec((1, tq, c), lambda b, i: (b, i, 0)),
        compiler_params=pltpu.CompilerParams(dimension_semantics=("parallel", "parallel"),
                                             vmem_limit_bytes=VMEM_LIMIT),
        name="attention",
    )(q, k_all, vt_all)


def _postnorm_residual(x, out, g_post, gate):
    ms = jnp.mean(out * out, axis=-1, keepdims=True)
    return x + gate * (out * lax.rsqrt(ms + EPS) * g_post)


def _out0_kernel(x_ref, gate_ref, of_ref, ob_ref, r_ref, k_ref, v_ref, rg_ref, oat_ref, ag_ref,
                 p_ref, bd_ref, wo_ref, y_ref):
    o = of_ref[0] + ob_ref[0]
    c = o.shape[-1]
    inv = 1.0 / HEAD64
    mu = _seg_sum(o, bd_ref) * inv
    dlt = o - mu
    var = _seg_sum(dlt * dlt, bd_ref) * inv
    gn = dlt * lax.rsqrt(var + GN_EPS) * _prow(p_ref, P0_GNG, c) + _prow(p_ref, P0_GNB, c)
    bonus = _seg_sum(r_ref[0] * k_ref[0] * _prow(p_ref, P0_RK, c), bd_ref) * v_ref[0]
    o_rw = (gn + bonus) * _silu(rg_ref[0])
    o_at = oat_ref[0] * _silu(ag_ref[0])
    out = _mm(o_rw, wo_ref[0:c, :]) + _mm(o_at, wo_ref[c:2 * c, :])
    y_ref[0] = _postnorm_residual(x_ref[0], out, _prow(p_ref, P0_G_POST, x_ref.shape[-1]), gate_ref[0])


def _out0(x, gate, o_f, o_b, r, k, v, rg, o_at, ag, w, tm):
    bsz, t, d = x.shape
    c = 512
    tokd = pl.BlockSpec((1, tm, d), lambda b, i: (b, i, 0))
    tok = pl.BlockSpec((1, tm, c), lambda b, i: (b, i, 0))
    names = ("p", "bd", "wout")
    return pl.pallas_call(
        _out0_kernel,
        out_shape=jax.ShapeDtypeStruct((bsz, t, d), F32),
        grid=(bsz, t // tm),
        in_specs=[tokd, _mod_spec(gate, d)] + [tok] * 8
                 + [_const_spec(w[n].shape) for n in names],
        out_specs=tokd,
        compiler_params=pltpu.CompilerParams(dimension_semantics=("parallel", "parallel"),
                                             vmem_limit_bytes=VMEM_LIMIT),
        name="out_proj0",
    )(x, gate, o_f, o_b, r, k, v, rg, o_at, ag, *[w[n] for n in names])


(P1_G_PRE, P1_G_POST, P1_GB, P1_ALOG_F, P1_ALOG_B, P1_DTB_F, P1_DTB_B, P1_GLA_G, P1_GDN_G, P1_CONV) = range(10)
W1_COLS = (0, 256, 512, 1024, 1152, 1664, 3200, 3328, 3456, 3968)


def _in1_kernel(x_ref, sc_ref, sh_ref, p_ref, w_ref, g2_ref,
                q_ref, k_ref, v_ref, lw_ref, gg_ref, dqkv_ref, small_ref, dg_ref):
    dm = x_ref.shape[-1]
    h = _prenorm(x_ref[0], _prow(p_ref, P1_G_PRE, dm), sc_ref[0], sh_ref[0]).astype(BF16)
    c = W1_COLS
    proj = lambda j: jnp.dot(h, w_ref[:, c[j]:c[j + 1]], preferred_element_type=F32)
    q_ref[0] = proj(0) * (HEAD64 ** -0.5)
    k_ref[0] = proj(1)
    v_ref[0] = proj(2)
    gg_ref[0] = proj(4)
    dqkv_ref[0] = proj(5)
    small_ref[0, 0] = proj(6)
    small_ref[1, 0] = proj(7)
    dg_ref[0] = proj(8)
    lw = _log_sigmoid(_mm(proj(3), g2_ref[...]) + _prow(p_ref, P1_GB, g2_ref.shape[-1])) * (1.0 / GLA_GATE_NORM)
    half = lw.shape[-1] // 2
    lw_ref[0, 0] = lw[:, :half]
    lw_ref[1, 0] = lw[:, half:]


def _in1(x, scale1p, shift, w, tm):
    bsz, t, d = x.shape
    widths = [256, 256, 512, -256, 512, 1536, -LANES, 512]
    tok = lambda n: pl.BlockSpec((1, tm, n), lambda b, i: (b, i, 0))
    outs, ospecs = [], []
    for n in widths:
        if n < 0:
            outs.append(jax.ShapeDtypeStruct((2, bsz, t, -n), F32))
            ospecs.append(pl.BlockSpec((2, 1, tm, -n), lambda b, i: (0, b, i, 0)))
        else:
            outs.append(jax.ShapeDtypeStruct((bsz, t, n), F32))
            ospecs.append(tok(n))
    return pl.pallas_call(
        _in1_kernel,
        out_shape=outs,
        grid=(bsz, t // tm),
        in_specs=[tok(d), _mod_spec(scale1p, d), _mod_spec(shift, d), _const_spec(w["p"].shape),
                  _const_spec(w["win"].shape, single=True), _const_spec(w["g2"].shape)],
        out_specs=ospecs,
        compiler_params=pltpu.CompilerParams(dimension_semantics=("parallel", "parallel"),
                                             vmem_limit_bytes=VMEM_LIMIT),
        name="in_proj1",
    )(x, scale1p, shift, w["p"], w["win"], w["g2"])


def _gdnprep_kernel(n_blocks, x_ref, xp_ref, xn_ref, s_ref, p_ref, q_ref, k_ref, v_ref, g_ref):
    i = pl.program_id(1)
    x = x_ref[0]
    prev, nxt = _neighbours(x, xp_ref[0], xn_ref[0], i, n_blocks)
    cw = lambda j: _prow(p_ref, P1_CONV + j, x.shape[-1])
    y = _silu(prev * cw(0) + x * cw(1) + nxt * cw(2))
    c = 512
    qs, ks = [], []
    for h in range(c // HEAD128):
        qh = y[:, h * HEAD128:(h + 1) * HEAD128]
        kh = y[:, c + h * HEAD128:c + (h + 1) * HEAD128]
        qs.append(qh * lax.rsqrt(jnp.sum(qh * qh, axis=-1, keepdims=True) + EPS) * (HEAD128 ** -0.5))
        ks.append(kh * lax.rsqrt(jnp.sum(kh * kh, axis=-1, keepdims=True) + EPS))
    q_ref[0] = jnp.concatenate(qs, axis=-1)
    k_ref[0] = jnp.concatenate(ks, axis=-1)
    v_ref[0] = y[:, 2 * c:]
    lane = lax.broadcasted_iota(jnp.int32, (1, LANES), 1)
    nh = c // HEAD128
    for d, (ia, ib) in enumerate(((P1_ALOG_F, P1_DTB_F), (P1_ALOG_B, P1_DTB_B))):
        s = s_ref[d, 0]
        loga = -jnp.exp(_prow(p_ref, ia, LANES)) * _softplus(s + _prow(p_ref, ib, LANES))
        g_ref[d, 0] = jnp.where(lane < nh, loga, jnp.where(lane < 2 * nh, jax.nn.sigmoid(s), 0.0))


def _gdnprep(dqkv, small, w, tm):
    bsz, t, cs = dqkv.shape
    c = 512
    nb = t // tm
    tok = pl.BlockSpec((1, tm, c), lambda b, i: (b, i, 0))
    tokl = pl.BlockSpec((2, 1, tm, LANES), lambda b, i: (0, b, i, 0))
    return pl.pallas_call(
        functools.partial(_gdnprep_kernel, nb),
        out_shape=[jax.ShapeDtypeStruct((bsz, t, c), F32)] * 3 + [jax.ShapeDtypeStruct((2, bsz, t, LANES), F32)],
        grid=(bsz, nb),
        in_specs=_halo_specs(tm, cs, t) + [tokl, _const_spec(w["p"].shape)],
        out_specs=[tok] * 3 + [tokl],
        compiler_params=pltpu.CompilerParams(dimension_semantics=("parallel", "parallel"),
                                             vmem_limit_bytes=VMEM_LIMIT),
        name="gdn_prep",
    )(dqkv, dqkv, dqkv, small, w["p"])


def _glascan_kernel(nc, nsub, qf, kf, vf, lwf, qb, kb, vb, lwb, s0_ref, of_ref, ob_ref, sf_ref, st_ref):
    c = pl.program_id(1)
    L = CHUNK
    nh = qf.shape[-1] // HEAD64

    @pl.when(c == 0)
    def _():
        st_ref[...] = s0_ref[0]

    hmask = _lane_group_masks(nh * HEAD64, HEAD64, nh)
    rowv = lax.broadcasted_iota(jnp.int32, st_ref.shape[1:], 0)
    colk = lax.broadcasted_iota(jnp.int32, st_ref.shape[1:], 1)
    bd = (rowv >> 7) == (colk >> 6)
    refs = ((qf, kf, vf, lwf, of_ref), (qb, kb, vb, lwb, ob_ref))
    rows = lambda d, s: slice(s * L, (s + 1) * L) if d == 0 else slice((nsub - 1 - s) * L, (nsub - s) * L)
    masks = []
    for d in range(2):
        _, incl = _order_masks(L, d == 1)
        _, incl4 = _order_masks(L, d == 1, reps=nh)
        masks.append((jnp.where(incl, 1.0, 0.0).astype(BF16), incl4))

    units = [(s, d) for s in range(nsub) for d in range(2)]
    dat = []
    for s, d in units:
        q_, k_, v_, lw_, _ = refs[d]
        rw = rows(d, s)
        dat.append(dict(q=q_[0, rw, :], k=k_[0, rw, :], v=v_[0, rw, :], lw=lw_[0, 0, rw, :]))
    cws = [_mm_exact_l(masks[d][0], x["lw"]) for (_, d), x in zip(units, dat)]
    for x, cw in zip(dat, cws):
        c0 = cw[L // 2:L // 2 + 1, :]
        tot = jnp.sum(x["lw"], axis=0, keepdims=True)
        q_rel = x["q"] * jnp.exp(cw - c0)
        k_rel = x["k"] * jnp.exp(c0 - cw)
        x.update(sq=_stack_heads(q_rel, hmask), k_rel=k_rel, q_abs=q_rel * jnp.exp(c0),
                 k_fin=k_rel * jnp.exp(tot - c0), w_tot=jnp.exp(tot))
    for (_, d), x in zip(units, dat):
        x["att"] = jnp.where(masks[d][1], _mm_nt(x["sq"], x["k_rel"]), 0.0)
        x["vk"] = jnp.where(bd, _mm_tn(x["v"], x["k_fin"]), 0.0)
    for x in dat:
        x["intra"] = jnp.concatenate(
            [_mm(x["att"][h * L:(h + 1) * L], x["v"][:, h * HEAD128:(h + 1) * HEAD128]) for h in range(nh)], axis=-1)

    state = [st_ref[0], st_ref[1]]
    for s in range(nsub):
        for d in range(2):
            x = dat[s * 2 + d]
            refs[d][4][0, rows(d, s), :] = x["intra"] + _mm_nt(x["q_abs"], state[d])
            state[d] = state[d] * x["w_tot"] + x["vk"]
    st_ref[0] = state[0]
    st_ref[1] = state[1]

    @pl.when(c == nc - 1)
    def _():
        sf_ref[0] = st_ref[...]


def _glascan(q, k, v, lw, s0):
    bsz, t, ck = q.shape
    cv = v.shape[-1]
    blk = GLA_SUB * CHUNK
    nc = t // blk
    kf, kb = _dir_specs((blk, ck), nc)
    vf, vb = _dir_specs((blk, cv), nc)
    lf, lb = _dir_specs((blk, ck), nc, lead=True)
    sspec = pl.BlockSpec((1, 2, cv, ck), lambda b_, c: (b_, 0, 0, 0))
    return pl.pallas_call(
        functools.partial(_glascan_kernel, nc, GLA_SUB),
        out_shape=[jax.ShapeDtypeStruct((bsz, t, cv), F32)] * 2 + [jax.ShapeDtypeStruct((bsz, 2, cv, ck), F32)],
        grid=(bsz, nc),
        in_specs=[kf, kf, vf, lf, kb, kb, vb, lb, sspec],
        out_specs=[vf, vb, sspec],
        scratch_shapes=[pltpu.VMEM((2, cv, ck), F32)],
        compiler_params=pltpu.CompilerParams(dimension_semantics=("parallel", "arbitrary"),
                                             vmem_limit_bytes=VMEM_LIMIT),
        name="gla_scan",
    )(q, k, v, lw, q, k, v, lw, s0)


def _gdnscan_kernel(nc, nsub, qf, kf, vf, gcf, qb, kb, vb, gcb, s0_ref, of_ref, ob_ref, sf_ref, st_ref):
    c = pl.program_id(1)
    L = CHUNK
    nh = qf.shape[-1] // HEAD128

    @pl.when(c == 0)
    def _():
        st_ref[...] = s0_ref[0]

    refs = ((qf, kf, vf, gcf), (qb, kb, vb, gcb))
    rows = lambda d, s: slice(s * L, (s + 1) * L) if d == 0 else slice((nsub - 1 - s) * L, (nsub - s) * L)
    per_dir = {}
    for d in range(2):
        strict, incl = _order_masks(L, d == 1)
        _, incl_t = _order_masks(L, d == 0)
        incl_bf = jnp.where(incl, 1.0, 0.0).astype(BF16)
        incl_t_bf = jnp.where(incl_t, 1.0, 0.0).astype(BF16)
        for s in range(nsub):
            gcol = refs[d][3][0, 0, rows(d, s), :]
            per_dir[d, s] = dict(strict=strict, incl=incl, gcol=gcol,
                                 cum_c=_mm_exact_l(incl_bf, gcol),
                                 cum_r=_mm_exact_tn(gcol, incl_t_bf),
                                 tot=jnp.sum(gcol, axis=0, keepdims=True))

    units = [(s, d, h) for s in range(nsub) for d in range(2) for h in range(nh)]
    dat = []
    for s, d, h in units:
        hs = slice(h * HEAD128, (h + 1) * HEAD128)
        pd = per_dir[d, s]
        rw = rows(d, s)
        q, k, v = refs[d][0][0, rw, hs], refs[d][1][0, rw, hs], refs[d][2][0, rw, hs]
        g = pd["cum_c"][:, h:h + 1]
        beta = pd["gcol"][:, nh + h:nh + h + 1]
        g_last = pd["tot"][:, h:h + 1]
        gam = jnp.exp(g)
        dat.append(dict(q=q, k=k, qg=q * gam, beta=beta, pd=pd,
                        dec=jnp.exp(jnp.minimum(g - pd["cum_r"][h:h + 1, :], 0.0)),
                        rhs=jnp.concatenate([(beta * gam) * k, beta * v], axis=-1),
                        k_dec=k * jnp.exp(g_last - g), gl=jnp.exp(g_last)))
    for x in dat:
        x["kk"] = _mm_nt(x["k"], x["k"])
        x["aqk"] = jnp.where(x["pd"]["incl"], x["dec"] * _mm_nt(x["q"], x["k"]), 0.0)
    tinv = _tri_inv_many([jnp.where(x["pd"]["strict"], x["dec"] * x["kk"] * x["beta"], 0.0) for x in dat], _mm)
    for x, t in zip(dat, tinv):
        x["wu"] = _mm(t, x["rhs"])
        x["wq"] = jnp.concatenate([x["wu"][:, :HEAD128], x["qg"]], axis=0)

    state = {(d, h): st_ref[d, h] for d in range(2) for h in range(nh)}
    for s in range(nsub):
        cur = [(d, h, dat[(s * 2 + d) * nh + h]) for d in range(2) for h in range(nh)]
        for d, h, x in cur:
            x["ws"] = _mm(x["wq"], state[d, h])
        for d, h, x in cur:
            x["u"] = x["wu"][:, HEAD128:] - x["ws"][0:L]
        for d, h, x in cur:
            x["o"] = x["ws"][L:2 * L] + _mm(x["aqk"], x["u"])
            state[d, h] = state[d, h] * x["gl"] + _mm_tn(x["k_dec"], x["u"])
        for d, o_ref in ((0, of_ref), (1, ob_ref)):
            o_ref[0, rows(d, s), :] = jnp.concatenate([x["o"] for dd, _, x in cur if dd == d], axis=-1)
    for (d, h), val in state.items():
        st_ref[d, h] = val

    @pl.when(c == nc - 1)
    def _():
        sf_ref[0] = st_ref[...]


def _gdnscan(q, k, v, gcol, s0):
    bsz, t, cdim = q.shape
    nh = cdim // HEAD128
    blk = GDN_SUB * CHUNK
    nc = t // blk
    tf, tb = _dir_specs((blk, cdim), nc)
    gcf, gcb = _dir_specs((blk, LANES), nc, lead=True)
    sspec = pl.BlockSpec((1, 2, nh, HEAD128, HEAD128), lambda b_, c: (b_, 0, 0, 0, 0))
    return pl.pallas_call(
        functools.partial(_gdnscan_kernel, nc, GDN_SUB),
        out_shape=[jax.ShapeDtypeStruct((bsz, t, cdim), F32)] * 2
                  + [jax.ShapeDtypeStruct((bsz, 2, nh, HEAD128, HEAD128), F32)],
        grid=(bsz, nc),
        in_specs=[tf, tf, tf, gcf, tb, tb, tb, gcb, sspec],
        out_specs=[tf, tb, sspec],
        scratch_shapes=[pltpu.VMEM((2, nh, HEAD128, HEAD128), F32)],
        compiler_params=pltpu.CompilerParams(dimension_semantics=("parallel", "arbitrary"),
                                             vmem_limit_bytes=VMEM_LIMIT),
        name="gdn_scan",
    )(q, k, v, gcol, q, k, v, gcol, s0)


def _head_rmsnorm(o, g):
    parts = []
    for h in range(o.shape[-1] // HEAD128):
        oh = o[:, h * HEAD128:(h + 1) * HEAD128]
        parts.append(oh * lax.rsqrt(jnp.mean(oh * oh, axis=-1, keepdims=True) + EPS) * g)
    return jnp.concatenate(parts, axis=-1)


def _out1_kernel(x_ref, gate_ref, glf_ref, glb_ref, gdf_ref, gdb_ref, gg_ref, dg_ref, p_ref, wo_ref, y_ref):
    c = gg_ref.shape[-1]
    o_gla = _head_rmsnorm(glf_ref[0] + glb_ref[0], _prow(p_ref, P1_GLA_G, HEAD128)) * _silu(gg_ref[0])
    o_gdn = _head_rmsnorm(gdf_ref[0] + gdb_ref[0], _prow(p_ref, P1_GDN_G, HEAD128)) * _silu(dg_ref[0])
    out = _mm(o_gla, wo_ref[0:c, :]) + _mm(o_gdn, wo_ref[c:2 * c, :])
    y_ref[0] = _postnorm_residual(x_ref[0], out, _prow(p_ref, P1_G_POST, x_ref.shape[-1]), gate_ref[0])


def _out1(x, gate, gla_f, gla_b, gdn_f, gdn_b, gg, dg, w, tm):
    bsz, t, d = x.shape
    c = 512
    tokd = pl.BlockSpec((1, tm, d), lambda b, i: (b, i, 0))
    tok = pl.BlockSpec((1, tm, c), lambda b, i: (b, i, 0))
    names = ("p", "wout")
    return pl.pallas_call(
        _out1_kernel,
        out_shape=jax.ShapeDtypeStruct((bsz, t, d), F32),
        grid=(bsz, t // tm),
        in_specs=[tokd, _mod_spec(gate, d)] + [tok] * 6
                 + [_const_spec(w[n].shape) for n in names],
        out_specs=tokd,
        compiler_params=pltpu.CompilerParams(dimension_semantics=("parallel", "parallel"),
                                             vmem_limit_bytes=VMEM_LIMIT),
        name="out_proj1",
    )(x, gate, gla_f, gla_b, gdn_f, gdn_b, gg, dg, *[w[n] for n in names])


def _block_ones(n_groups, width):
    return jnp.asarray(np.kron(np.eye(n_groups, dtype=np.float32), np.ones((width, width), np.float32)), BF16)


def _rope_tables(t):
    rows = t // GRID_W
    row_id = np.repeat(np.arange(rows, dtype=np.float32), GRID_W)
    col_id = np.tile(np.arange(GRID_W, dtype=np.float32), rows)
    nf = HEAD64 // 4
    inv = np.float32(ROPE_THETA) ** (-np.arange(nf, dtype=np.float32) / np.float32(nf))
    ang_r = (row_id[:, None] * inv[None, :]).astype(np.float32)
    ang_c = (col_id[:, None] * inv[None, :]).astype(np.float32)
    cos = np.concatenate([np.cos(ang_r)] * 2 + [np.cos(ang_c)] * 2, axis=-1)
    sin = np.concatenate([-np.sin(ang_r), np.sin(ang_r), -np.sin(ang_c), np.sin(ang_c)], axis=-1)
    tile2 = lambda a: jnp.asarray(np.concatenate([a, a], axis=-1).astype(np.float32))
    return tile2(cos), tile2(sin)


def _pack_rows(rows, width):
    return jnp.stack([jnp.pad(r.reshape(-1).astype(F32), (0, width - r.size)) for r in rows])


def _layer0_weights(p):
    w_in = p["w_in"]
    rw_w = 512
    slab_w = 3 * rw_w + 192
    d = w_in.shape[0]
    w = {}
    w["win"] = jnp.concatenate([w_in[:, :slab_w], jnp.zeros((d, W0_COLS[1] - slab_w), F32), w_in[:, slab_w:]],
                               axis=1).astype(BF16)
    w["wout"] = p["w_out"].astype(BF16)
    w["bdq"] = _block_ones(8, HEAD64)
    w["bdk"] = _block_ones(2, HEAD64)
    w["bd"] = w["bdq"]
    w["p"] = _pack_rows([p["g_pre"], p["g_post"], p["rw_mu"], jnp.concatenate([p["rw_w0_f"], p["rw_w0_b"]]),
                         p["rw_a0"], p["rw_k_k"], p["rw_k_a"], p["rw_r_k"], p["rw_gn_g"], p["rw_gn_b"],
                         jnp.tile(p["at_gq"], 8), jnp.tile(p["at_gk"], 2)], W0_COLS[1])
    z = jnp.zeros((64, rw_w), F32)
    w["w2"] = jnp.concatenate([jnp.concatenate([p["rw_w2_f"], z], axis=1),
                               jnp.concatenate([z, p["rw_w2_b"]], axis=1)], axis=0).astype(BF16)
    w["a2"] = jnp.concatenate([p["rw_a2"], z], axis=0).astype(BF16)
    return w


def _layer1_weights(p):
    w_in = p["w_in"]
    d = w_in.shape[0]
    zc = lambda n: jnp.zeros((d, n), F32)
    af, ab, be = w_in[:, 3104:3108], w_in[:, 3108:3112], w_in[:, 3112:3116]
    w = {}
    w["win"] = jnp.concatenate([w_in[:, :1024], w_in[:, 1024:1056], zc(LANES - 32), w_in[:, 1056:3104],
                                af, be, zc(LANES - 8), ab, be, zc(LANES - 8), w_in[:, 3116:]],
                               axis=1).astype(BF16)
    w["wout"] = p["w_out"].astype(BF16)
    z = jnp.zeros((16, 256), F32)
    g2 = jnp.concatenate([jnp.concatenate([p["gla_g2_f"], z], axis=1),
                          jnp.concatenate([z, p["gla_g2_b"]], axis=1)], axis=0)
    w["g2"] = jnp.pad(g2, ((0, LANES - 32), (0, 0))).astype(BF16)
    conv = p["gdn_conv"]
    w["p"] = _pack_rows([p["g_pre"], p["g_post"], jnp.concatenate([p["gla_gb_f"], p["gla_gb_b"]]),
                         p["gdn_A_log_f"], p["gdn_A_log_b"], p["gdn_dt_bias_f"], p["gdn_dt_bias_b"],
                         p["gla_norm_g"], p["gdn_norm_g"], conv[0], conv[1], conv[2]], conv.shape[1])
    return w


def _rw_state_to_bd(s):
    bsz = s.shape[0]
    s = s.reshape(bsz, 2, 4, HEAD64, HEAD64)
    eye = jnp.eye(4, dtype=s.dtype)
    return jnp.einsum("bqhvk,hg->bqhvgk", s, eye).reshape(bsz, 2, QUAD, QUAD)


def _rw_state_from_bd(s):
    bsz = s.shape[0]
    s = s.reshape(bsz, 2, 4, HEAD64, 4, HEAD64)
    return jnp.stack([s[:, :, h, :, h, :] for h in range(4)], axis=2).reshape(bsz, 8, HEAD64, HEAD64)


def _gla_state_to_bd(s):
    bsz = s.shape[0]
    eye = jnp.eye(4, dtype=s.dtype)
    return jnp.einsum("bhkv,hg->bhvgk", s, eye).reshape(bsz, 4 * HEAD128, 4 * HEAD64)


def _gla_state_from_bd(s):
    bsz = s.shape[0]
    s = s.reshape(bsz, 4, HEAD128, 4, HEAD64)
    return jnp.stack([jnp.swapaxes(s[:, h, :, h, :], -1, -2) for h in range(4)], axis=1)


def _trunk(x, mods, w0, w1, ctx, tm, tq):
    bsz, t, dm = x.shape
    latent = ctx is not None
    ts = min(tm, t)
    shared = mods[0][0].shape[0] == 1
    flat = (lambda a: a.reshape(a.shape[:-3] + (1, bsz * t, a.shape[-1]))) if shared else (lambda a: a)
    unflat = lambda a: a.reshape(a.shape[:-3] + (bsz, t, a.shape[-1]))

    sc, sh, gt = mods[0]
    slab, rg, q, k, v, ag = [unflat(a) for a in _in0(flat(x), sc, sh, w0, latent, tm)]
    r, kmod, vr, kkn, bvec, lw = _rwprep(slab, w0, ts)
    if latent:
        s0f, s0b, k_ctx, v_ctx = ctx[:4]
        s0 = jnp.stack([_rw_state_to_bd(s0f), _rw_state_to_bd(s0b)], axis=1)
        k_all = jnp.concatenate([k_ctx.reshape(bsz, -1, 2 * HEAD64), k], axis=1)
        v_all = jnp.concatenate([v_ctx.reshape(bsz, -1, 2 * HEAD64), v], axis=1)
    else:
        s0 = jnp.zeros((bsz, 2, 2, QUAD, QUAD), F32)
        k_all, v_all = k, v
    o_rwf, o_rwb, s_rw = _rwscan(r, kmod, vr, kkn, bvec, lw, s0)
    vt_all = jnp.concatenate([jnp.swapaxes(v_all, 1, 2), jnp.ones((bsz, 16, v_all.shape[1]), F32)], axis=1)
    o_at = _attention(q, k_all.astype(BF16), vt_all.astype(BF16), tq)
    x1 = _out0(flat(x), gt, *[flat(a) for a in (o_rwf, o_rwb, r, kmod, vr, rg, o_at, ag)], w0, tm)

    sc, sh, gt = mods[1]
    gq, gk, gv, glw, gg, dqkv, small, dg = [unflat(a) for a in _in1(x1, sc, sh, w1, tm)]
    dq, dk, dv, gcol = _gdnprep(dqkv, small, w1, ts)
    if latent:
        sgf, sgb, sdf, sdb = ctx[4:]
        s0_gla = jnp.stack([_gla_state_to_bd(sgf), _gla_state_to_bd(sgb)], axis=1)
        s0_gdn = jnp.stack([sdf, sdb], axis=1)
    else:
        s0_gla = jnp.zeros((bsz, 2, 4 * HEAD128, 4 * HEAD64), F32)
        s0_gdn = jnp.zeros((bsz, 2, 4, HEAD128, HEAD128), F32)
    gla_f, gla_b, s_gla = _glascan(gq, gk, gv, glw, s0_gla)
    gdn_f, gdn_b, s_gdn = _gdnscan(dq, dk, dv, gcol, s0_gdn)
    y = unflat(_out1(x1, gt, *[flat(a) for a in (gla_f, gla_b, gdn_f, gdn_b, gg, dg)], w1, tm))

    new = None
    if not latent:
        new = (_rw_state_from_bd(s_rw[:, 0]), _rw_state_from_bd(s_rw[:, 1]),
               k.reshape(bsz, t, 2, HEAD64), v.reshape(bsz, t, 2, HEAD64),
               _gla_state_from_bd(s_gla[:, 0]), _gla_state_from_bd(s_gla[:, 1]), s_gdn[:, 0], s_gdn[:, 1])
    return y, new


def _split_mod(m, d):
    shift, scale, gate = m[..., :d], m[..., d:2 * d], m[..., 2 * d:]
    return (1.0 + scale)[:, None, :], shift[:, None, :], gate[:, None, :]


def kernel(x_prompt, x_sample, state_l0_rwkv_fwd, state_l0_rwkv_bwd, cache_l0_k, cache_l0_v, state_l1_gla_fwd, state_l1_gla_bwd, state_l1_gdn_fwd, state_l1_gdn_bwd, c, c_ctx, l0_mod_w, l0_mod_b, l0_g_pre, l0_g_post, l0_w_in, l0_w_out, l0_rw_mu, l0_rw_w0_f, l0_rw_w2_f, l0_rw_w0_b, l0_rw_w2_b, l0_rw_a0, l0_rw_a2, l0_rw_k_k, l0_rw_k_a, l0_rw_r_k, l0_rw_gn_g, l0_rw_gn_b, l0_at_gq, l0_at_gk, l1_mod_w, l1_mod_b, l1_g_pre, l1_g_post, l1_w_in, l1_w_out, l1_gla_g2_f, l1_gla_gb_f, l1_gla_g2_b, l1_gla_gb_b, l1_gla_norm_g, l1_gdn_conv, l1_gdn_A_log_f, l1_gdn_dt_bias_f, l1_gdn_A_log_b, l1_gdn_dt_bias_b, l1_gdn_norm_g):
    p0 = {"g_pre": l0_g_pre, "g_post": l0_g_post, "w_in": l0_w_in, "w_out": l0_w_out, "rw_mu": l0_rw_mu,
          "rw_w0_f": l0_rw_w0_f, "rw_w2_f": l0_rw_w2_f, "rw_w0_b": l0_rw_w0_b, "rw_w2_b": l0_rw_w2_b,
          "rw_a0": l0_rw_a0, "rw_a2": l0_rw_a2, "rw_k_k": l0_rw_k_k, "rw_k_a": l0_rw_k_a,
          "rw_r_k": l0_rw_r_k, "rw_gn_g": l0_rw_gn_g, "rw_gn_b": l0_rw_gn_b,
          "at_gq": l0_at_gq, "at_gk": l0_at_gk}
    p1 = {"g_pre": l1_g_pre, "g_post": l1_g_post, "w_in": l1_w_in, "w_out": l1_w_out,
          "gla_g2_f": l1_gla_g2_f, "gla_gb_f": l1_gla_gb_f, "gla_g2_b": l1_gla_g2_b,
          "gla_gb_b": l1_gla_gb_b, "gla_norm_g": l1_gla_norm_g, "gdn_conv": l1_gdn_conv,
          "gdn_A_log_f": l1_gdn_A_log_f, "gdn_dt_bias_f": l1_gdn_dt_bias_f,
          "gdn_A_log_b": l1_gdn_A_log_b, "gdn_dt_bias_b": l1_gdn_dt_bias_b,
          "gdn_norm_g": l1_gdn_norm_g}
    d = x_prompt.shape[-1]
    nb = c.shape[0]
    w0 = _layer0_weights(p0)
    w1 = _layer1_weights(p1)
    cos, sin = _rope_tables(x_sample.shape[1])
    w0["cos"], w0["sin"] = cos, sin

    cvec = jnp.concatenate([c, c_ctx[None, :], jnp.zeros((SUBLANES - nb - 1, d), F32)], axis=0)
    m0 = _modulation(cvec, l0_mod_w, l0_mod_b)
    m1 = _modulation(cvec, l1_mod_w, l1_mod_b)
    mods_lat = [_split_mod(m[:nb], d) for m in (m0, m1)]
    mods_ctx = [_split_mod(m[nb:nb + 1], d) for m in (m0, m1)]

    tm = 512
    y_prompt, new = _trunk(x_prompt, mods_ctx, w0, w1, None, tm, min(512, x_prompt.shape[1]))
    ctx = (state_l0_rwkv_fwd, state_l0_rwkv_bwd, cache_l0_k, cache_l0_v,
           state_l1_gla_fwd, state_l1_gla_bwd, state_l1_gdn_fwd, state_l1_gdn_bwd)
    y_sample, _ = _trunk(x_sample, mods_lat, w0, w1, ctx, tm, min(512, x_sample.shape[1]))
    return (y_prompt, y_sample) + tuple(new)
```

```python
import functools

import numpy as np
import jax
import jax.numpy as jnp
from jax import lax
from jax.experimental import pallas as pl
from jax.experimental.pallas import tpu as pltpu

F32 = jnp.float32
BF16 = jnp.bfloat16

EPS = 1e-6
GN_EPS = 64e-5
CHUNK = 64
GRID_W = 64
ROPE_THETA = 10000.0
RW_DECAY_SCALE = 0.6065306597126334
GLA_GATE_NORM = 16.0
LOG2E = 1.4426950408889634
HEAD64 = 64
HEAD128 = 128
LANES = 128
SUBLANES = 8
VMEM_LIMIT = 56 * 1024 * 1024


def _mm(a, b):
    return jnp.dot(a.astype(BF16), b.astype(BF16), preferred_element_type=F32)


def _mm_nt(a, b):
    return lax.dot_general(a.astype(BF16), b.astype(BF16), (((1,), (1,)), ((), ())),
                           preferred_element_type=F32)


def _mm_tn(a, b):
    return lax.dot_general(a.astype(BF16), b.astype(BF16), (((0,), (0,)), ((), ())),
                           preferred_element_type=F32)


def _split3(x):
    hi = x.astype(BF16)
    r1 = x - hi.astype(F32)
    mid = r1.astype(BF16)
    lo = (r1 - mid.astype(F32)).astype(BF16)
    return hi, mid, lo


def _mm_exact_l(mask_bf16, x):
    hi, mid, lo = _split3(x)
    d = functools.partial(jnp.dot, preferred_element_type=F32)
    return d(mask_bf16, hi) + d(mask_bf16, mid) + d(mask_bf16, lo)


def _mm_exact_tn(x, mask_bf16):
    hi, mid, lo = _split3(x)
    d = lambda a: lax.dot_general(a, mask_bf16, (((0,), (0,)), ((), ())), preferred_element_type=F32)
    return d(hi) + d(mid) + d(lo)


def _mm3(a, b):
    ah = a.astype(BF16)
    al = (a - ah.astype(F32)).astype(BF16)
    bh = b.astype(BF16)
    bl = (b - bh.astype(F32)).astype(BF16)
    d = functools.partial(jnp.dot, preferred_element_type=F32)
    return d(ah, bh) + d(ah, bl) + d(al, bh)


def _silu(x):
    return x * jax.nn.sigmoid(x)


def _softplus(x):
    return jnp.maximum(x, 0.0) + jnp.log(1.0 + jnp.exp(-jnp.abs(x)))


def _log_sigmoid(x):
    return jnp.minimum(x, 0.0) - jnp.log(1.0 + jnp.exp(-jnp.abs(x)))


def _order_masks(n, reverse, reps=1):
    row = lax.broadcasted_iota(jnp.int32, (reps * n, n), 0) & (n - 1)
    col = lax.broadcasted_iota(jnp.int32, (reps * n, n), 1)
    d = (col - row) if reverse else (row - col)
    return d > 0, d >= 0


def _tri_inv_many(nmats, mm):
    n = nmats[0].shape[0]
    row = lax.broadcasted_iota(jnp.int32, (n, n), 0)
    col = lax.broadcasted_iota(jnp.int32, (n, n), 1)
    x = row ^ col
    eye = jnp.where(row == col, 1.0, 0.0).astype(F32)
    ts = [eye - jnp.where(x == 1, nm, 0.0) for nm in nmats]
    s = 1
    while (2 << s) <= n:
        lvl = (x >> s) == 1
        tcs = [mm(t, jnp.where(lvl, nm, 0.0)) for t, nm in zip(ts, nmats)]
        ts = [t - mm(tc, t) for t, tc in zip(ts, tcs)]
        s += 1
    return ts


def _lane_group_masks(width, group, count):
    lane = lax.broadcasted_iota(jnp.int32, (1, width), 1)
    return [((lane >= g * group) & (lane < (g + 1) * group)) for g in range(count)]


def _stack_heads(x, masks):
    return jnp.concatenate([jnp.where(m, x, 0.0) for m in masks], axis=0)


def _mod_kernel(c_ref, w_ref, b_ref, o_ref):
    s = _silu(c_ref[...])
    o_ref[...] = _mm3(s, w_ref[...]) + b_ref[...]


def _modulation(cvec, mod_w, mod_b):
    rows, d = cvec.shape
    n = mod_w.shape[1]
    tn = 1024
    return pl.pallas_call(
        _mod_kernel,
        out_shape=jax.ShapeDtypeStruct((rows, n), F32),
        grid=(n // tn,),
        in_specs=[pl.BlockSpec((rows, d), lambda j: (0, 0)),
                  pl.BlockSpec((d, tn), lambda j: (0, j)),
                  pl.BlockSpec((1, tn), lambda j: (0, j))],
        out_specs=pl.BlockSpec((rows, tn), lambda j: (0, j)),
        compiler_params=pltpu.CompilerParams(vmem_limit_bytes=VMEM_LIMIT),
        name="modulation",
    )(cvec, mod_w, mod_b.reshape(1, n))


def _prenorm(x, g, scale1p, shift):
    ms = jnp.mean(x * x, axis=-1, keepdims=True)
    return x * lax.rsqrt(ms + EPS) * g * scale1p + shift


def _seg_sum(x, bd_ref):
    hi = x.astype(BF16)
    lo = (x - hi.astype(F32)).astype(BF16)
    bd = bd_ref[...]
    return jnp.dot(hi, bd, preferred_element_type=F32) + jnp.dot(lo, bd, preferred_element_type=F32)


def _rope(x, cos, sin_signed):
    w = x.shape[-1]
    lane = lax.broadcasted_iota(jnp.int32, (1, w), 1)
    first = (lane & 31) < 16
    partner = jnp.where(first, pltpu.roll(x, w - 16, axis=1), pltpu.roll(x, 16, axis=1))
    return x * cos + partner * sin_signed


def _tile_lanes(x, reps):
    return jnp.concatenate([x] * reps, axis=-1) if reps > 1 else x


P0_G_PRE, P0_G_POST, P0_MU, P0_W0, P0_A0, P0_KK, P0_KA, P0_RK, P0_GNG, P0_GNB, P0_GQ, P0_GK = range(12)
W0_COLS = (0, 1792, 2304, 2816, 2944, 3072, 3584)


def _prow(p_ref, i, n):
    return p_ref[i:i + 1, 0:n]


def _in0_kernel(use_rope, x_ref, sc_ref, sh_ref, p_ref, w_ref, bdq_ref, bdk_ref, cos_ref, sin_ref,
                slab_ref, rg_ref, q_ref, k_ref, v_ref, ag_ref):
    dm = x_ref.shape[-1]
    h = _prenorm(x_ref[0], _prow(p_ref, P0_G_PRE, dm), sc_ref[0], sh_ref[0]).astype(BF16)
    c = W0_COLS
    proj = lambda j: jnp.dot(h, w_ref[:, c[j]:c[j + 1]], preferred_element_type=F32)
    slab_ref[0] = proj(0)
    rg_ref[0] = proj(1)
    v_ref[0] = proj(4)
    ag_ref[0] = proj(5)
    q = proj(2)
    k = proj(3)
    q = q * lax.rsqrt(_seg_sum(q * q, bdq_ref) * (1.0 / HEAD64) + EPS) * _prow(p_ref, P0_GQ, q.shape[-1])
    k = k * lax.rsqrt(_seg_sum(k * k, bdk_ref) * (1.0 / HEAD64) + EPS) * _prow(p_ref, P0_GK, k.shape[-1])
    if use_rope:
        cos = cos_ref[...]
        sin = sin_ref[...]
        q = _rope(q, _tile_lanes(cos, q.shape[-1] // LANES), _tile_lanes(sin, q.shape[-1] // LANES))
        k = _rope(k, cos, sin)
    q_ref[0] = q
    k_ref[0] = k


def _const_spec(shape, single=False):
    nd = len(shape)
    if single:
        return pl.BlockSpec(shape, lambda *_: (0,) * nd, pipeline_mode=pl.Buffered(1))
    return pl.BlockSpec(shape, lambda *_: (0,) * nd)


def _mod_spec(arr, d):
    if arr.shape[0] == 1:
        return pl.BlockSpec((1, 1, d), lambda b, i: (0, 0, 0))
    return pl.BlockSpec((1, 1, d), lambda b, i: (b, 0, 0))


def _in0(x, scale1p, shift, w, use_rope, tm):
    bsz, t, d = x.shape
    widths = [W0_COLS[j + 1] - W0_COLS[j] for j in range(6)]
    outs = [jax.ShapeDtypeStruct((bsz, t, n), F32) for n in widths]
    tok = lambda n: pl.BlockSpec((1, tm, n), lambda b, i: (b, i, 0))
    in_specs = [tok(d), _mod_spec(scale1p, d), _mod_spec(shift, d), _const_spec(w["p"].shape),
                _const_spec(w["win"].shape, single=True), _const_spec(w["bdq"].shape), _const_spec(w["bdk"].shape)]
    in_specs += [pl.BlockSpec((tm, LANES), lambda b, i: (i, 0))] * 2
    return pl.pallas_call(
        functools.partial(_in0_kernel, use_rope),
        out_shape=outs,
        grid=(bsz, t // tm),
        in_specs=in_specs,
        out_specs=[tok(n) for n in widths],
        compiler_params=pltpu.CompilerParams(dimension_semantics=("parallel", "parallel"),
                                             vmem_limit_bytes=VMEM_LIMIT),
        name="in_proj0",
    )(x, scale1p, shift, w["p"], w["win"], w["bdq"], w["bdk"], w["cos"], w["sin"])


def _neighbours(x, prev_blk, next_blk, i, n_blocks):
    tm = x.shape[0]
    row = lax.broadcasted_iota(jnp.int32, (tm, 1), 0)
    prev_row = jnp.where(i == 0, 0.0, prev_blk[SUBLANES - 1:SUBLANES, :])
    next_row = jnp.where(i == n_blocks - 1, 0.0, next_blk[0:1, :])
    prev = jnp.where(row == 0, prev_row, pltpu.roll(x, 1, axis=0))
    nxt = jnp.where(row == tm - 1, next_row, pltpu.roll(x, tm - 1, axis=0))
    return prev, nxt


def _halo_specs(tm, c, t):
    r = tm // SUBLANES
    last = t // SUBLANES - 1
    main = pl.BlockSpec((1, tm, c), lambda b, i: (b, i, 0))
    prev = pl.BlockSpec((1, SUBLANES, c), lambda b, i: (b, jnp.maximum(i * r - 1, 0), 0))
    nxt = pl.BlockSpec((1, SUBLANES, c), lambda b, i: (b, jnp.minimum((i + 1) * r, last), 0))
    return [main, prev, nxt]


def _rwprep_kernel(n_blocks, s_ref, sp_ref, sn_ref, p_ref, w2_ref, a2_ref, bd_ref,
                   r_ref, k_ref, v_ref, kkn_ref, b_ref, lw_ref):
    i = pl.program_id(1)
    x = s_ref[0]
    prev, nxt = _neighbours(x, sp_ref[0], sn_ref[0], i, n_blocks)
    xs = x + _prow(p_ref, P0_MU, x.shape[-1]) * (0.5 * (prev + nxt) - x)
    c = 512
    r, kr, vr = xs[:, :c], xs[:, c:2 * c], xs[:, 2 * c:3 * c]
    lw_in = xs[:, 3 * c:3 * c + LANES]
    la_in = xs[:, 3 * c + LANES:3 * c + 2 * LANES]
    lor = _mm(jnp.tanh(lw_in), w2_ref[...]) + _prow(p_ref, P0_W0, 2 * c)
    logw = -RW_DECAY_SCALE * jax.nn.sigmoid(lor)
    a = jax.nn.sigmoid(_prow(p_ref, P0_A0, c) + _mm(la_in, a2_ref[...]))
    kkp = kr * _prow(p_ref, P0_KK, c)
    kkn = kkp * lax.rsqrt(_seg_sum(kkp * kkp, bd_ref) + EPS)
    r_ref[0] = r
    k_ref[0] = kr * (1.0 + (a - 1.0) * _prow(p_ref, P0_KA, c))
    v_ref[0] = vr
    kkn_ref[0] = kkn
    b_ref[0] = kkn * a
    lw_ref[0, 0] = logw[:, :c]
    lw_ref[1, 0] = logw[:, c:]


def _rwprep(slab, w, tm):
    bsz, t, cs = slab.shape
    c = 512
    nb = t // tm
    tok = pl.BlockSpec((1, tm, c), lambda b, i: (b, i, 0))
    outs = [jax.ShapeDtypeStruct((bsz, t, c), F32)] * 5 + [jax.ShapeDtypeStruct((2, bsz, t, c), F32)]
    names = ("p", "w2", "a2", "bd")
    return pl.pallas_call(
        functools.partial(_rwprep_kernel, nb),
        out_shape=outs,
        grid=(bsz, nb),
        in_specs=_halo_specs(tm, cs, t) + [_const_spec(w[k].shape) for k in names],
        out_specs=[tok] * 5 + [pl.BlockSpec((2, 1, tm, c), lambda b, i: (0, b, i, 0))],
        compiler_params=pltpu.CompilerParams(dimension_semantics=("parallel", "parallel"),
                                             vmem_limit_bytes=VMEM_LIMIT),
        name="rwkv_prep",
    )(slab, slab, slab, *[w[k] for k in names])


QUAD = 4 * HEAD64


RW_SUB = 2
GDN_SUB = 4
GLA_SUB = 4


def _rwscan_kernel(nc, nsub, rf, kf, vf, kkf, bf, lwf, rb, kb, vb, kkb, bb, lwb, s0_ref,
                   of_ref, ob_ref, sf_ref, st_ref):
    c = pl.program_id(1)
    L = CHUNK
    nq = rf.shape[-1] // QUAD

    @pl.when(c == 0)
    def _():
        st_ref[...] = s0_ref[0]

    hmask = _lane_group_masks(QUAD, HEAD64, 4)
    rowq = lax.broadcasted_iota(jnp.int32, (QUAD, QUAD), 0)
    colq = lax.broadcasted_iota(jnp.int32, (QUAD, QUAD), 1)
    bd = (rowq >> 6) == (colq >> 6)
    masks = []
    rowc = lax.broadcasted_iota(jnp.int32, (L, QUAD), 0)
    colc = lax.broadcasted_iota(jnp.int32, (L, QUAD), 1) & (L - 1)
    for rev in (False, True):
        _, incl = _order_masks(L, rev)
        dlt = (colc - rowc) if rev else (rowc - colc)
        masks.append((jnp.where(incl, 1.0, 0.0).astype(BF16), dlt > 0, dlt >= 0))
    refs = ((rf, kf, vf, kkf, bf, lwf), (rb, kb, vb, kkb, bb, lwb))

    rows = lambda d, s: slice(s * L, (s + 1) * L) if d == 0 else slice((nsub - 1 - s) * L, (nsub - s) * L)
    units = [(s, d, qd) for s in range(nsub) for d in range(2) for qd in range(nq)]

    dat = []
    for s, d, qd in units:
        sl = slice(qd * QUAD, (qd + 1) * QUAD)
        rw = rows(d, s)
        r_, k_, v_, kk_, b_, lw_ = refs[d]
        dat.append(dict(r=r_[0, rw, sl], k=k_[0, rw, sl], v=v_[0, rw, sl], kk=kk_[0, rw, sl], b=b_[0, rw, sl],
                        lw=lw_[0, 0, rw, sl]))
    cws = [_mm_exact_l(masks[d][0], x["lw"]) for (_, d, _), x in zip(units, dat)]
    for x, cw in zip(dat, cws):
        c0 = cw[L // 2:L // 2 + 1, :]
        tot = jnp.sum(x["lw"], axis=0, keepdims=True)
        e_rel = jnp.exp(cw - c0)
        e_inv = jnp.exp(c0 - cw)
        ec0 = jnp.exp(c0)
        e_fin = jnp.exp(tot - c0)
        a_rel = x["kk"] * e_rel * jnp.exp(-x["lw"])
        r_rel = x["r"] * e_rel
        b_rel = x["b"] * e_inv
        k_rel = x["k"] * e_inv
        x.update(ar_abs=jnp.concatenate([a_rel * ec0, r_rel * ec0], axis=0),
                 ar_rel=jnp.concatenate([a_rel, r_rel], axis=0),
                 sbk=jnp.concatenate([_stack_heads(b_rel, hmask), _stack_heads(k_rel, hmask)], axis=0),
                 bk_fin=jnp.concatenate([b_rel * e_fin, k_rel * e_fin], axis=0), w_tot=jnp.exp(tot),
                 sv=_stack_heads(x["v"], hmask))
    for (_, d, _), x in zip(units, dat):
        _, strict_c, incl_c = masks[d]
        xx = _mm_nt(x["ar_rel"], x["sbk"])
        x["n_c"] = jnp.where(strict_c, xx[0:L, 0:QUAD], 0.0)
        x["mp"] = jnp.concatenate([jnp.where(strict_c, xx[0:L, QUAD:], 0.0),
                                   jnp.where(incl_c, xx[L:, QUAD:], 0.0)], axis=0)
        x["p_rb"] = jnp.where(incl_c, xx[L:, 0:QUAD], 0.0)
    tinv = _tri_inv_many([x["n_c"][:, h * L:(h + 1) * L] for x in dat for h in range(4)], _mm)
    for i, x in enumerate(dat):
        x["t_c"] = jnp.concatenate(tinv[4 * i:4 * i + 4], axis=-1)
        x["mpv"] = _mm(x["mp"], x["sv"])

    state = {(d, qd): st_ref[d, qd] for d in range(2) for qd in range(nq)}
    for s in range(nsub):
        cur = [(d, qd, dat[(s * 2 + d) * nq + qd]) for d in range(2) for qd in range(nq)]
        for d, qd, x in cur:
            x["ars"] = _mm_nt(x["ar_abs"], state[d, qd])
        for d, qd, x in cur:
            rhs = -x["ars"][0:L] - x["mpv"][0:L]
            x["e"] = _mm(x["t_c"], _stack_heads(rhs, hmask))
        for d, qd, x in cur:
            x["o"] = x["ars"][L:2 * L] + x["mpv"][L:2 * L] + _mm(x["p_rb"], _stack_heads(x["e"], hmask))
            ev = jnp.concatenate([x["e"], x["v"]], axis=0)
            state[d, qd] = state[d, qd] * x["w_tot"] + jnp.where(bd, _mm_tn(ev, x["bk_fin"]), 0.0)
        for d, o_ref in ((0, of_ref), (1, ob_ref)):
            o_ref[0, rows(d, s), :] = jnp.concatenate([x["o"] for dd, _, x in cur if dd == d], axis=-1)
    for (d, qd), val in state.items():
        st_ref[d, qd] = val

    @pl.when(c == nc - 1)
    def _():
        sf_ref[0] = st_ref[...]


def _dir_specs(block, nc, lead=None):
    specs = []
    for d in range(2):
        idx = (lambda b_, c: c) if d == 0 else (lambda b_, c: nc - 1 - c)
        if lead is None:
            specs.append(pl.BlockSpec((1,) + block, functools.partial(
                lambda f, b_, c: (b_, f(b_, c)) + (0,) * (len(block) - 1), idx)))
        else:
            specs.append(pl.BlockSpec((1, 1) + block, functools.partial(
                lambda f, dd, b_, c: (dd, b_, f(b_, c)) + (0,) * (len(block) - 1), idx, d)))
    return specs


def _rwscan(r, k, v, kk, b, lw, s0):
    bsz, t, cdim = r.shape
    blk = RW_SUB * CHUNK
    nc = t // blk
    nq = cdim // QUAD
    tf, tb = _dir_specs((blk, cdim), nc)
    lf, lb = _dir_specs((blk, cdim), nc, lead=True)
    sspec = pl.BlockSpec((1, 2, nq, QUAD, QUAD), lambda b_, c: (b_, 0, 0, 0, 0))
    return pl.pallas_call(
        functools.partial(_rwscan_kernel, nc, RW_SUB),
        out_shape=[jax.ShapeDtypeStruct((bsz, t, cdim), F32)] * 2
                  + [jax.ShapeDtypeStruct((bsz, 2, nq, QUAD, QUAD), F32)],
        grid=(bsz, nc),
        in_specs=[tf] * 5 + [lf] + [tb] * 5 + [lb] + [sspec],
        out_specs=[tf, tb, sspec],
        scratch_shapes=[pltpu.VMEM((2, nq, QUAD, QUAD), F32)],
        compiler_params=pltpu.CompilerParams(dimension_semantics=("parallel", "arbitrary"),
                                             vmem_limit_bytes=VMEM_LIMIT),
        name="rwkv_scan",
    )(r, k, v, kk, b, lw, r, k, v, kk, b, lw, s0)


def _attn_kernel(n_kv, q_ref, k_ref, vt_ref, o_ref):
    n_heads = q_ref.shape[-1] // HEAD64
    group = n_heads // n_kv
    lane = lax.broadcasted_iota(jnp.int32, (1, LANES), 1)
    kmat = k_ref[0]
    ones = vt_ref[0, n_kv * HEAD64:, :]
    vts = [jnp.concatenate([vt_ref[0, kv * HEAD64:(kv + 1) * HEAD64, :], ones], axis=0) for kv in range(n_kv)]
    def scores(h):
        kv = h // group
        pair = q_ref[0, :, (h // 2) * LANES:(h // 2 + 1) * LANES] * (HEAD64 ** -0.5 * LOG2E)
        if (h % 2) != kv:
            pair = pltpu.roll(pair, HEAD64, axis=1)
        qh = jnp.where((lane >= kv * HEAD64) & (lane < (kv + 1) * HEAD64), pair, 0.0).astype(BF16)
        return lax.dot_general(kmat, qh, (((1,), (1,)), ((), ())), preferred_element_type=F32).astype(BF16)

    outs = []
    st_next = scores(0)
    for h in range(n_heads):
        st = st_next
        if h + 1 < n_heads:
            st_next = scores(h + 1)
        m = jnp.max(st, axis=0, keepdims=True)
        pt = jnp.exp2(st - m)
        res = jnp.dot(vts[h // group], pt, preferred_element_type=F32)
        outs.append(res[0:HEAD64] / res[HEAD64:HEAD64 + 1])
    o_ref[0] = jnp.concatenate(outs, axis=0).T


def _attention(q, k_all, vt_all, tq):
    bsz, t, c = q.shape
    tk = k_all.shape[1]
    n_kv = k_all.shape[-1] // HEAD64
    assert n_kv * HEAD64 == LANES
    return pl.pallas_call(
        functools.partial(_attn_kernel, n_kv),
        out_shape=jax.ShapeDtypeStruct((bsz, t, c), F32),
        grid=(bsz, t // tq),
        in_specs=[pl.BlockSpec((1, tq, c), lambda b, i: (b, i, 0)),
                  pl.BlockSpec((1, tk, LANES), lambda b, i: (b, 0, 0), pipeline_mode=pl.Buffered(1)),
                  pl.BlockSpec((1, vt_all.shape[1], tk), lambda b, i: (b, 0, 0), pipeline_mode=pl.Buffered(1))],
        out_specs=pl.BlockSpec((1, tq, c), lambda b, i: (b, i, 0)),
        compiler_params=pltpu.CompilerParams(dimension_semantics=("parallel", "parallel"),
                                             vmem_limit_bytes=VMEM_LIMIT),
        name="attention",
    )(q, k_all, vt_all)


def _postnorm_residual(x, out, g_post, gate):
    ms = jnp.mean(out * out, axis=-1, keepdims=True)
    return x + gate * (out * lax.rsqrt(ms + EPS) * g_post)


def _out0_kernel(x_ref, gate_ref, of_ref, ob_ref, r_ref, k_ref, v_ref, rg_ref, oat_ref, ag_ref,
                 p_ref, bd_ref, wo_ref, y_ref):
    o = of_ref[0] + ob_ref[0]
    c = o.shape[-1]
    inv = 1.0 / HEAD64
    mu = _seg_sum(o, bd_ref) * inv
    dlt = o - mu
    var = _seg_sum(dlt * dlt, bd_ref) * inv
    gn = dlt * lax.rsqrt(var + GN_EPS) * _prow(p_ref, P0_GNG, c) + _prow(p_ref, P0_GNB, c)
    bonus = _seg_sum(r_ref[0] * k_ref[0] * _prow(p_ref, P0_RK, c), bd_ref) * v_ref[0]
    o_rw = (gn + bonus) * _silu(rg_ref[0])
    o_at = oat_ref[0] * _silu(ag_ref[0])
    out = _mm(o_rw, wo_ref[0:c, :]) + _mm(o_at, wo_ref[c:2 * c, :])
    y_ref[0] = _postnorm_residual(x_ref[0], out, _prow(p_ref, P0_G_POST, x_ref.shape[-1]), gate_ref[0])


def _out0(x, gate, o_f, o_b, r, k, v, rg, o_at, ag, w, tm):
    bsz, t, d = x.shape
    c = 512
    tokd = pl.BlockSpec((1, tm, d), lambda b, i: (b, i, 0))
    tok = pl.BlockSpec((1, tm, c), lambda b, i: (b, i, 0))
    names = ("p", "bd", "wout")
    return pl.pallas_call(
        _out0_kernel,
        out_shape=jax.ShapeDtypeStruct((bsz, t, d), F32),
        grid=(bsz, t // tm),
        in_specs=[tokd, _mod_spec(gate, d)] + [tok] * 8
                 + [_const_spec(w[n].shape) for n in names],
        out_specs=tokd,
        compiler_params=pltpu.CompilerParams(dimension_semantics=("parallel", "parallel"),
                                             vmem_limit_bytes=VMEM_LIMIT),
        name="out_proj0",
    )(x, gate, o_f, o_b, r, k, v, rg, o_at, ag, *[w[n] for n in names])


(P1_G_PRE, P1_G_POST, P1_GB, P1_ALOG_F, P1_ALOG_B, P1_DTB_F, P1_DTB_B, P1_GLA_G, P1_GDN_G, P1_CONV) = range(10)
W1_COLS = (0, 256, 512, 1024, 1152, 1664, 3200, 3328, 3456, 3968)


def _in1_kernel(x_ref, sc_ref, sh_ref, p_ref, w_ref, g2_ref,
                q_ref, k_ref, v_ref, lw_ref, gg_ref, dqkv_ref, small_ref, dg_ref):
    dm = x_ref.shape[-1]
    h = _prenorm(x_ref[0], _prow(p_ref, P1_G_PRE, dm), sc_ref[0], sh_ref[0]).astype(BF16)
    c = W1_COLS
    proj = lambda j: jnp.dot(h, w_ref[:, c[j]:c[j + 1]], preferred_element_type=F32)
    q_ref[0] = proj(0) * (HEAD64 ** -0.5)
    k_ref[0] = proj(1)
    v_ref[0] = proj(2)
    gg_ref[0] = proj(4)
    dqkv_ref[0] = proj(5)
    small_ref[0, 0] = proj(6)
    small_ref[1, 0] = proj(7)
    dg_ref[0] = proj(8)
    lw = _log_sigmoid(_mm(proj(3), g2_ref[...]) + _prow(p_ref, P1_GB, g2_ref.shape[-1])) * (1.0 / GLA_GATE_NORM)
    half = lw.shape[-1] // 2
    lw_ref[0, 0] = lw[:, :half]
    lw_ref[1, 0] = lw[:, half:]


def _in1(x, scale1p, shift, w, tm):
    bsz, t, d = x.shape
    widths = [256, 256, 512, -256, 512, 1536, -LANES, 512]
    tok = lambda n: pl.BlockSpec((1, tm, n), lambda b, i: (b, i, 0))
    outs, ospecs = [], []
    for n in widths:
        if n < 0:
            outs.append(jax.ShapeDtypeStruct((2, bsz, t, -n), F32))
            ospecs.append(pl.BlockSpec((2, 1, tm, -n), lambda b, i: (0, b, i, 0)))
        else:
            outs.append(jax.ShapeDtypeStruct((bsz, t, n), F32))
            ospecs.append(tok(n))
    return pl.pallas_call(
        _in1_kernel,
        out_shape=outs,
        grid=(bsz, t // tm),
        in_specs=[tok(d), _mod_spec(scale1p, d), _mod_spec(shift, d), _const_spec(w["p"].shape),
                  _const_spec(w["win"].shape, single=True), _const_spec(w["g2"].shape)],
        out_specs=ospecs,
        compiler_params=pltpu.CompilerParams(dimension_semantics=("parallel", "parallel"),
                                             vmem_limit_bytes=VMEM_LIMIT),
        name="in_proj1",
    )(x, scale1p, shift, w["p"], w["win"], w["g2"])


def _gdnprep_kernel(n_blocks, x_ref, xp_ref, xn_ref, s_ref, p_ref, q_ref, k_ref, v_ref, g_ref):
    i = pl.program_id(1)
    x = x_ref[0]
    prev, nxt = _neighbours(x, xp_ref[0], xn_ref[0], i, n_blocks)
    cw = lambda j: _prow(p_ref, P1_CONV + j, x.shape[-1])
    y = _silu(prev * cw(0) + x * cw(1) + nxt * cw(2))
    c = 512
    qs, ks = [], []
    for h in range(c // HEAD128):
        qh = y[:, h * HEAD128:(h + 1) * HEAD128]
        kh = y[:, c + h * HEAD128:c + (h + 1) * HEAD128]
        qs.append(qh * lax.rsqrt(jnp.sum(qh * qh, axis=-1, keepdims=True) + EPS) * (HEAD128 ** -0.5))
        ks.append(kh * lax.rsqrt(jnp.sum(kh * kh, axis=-1, keepdims=True) + EPS))
    q_ref[0] = jnp.concatenate(qs, axis=-1)
    k_ref[0] = jnp.concatenate(ks, axis=-1)
    v_ref[0] = y[:, 2 * c:]
    lane = lax.broadcasted_iota(jnp.int32, (1, LANES), 1)
    nh = c // HEAD128
    for d, (ia, ib) in enumerate(((P1_ALOG_F, P1_DTB_F), (P1_ALOG_B, P1_DTB_B))):
        s = s_ref[d, 0]
        loga = -jnp.exp(_prow(p_ref, ia, LANES)) * _softplus(s + _prow(p_ref, ib, LANES))
        g_ref[d, 0] = jnp.where(lane < nh, loga, jnp.where(lane < 2 * nh, jax.nn.sigmoid(s), 0.0))


def _gdnprep(dqkv, small, w, tm):
    bsz, t, cs = dqkv.shape
    c = 512
    nb = t // tm
    tok = pl.BlockSpec((1, tm, c), lambda b, i: (b, i, 0))
    tokl = pl.BlockSpec((2, 1, tm, LANES), lambda b, i: (0, b, i, 0))
    return pl.pallas_call(
        functools.partial(_gdnprep_kernel, nb),
        out_shape=[jax.ShapeDtypeStruct((bsz, t, c), F32)] * 3 + [jax.ShapeDtypeStruct((2, bsz, t, LANES), F32)],
        grid=(bsz, nb),
        in_specs=_halo_specs(tm, cs, t) + [tokl, _const_spec(w["p"].shape)],
        out_specs=[tok] * 3 + [tokl],
        compiler_params=pltpu.CompilerParams(dimension_semantics=("parallel", "parallel"),
                                             vmem_limit_bytes=VMEM_LIMIT),
        name="gdn_prep",
    )(dqkv, dqkv, dqkv, small, w["p"])


def _glascan_kernel(nc, nsub, qf, kf, vf, lwf, qb, kb, vb, lwb, s0_ref, of_ref, ob_ref, sf_ref, st_ref):
    c = pl.program_id(1)
    L = CHUNK
    nh = qf.shape[-1] // HEAD64

    @pl.when(c == 0)
    def _():
        st_ref[...] = s0_ref[0]

    hmask = _lane_group_masks(nh * HEAD64, HEAD64, nh)
    rowv = lax.broadcasted_iota(jnp.int32, st_ref.shape[1:], 0)
    colk = lax.broadcasted_iota(jnp.int32, st_ref.shape[1:], 1)
    bd = (rowv >> 7) == (colk >> 6)
    refs = ((qf, kf, vf, lwf, of_ref), (qb, kb, vb, lwb, ob_ref))
    rows = lambda d, s: slice(s * L, (s + 1) * L) if d == 0 else slice((nsub - 1 - s) * L, (nsub - s) * L)
    masks = []
    for d in range(2):
        _, incl = _order_masks(L, d == 1)
        _, incl4 = _order_masks(L, d == 1, reps=nh)
        masks.append((jnp.where(incl, 1.0, 0.0).astype(BF16), incl4))

    units = [(s, d) for s in range(nsub) for d in range(2)]
    dat = []
    for s, d in units:
        q_, k_, v_, lw_, _ = refs[d]
        rw = rows(d, s)
        dat.append(dict(q=q_[0, rw, :], k=k_[0, rw, :], v=v_[0, rw, :], lw=lw_[0, 0, rw, :]))
    cws = [_mm_exact_l(masks[d][0], x["lw"]) for (_, d), x in zip(units, dat)]
    for x, cw in zip(dat, cws):
        c0 = cw[L // 2:L // 2 + 1, :]
        tot = jnp.sum(x["lw"], axis=0, keepdims=True)
        q_rel = x["q"] * jnp.exp(cw - c0)
        k_rel = x["k"] * jnp.exp(c0 - cw)
        x.update(sq=_stack_heads(q_rel, hmask), k_rel=k_rel, q_abs=q_rel * jnp.exp(c0),
                 k_fin=k_rel * jnp.exp(tot - c0), w_tot=jnp.exp(tot))
    for (_, d), x in zip(units, dat):
        x["att"] = jnp.where(masks[d][1], _mm_nt(x["sq"], x["k_rel"]), 0.0)
        x["vk"] = jnp.where(bd, _mm_tn(x["v"], x["k_fin"]), 0.0)
    for x in dat:
        x["intra"] = jnp.concatenate(
            [_mm(x["att"][h * L:(h + 1) * L], x["v"][:, h * HEAD128:(h + 1) * HEAD128]) for h in range(nh)], axis=-1)

    state = [st_ref[0], st_ref[1]]
    for s in range(nsub):
        for d in range(2):
            x = dat[s * 2 + d]
            refs[d][4][0, rows(d, s), :] = x["intra"] + _mm_nt(x["q_abs"], state[d])
            state[d] = state[d] * x["w_tot"] + x["vk"]
    st_ref[0] = state[0]
    st_ref[1] = state[1]

    @pl.when(c == nc - 1)
    def _():
        sf_ref[0] = st_ref[...]


def _glascan(q, k, v, lw, s0):
    bsz, t, ck = q.shape
    cv = v.shape[-1]
    blk = GLA_SUB * CHUNK
    nc = t // blk
    kf, kb = _dir_specs((blk, ck), nc)
    vf, vb = _dir_specs((blk, cv), nc)
    lf, lb = _dir_specs((blk, ck), nc, lead=True)
    sspec = pl.BlockSpec((1, 2, cv, ck), lambda b_, c: (b_, 0, 0, 0))
    return pl.pallas_call(
        functools.partial(_glascan_kernel, nc, GLA_SUB),
        out_shape=[jax.ShapeDtypeStruct((bsz, t, cv), F32)] * 2 + [jax.ShapeDtypeStruct((bsz, 2, cv, ck), F32)],
        grid=(bsz, nc),
        in_specs=[kf, kf, vf, lf, kb, kb, vb, lb, sspec],
        out_specs=[vf, vb, sspec],
        scratch_shapes=[pltpu.VMEM((2, cv, ck), F32)],
        compiler_params=pltpu.CompilerParams(dimension_semantics=("parallel", "arbitrary"),
                                             vmem_limit_bytes=VMEM_LIMIT),
        name="gla_scan",
    )(q, k, v, lw, q, k, v, lw, s0)


def _gdnscan_kernel(nc, nsub, qf, kf, vf, gcf, qb, kb, vb, gcb, s0_ref, of_ref, ob_ref, sf_ref, st_ref):
    c = pl.program_id(1)
    L = CHUNK
    nh = qf.shape[-1] // HEAD128

    @pl.when(c == 0)
    def _():
        st_ref[...] = s0_ref[0]

    refs = ((qf, kf, vf, gcf), (qb, kb, vb, gcb))
    rows = lambda d, s: slice(s * L, (s + 1) * L) if d == 0 else slice((nsub - 1 - s) * L, (nsub - s) * L)
    per_dir = {}
    for d in range(2):
        strict, incl = _order_masks(L, d == 1)
        _, incl_t = _order_masks(L, d == 0)
        incl_bf = jnp.where(incl, 1.0, 0.0).astype(BF16)
        incl_t_bf = jnp.where(incl_t, 1.0, 0.0).astype(BF16)
        for s in range(nsub):
            gcol = refs[d][3][0, 0, rows(d, s), :]
            per_dir[d, s] = dict(strict=strict, incl=incl, gcol=gcol,
                                 cum_c=_mm_exact_l(incl_bf, gcol),
                                 cum_r=_mm_exact_tn(gcol, incl_t_bf),
                                 tot=jnp.sum(gcol, axis=0, keepdims=True))

    units = [(s, d, h) for s in range(nsub) for d in range(2) for h in range(nh)]
    dat = []
    for s, d, h in units:
        hs = slice(h * HEAD128, (h + 1) * HEAD128)
        pd = per_dir[d, s]
        rw = rows(d, s)
        q, k, v = refs[d][0][0, rw, hs], refs[d][1][0, rw, hs], refs[d][2][0, rw, hs]
        g = pd["cum_c"][:, h:h + 1]
        beta = pd["gcol"][:, nh + h:nh + h + 1]
        g_last = pd["tot"][:, h:h + 1]
        gam = jnp.exp(g)
        dat.append(dict(q=q, k=k, qg=q * gam, beta=beta, pd=pd,
                        dec=jnp.exp(jnp.minimum(g - pd["cum_r"][h:h + 1, :], 0.0)),
                        rhs=jnp.concatenate([(beta * gam) * k, beta * v], axis=-1),
                        k_dec=k * jnp.exp(g_last - g), gl=jnp.exp(g_last)))
    for x in dat:
        x["kk"] = _mm_nt(x["k"], x["k"])
        x["aqk"] = jnp.where(x["pd"]["incl"], x["dec"] * _mm_nt(x["q"], x["k"]), 0.0)
    tinv = _tri_inv_many([jnp.where(x["pd"]["strict"], x["dec"] * x["kk"] * x["beta"], 0.0) for x in dat], _mm)
    for x, t in zip(dat, tinv):
        x["wu"] = _mm(t, x["rhs"])
        x["wq"] = jnp.concatenate([x["wu"][:, :HEAD128], x["qg"]], axis=0)

    state = {(d, h): st_ref[d, h] for d in range(2) for h in range(nh)}
    for s in range(nsub):
        cur = [(d, h, dat[(s * 2 + d) * nh + h]) for d in range(2) for h in range(nh)]
        for d, h, x in cur:
            x["ws"] = _mm(x["wq"], state[d, h])
        for d, h, x in cur:
            x["u"] = x["wu"][:, HEAD128:] - x["ws"][0:L]
        for d, h, x in cur:
            x["o"] = x["ws"][L:2 * L] + _mm(x["aqk"], x["u"])
            state[d, h] = state[d, h] * x["gl"] + _mm_tn(x["k_dec"], x["u"])
        for d, o_ref in ((0, of_ref), (1, ob_ref)):
            o_ref[0, rows(d, s), :] = jnp.concatenate([x["o"] for dd, _, x in cur if dd == d], axis=-1)
    for (d, h), val in state.items():
        st_ref[d, h] = val

    @pl.when(c == nc - 1)
    def _():
        sf_ref[0] = st_ref[...]


def _gdnscan(q, k, v, gcol, s0):
    bsz, t, cdim = q.shape
    nh = cdim // HEAD128
    blk = GDN_SUB * CHUNK
    nc = t // blk
    tf, tb = _dir_specs((blk, cdim), nc)
    gcf, gcb = _dir_specs((blk, LANES), nc, lead=True)
    sspec = pl.BlockSpec((1, 2, nh, HEAD128, HEAD128), lambda b_, c: (b_, 0, 0, 0, 0))
    return pl.pallas_call(
        functools.partial(_gdnscan_kernel, nc, GDN_SUB),
        out_shape=[jax.ShapeDtypeStruct((bsz, t, cdim), F32)] * 2
                  + [jax.ShapeDtypeStruct((bsz, 2, nh, HEAD128, HEAD128), F32)],
        grid=(bsz, nc),
        in_specs=[tf, tf, tf, gcf, tb, tb, tb, gcb, sspec],
        out_specs=[tf, tb, sspec],
        scratch_shapes=[pltpu.VMEM((2, nh, HEAD128, HEAD128), F32)],
        compiler_params=pltpu.CompilerParams(dimension_semantics=("parallel", "arbitrary"),
                                             vmem_limit_bytes=VMEM_LIMIT),
        name="gdn_scan",
    )(q, k, v, gcol, q, k, v, gcol, s0)


def _head_rmsnorm(o, g):
    parts = []
    for h in range(o.shape[-1] // HEAD128):
        oh = o[:, h * HEAD128:(h + 1) * HEAD128]
        parts.append(oh * lax.rsqrt(jnp.mean(oh * oh, axis=-1, keepdims=True) + EPS) * g)
    return jnp.concatenate(parts, axis=-1)


def _out1_kernel(x_ref, gate_ref, glf_ref, glb_ref, gdf_ref, gdb_ref, gg_ref, dg_ref, p_ref, wo_ref, y_ref):
    c = gg_ref.shape[-1]
    o_gla = _head_rmsnorm(glf_ref[0] + glb_ref[0], _prow(p_ref, P1_GLA_G, HEAD128)) * _silu(gg_ref[0])
    o_gdn = _head_rmsnorm(gdf_ref[0] + gdb_ref[0], _prow(p_ref, P1_GDN_G, HEAD128)) * _silu(dg_ref[0])
    out = _mm(o_gla, wo_ref[0:c, :]) + _mm(o_gdn, wo_ref[c:2 * c, :])
    y_ref[0] = _postnorm_residual(x_ref[0], out, _prow(p_ref, P1_G_POST, x_ref.shape[-1]), gate_ref[0])


def _out1(x, gate, gla_f, gla_b, gdn_f, gdn_b, gg, dg, w, tm):
    bsz, t, d = x.shape
    c = 512
    tokd = pl.BlockSpec((1, tm, d), lambda b, i: (b, i, 0))
    tok = pl.BlockSpec((1, tm, c), lambda b, i: (b, i, 0))
    names = ("p", "wout")
    return pl.pallas_call(
        _out1_kernel,
        out_shape=jax.ShapeDtypeStruct((bsz, t, d), F32),
        grid=(bsz, t // tm),
        in_specs=[tokd, _mod_spec(gate, d)] + [tok] * 6
                 + [_const_spec(w[n].shape) for n in names],
        out_specs=tokd,
        compiler_params=pltpu.CompilerParams(dimension_semantics=("parallel", "parallel"),
                                             vmem_limit_bytes=VMEM_LIMIT),
        name="out_proj1",
    )(x, gate, gla_f, gla_b, gdn_f, gdn_b, gg, dg, *[w[n] for n in names])


def _block_ones(n_groups, width):
    return jnp.asarray(np.kron(np.eye(n_groups, dtype=np.float32), np.ones((width, width), np.float32)), BF16)


def _rope_tables(t):
    rows = t // GRID_W
    row_id = np.repeat(np.arange(rows, dtype=np.float32), GRID_W)
    col_id = np.tile(np.arange(GRID_W, dtype=np.float32), rows)
    nf = HEAD64 // 4
    inv = np.float32(ROPE_THETA) ** (-np.arange(nf, dtype=np.float32) / np.float32(nf))
    ang_r = (row_id[:, None] * inv[None, :]).astype(np.float32)
    ang_c = (col_id[:, None] * inv[None, :]).astype(np.float32)
    cos = np.concatenate([np.cos(ang_r)] * 2 + [np.cos(ang_c)] * 2, axis=-1)
    sin = np.concatenate([-np.sin(ang_r), np.sin(ang_r), -np.sin(ang_c), np.sin(ang_c)], axis=-1)
    tile2 = lambda a: jnp.asarray(np.concatenate([a, a], axis=-1).astype(np.float32))
    return tile2(cos), tile2(sin)


def _pack_rows(rows, width):
    return jnp.stack([jnp.pad(r.reshape(-1).astype(F32), (0, width - r.size)) for r in rows])


def _layer0_weights(p):
    w_in = p["w_in"]
    rw_w = 512
    slab_w = 3 * rw_w + 192
    d = w_in.shape[0]
    w = {}
    w["win"] = jnp.concatenate([w_in[:, :slab_w], jnp.zeros((d, W0_COLS[1] - slab_w), F32), w_in[:, slab_w:]],
                               axis=1).astype(BF16)
    w["wout"] = p["w_out"].astype(BF16)
    w["bdq"] = _block_ones(8, HEAD64)
    w["bdk"] = _block_ones(2, HEAD64)
    w["bd"] = w["bdq"]
    w["p"] = _pack_rows([p["g_pre"], p["g_post"], p["rw_mu"], jnp.concatenate([p["rw_w0_f"], p["rw_w0_b"]]),
                         p["rw_a0"], p["rw_k_k"], p["rw_k_a"], p["rw_r_k"], p["rw_gn_g"], p["rw_gn_b"],
                         jnp.tile(p["at_gq"], 8), jnp.tile(p["at_gk"], 2)], W0_COLS[1])
    z = jnp.zeros((64, rw_w), F32)
    w["w2"] = jnp.concatenate([jnp.concatenate([p["rw_w2_f"], z], axis=1),
                               jnp.concatenate([z, p["rw_w2_b"]], axis=1)], axis=0).astype(BF16)
    w["a2"] = jnp.concatenate([p["rw_a2"], z], axis=0).astype(BF16)
    return w


def _layer1_weights(p):
    w_in = p["w_in"]
    d = w_in.shape[0]
    zc = lambda n: jnp.zeros((d, n), F32)
    af, ab, be = w_in[:, 3104:3108], w_in[:, 3108:3112], w_in[:, 3112:3116]
    w = {}
    w["win"] = jnp.concatenate([w_in[:, :1024], w_in[:, 1024:1056], zc(LANES - 32), w_in[:, 1056:3104],
                                af, be, zc(LANES - 8), ab, be, zc(LANES - 8), w_in[:, 3116:]],
                               axis=1).astype(BF16)
    w["wout"] = p["w_out"].astype(BF16)
    z = jnp.zeros((16, 256), F32)
    g2 = jnp.concatenate([jnp.concatenate([p["gla_g2_f"], z], axis=1),
                          jnp.concatenate([z, p["gla_g2_b"]], axis=1)], axis=0)
    w["g2"] = jnp.pad(g2, ((0, LANES - 32), (0, 0))).astype(BF16)
    conv = p["gdn_conv"]
    w["p"] = _pack_rows([p["g_pre"], p["g_post"], jnp.concatenate([p["gla_gb_f"], p["gla_gb_b"]]),
                         p["gdn_A_log_f"], p["gdn_A_log_b"], p["gdn_dt_bias_f"], p["gdn_dt_bias_b"],
                         p["gla_norm_g"], p["gdn_norm_g"], conv[0], conv[1], conv[2]], conv.shape[1])
    return w


def _rw_state_to_bd(s):
    bsz = s.shape[0]
    s = s.reshape(bsz, 2, 4, HEAD64, HEAD64)
    eye = jnp.eye(4, dtype=s.dtype)
    return jnp.einsum("bqhvk,hg->bqhvgk", s, eye).reshape(bsz, 2, QUAD, QUAD)


def _rw_state_from_bd(s):
    bsz = s.shape[0]
    s = s.reshape(bsz, 2, 4, HEAD64, 4, HEAD64)
    return jnp.stack([s[:, :, h, :, h, :] for h in range(4)], axis=2).reshape(bsz, 8, HEAD64, HEAD64)


def _gla_state_to_bd(s):
    bsz = s.shape[0]
    eye = jnp.eye(4, dtype=s.dtype)
    return jnp.einsum("bhkv,hg->bhvgk", s, eye).reshape(bsz, 4 * HEAD128, 4 * HEAD64)


def _gla_state_from_bd(s):
    bsz = s.shape[0]
    s = s.reshape(bsz, 4, HEAD128, 4, HEAD64)
    return jnp.stack([jnp.swapaxes(s[:, h, :, h, :], -1, -2) for h in range(4)], axis=1)


def _trunk(x, mods, w0, w1, ctx, tm, tq):
    bsz, t, dm = x.shape
    latent = ctx is not None
    ts = min(tm, t)
    shared = mods[0][0].shape[0] == 1
    flat = (lambda a: a.reshape(a.shape[:-3] + (1, bsz * t, a.shape[-1]))) if shared else (lambda a: a)
    unflat = lambda a: a.reshape(a.shape[:-3] + (bsz, t, a.shape[-1]))

    sc, sh, gt = mods[0]
    slab, rg, q, k, v, ag = [unflat(a) for a in _in0(flat(x), sc, sh, w0, latent, tm)]
    r, kmod, vr, kkn, bvec, lw = _rwprep(slab, w0, ts)
    if latent:
        s0f, s0b, k_ctx, v_ctx = ctx[:4]
        s0 = jnp.stack([_rw_state_to_bd(s0f), _rw_state_to_bd(s0b)], axis=1)
        k_all = jnp.concatenate([k_ctx.reshape(bsz, -1, 2 * HEAD64), k], axis=1)
        v_all = jnp.concatenate([v_ctx.reshape(bsz, -1, 2 * HEAD64), v], axis=1)
    else:
        s0 = jnp.zeros((bsz, 2, 2, QUAD, QUAD), F32)
        k_all, v_all = k, v
    o_rwf, o_rwb, s_rw = _rwscan(r, kmod, vr, kkn, bvec, lw, s0)
    vt_all = jnp.concatenate([jnp.swapaxes(v_all, 1, 2), jnp.ones((bsz, 16, v_all.shape[1]), F32)], axis=1)
    o_at = _attention(q, k_all.astype(BF16), vt_all.astype(BF16), tq)
    x1 = _out0(flat(x), gt, *[flat(a) for a in (o_rwf, o_rwb, r, kmod, vr, rg, o_at, ag)], w0, tm)

    sc, sh, gt = mods[1]
    gq, gk, gv, glw, gg, dqkv, small, dg = [unflat(a) for a in _in1(x1, sc, sh, w1, tm)]
    dq, dk, dv, gcol = _gdnprep(dqkv, small, w1, ts)
    if latent:
        sgf, sgb, sdf, sdb = ctx[4:]
        s0_gla = jnp.stack([_gla_state_to_bd(sgf), _gla_state_to_bd(sgb)], axis=1)
        s0_gdn = jnp.stack([sdf, sdb], axis=1)
    else:
        s0_gla = jnp.zeros((bsz, 2, 4 * HEAD128, 4 * HEAD64), F32)
        s0_gdn = jnp.zeros((bsz, 2, 4, HEAD128, HEAD128), F32)
    gla_f, gla_b, s_gla = _glascan(gq, gk, gv, glw, s0_gla)
    gdn_f, gdn_b, s_gdn = _gdnscan(dq, dk, dv, gcol, s0_gdn)
    y = unflat(_out1(x1, gt, *[flat(a) for a in (gla_f, gla_b, gdn_f, gdn_b, gg, dg)], w1, tm))

    new = None
    if not latent:
        new = (_rw_state_from_bd(s_rw[:, 0]), _rw_state_from_bd(s_rw[:, 1]),
               k.reshape(bsz, t, 2, HEAD64), v.reshape(bsz, t, 2, HEAD64),
               _gla_state_from_bd(s_gla[:, 0]), _gla_state_from_bd(s_gla[:, 1]), s_gdn[:, 0], s_gdn[:, 1])
    return y, new


def _split_mod(m, d):
    shift, scale, gate = m[..., :d], m[..., d:2 * d], m[..., 2 * d:]
    return (1.0 + scale)[:, None, :], shift[:, None, :], gate[:, None, :]


def kernel(x_prompt, x_sample, state_l0_rwkv_fwd, state_l0_rwkv_bwd, cache_l0_k, cache_l0_v, state_l1_gla_fwd, state_l1_gla_bwd, state_l1_gdn_fwd, state_l1_gdn_bwd, c, c_ctx, l0_mod_w, l0_mod_b, l0_g_pre, l0_g_post, l0_w_in, l0_w_out, l0_rw_mu, l0_rw_w0_f, l0_rw_w2_f, l0_rw_w0_b, l0_rw_w2_b, l0_rw_a0, l0_rw_a2, l0_rw_k_k, l0_rw_k_a, l0_rw_r_k, l0_rw_gn_g, l0_rw_gn_b, l0_at_gq, l0_at_gk, l1_mod_w, l1_mod_b, l1_g_pre, l1_g_post, l1_w_in, l1_w_out, l1_gla_g2_f, l1_gla_gb_f, l1_gla_g2_b, l1_gla_gb_b, l1_gla_norm_g, l1_gdn_conv, l1_gdn_A_log_f, l1_gdn_dt_bias_f, l1_gdn_A_log_b, l1_gdn_dt_bias_b, l1_gdn_norm_g):
    p0 = {"g_pre": l0_g_pre, "g_post": l0_g_post, "w_in": l0_w_in, "w_out": l0_w_out, "rw_mu": l0_rw_mu,
          "rw_w0_f": l0_rw_w0_f, "rw_w2_f": l0_rw_w2_f, "rw_w0_b": l0_rw_w0_b, "rw_w2_b": l0_rw_w2_b,
          "rw_a0": l0_rw_a0, "rw_a2": l0_rw_a2, "rw_k_k": l0_rw_k_k, "rw_k_a": l0_rw_k_a,
          "rw_r_k": l0_rw_r_k, "rw_gn_g": l0_rw_gn_g, "rw_gn_b": l0_rw_gn_b,
          "at_gq": l0_at_gq, "at_gk": l0_at_gk}
    p1 = {"g_pre": l1_g_pre, "g_post": l1_g_post, "w_in": l1_w_in, "w_out": l1_w_out,
          "gla_g2_f": l1_gla_g2_f, "gla_gb_f": l1_gla_gb_f, "gla_g2_b": l1_gla_g2_b,
          "gla_gb_b": l1_gla_gb_b, "gla_norm_g": l1_gla_norm_g, "gdn_conv": l1_gdn_conv,
          "gdn_A_log_f": l1_gdn_A_log_f, "gdn_dt_bias_f": l1_gdn_dt_bias_f,
          "gdn_A_log_b": l1_gdn_A_log_b, "gdn_dt_bias_b": l1_gdn_dt_bias_b,
          "gdn_norm_g": l1_gdn_norm_g}
    d = x_prompt.shape[-1]
    nb = c.shape[0]
    w0 = _layer0_weights(p0)
    w1 = _layer1_weights(p1)
    cos, sin = _rope_tables(x_sample.shape[1])
    w0["cos"], w0["sin"] = cos, sin

    cvec = jnp.concatenate([c, c_ctx[None, :], jnp.zeros((SUBLANES - nb - 1, d), F32)], axis=0)
    m0 = _modulation(cvec, l0_mod_w, l0_mod_b)
    m1 = _modulation(cvec, l1_mod_w, l1_mod_b)
    mods_lat = [_split_mod(m[:nb], d) for m in (m0, m1)]
    mods_ctx = [_split_mod(m[nb:nb + 1], d) for m in (m0, m1)]

    tm = 512
    y_prompt, new = _trunk(x_prompt, mods_ctx, w0, w1, None, tm, min(512, x_prompt.shape[1]))
    ctx = (state_l0_rwkv_fwd, state_l0_rwkv_bwd, cache_l0_k, cache_l0_v,
           state_l1_gla_fwd, state_l1_gla_bwd, state_l1_gdn_fwd, state_l1_gdn_bwd)
    y_sample, _ = _trunk(x_sample, mods_lat, w0, w1, ctx, tm, min(512, x_sample.shape[1]))
    return (y_prompt, y_sample) + tuple(new)
```

```python
import functools

import numpy as np
import jax
import jax.numpy as jnp
from jax import lax
from jax.experimental import pallas as pl
from jax.experimental.pallas import tpu as pltpu

F32 = jnp.float32
BF16 = jnp.bfloat16

EPS = 1e-6
GN_EPS = 64e-5
CHUNK = 64
GRID_W = 64
ROPE_THETA = 10000.0
RW_DECAY_SCALE = 0.6065306597126334
GLA_GATE_NORM = 16.0
LOG2E = 1.4426950408889634
HEAD64 = 64
HEAD128 = 128
LANES = 128
SUBLANES = 8
VMEM_LIMIT = 56 * 1024 * 1024


def _mm(a, b):
    return jnp.dot(a.astype(BF16), b.astype(BF16), preferred_element_type=F32)


def _mm_nt(a, b):
    return lax.dot_general(a.astype(BF16), b.astype(BF16), (((1,), (1,)), ((), ())),
                           preferred_element_type=F32)


def _mm_tn(a, b):
    return lax.dot_general(a.astype(BF16), b.astype(BF16), (((0,), (0,)), ((), ())),
                           preferred_element_type=F32)


def _split3(x):
    hi = x.astype(BF16)
    r1 = x - hi.astype(F32)
    mid = r1.astype(BF16)
    lo = (r1 - mid.astype(F32)).astype(BF16)
    return hi, mid, lo


def _mm_exact_l(mask_bf16, x):
    hi, mid, lo = _split3(x)
    d = functools.partial(jnp.dot, preferred_element_type=F32)
    return d(mask_bf16, hi) + d(mask_bf16, mid) + d(mask_bf16, lo)


def _mm_exact_tn(x, mask_bf16):
    hi, mid, lo = _split3(x)
    d = lambda a: lax.dot_general(a, mask_bf16, (((0,), (0,)), ((), ())), preferred_element_type=F32)
    return d(hi) + d(mid) + d(lo)


def _mm3(a, b):
    ah = a.astype(BF16)
    al = (a - ah.astype(F32)).astype(BF16)
    bh = b.astype(BF16)
    bl = (b - bh.astype(F32)).astype(BF16)
    d = functools.partial(jnp.dot, preferred_element_type=F32)
    return d(ah, bh) + d(ah, bl) + d(al, bh)


def _silu(x):
    return x * jax.nn.sigmoid(x)


def _softplus(x):
    return jnp.maximum(x, 0.0) + jnp.log(1.0 + jnp.exp(-jnp.abs(x)))


def _log_sigmoid(x):
    return jnp.minimum(x, 0.0) - jnp.log(1.0 + jnp.exp(-jnp.abs(x)))


def _order_masks(n, reverse, reps=1):
    row = lax.broadcasted_iota(jnp.int32, (reps * n, n), 0) & (n - 1)
    col = lax.broadcasted_iota(jnp.int32, (reps * n, n), 1)
    d = (col - row) if reverse else (row - col)
    return d > 0, d >= 0


def _tri_inv_many(nmats, mm):
    n = nmats[0].shape[0]
    row = lax.broadcasted_iota(jnp.int32, (n, n), 0)
    col = lax.broadcasted_iota(jnp.int32, (n, n), 1)
    x = row ^ col
    eye = jnp.where(row == col, 1.0, 0.0).astype(F32)
    ts = [eye - jnp.where(x == 1, nm, 0.0) for nm in nmats]
    s = 1
    while (2 << s) <= n:
        lvl = (x >> s) == 1
        tcs = [mm(t, jnp.where(lvl, nm, 0.0)) for t, nm in zip(ts, nmats)]
        ts = [t - mm(tc, t) for t, tc in zip(ts, tcs)]
        s += 1
    return ts


def _tri_inv_pairs(npairs, mm):
    n = npairs[0].shape[0]
    row = lax.broadcasted_iota(jnp.int32, (n, 2 * n), 0)
    col = lax.broadcasted_iota(jnp.int32, (n, 2 * n), 1) & (n - 1)
    x = row ^ col
    row2 = lax.broadcasted_iota(jnp.int32, (2 * n, 2 * n), 0)
    col2 = lax.broadcasted_iota(jnp.int32, (2 * n, 2 * n), 1)
    same = (row2 >= n) == (col2 >= n)
    blockdiag = lambda a: jnp.where(same, jnp.concatenate([a, a], axis=0), 0.0)
    eye = jnp.where(x == 0, 1.0, 0.0).astype(F32)
    ts = [eye - jnp.where(x == 1, nm, 0.0) for nm in npairs]
    s = 1
    while (2 << s) <= n:
        lvl = (x >> s) == 1
        tcs = [mm(t, blockdiag(jnp.where(lvl, nm, 0.0))) for t, nm in zip(ts, npairs)]
        ts = [t - mm(tc, blockdiag(t)) for t, tc in zip(ts, tcs)]
        s += 1
    return ts


def _lane_group_masks(width, group, count):
    lane = lax.broadcasted_iota(jnp.int32, (1, width), 1)
    return [((lane >= g * group) & (lane < (g + 1) * group)) for g in range(count)]


def _stack_heads(x, masks):
    return jnp.concatenate([jnp.where(m, x, 0.0) for m in masks], axis=0)


def _mod_kernel(c_ref, w_ref, b_ref, o_ref):
    s = _silu(c_ref[...])
    o_ref[...] = _mm3(s, w_ref[...]) + b_ref[...]


def _modulation(cvec, mod_w, mod_b):
    rows, d = cvec.shape
    n = mod_w.shape[1]
    tn = 1024
    return pl.pallas_call(
        _mod_kernel,
        out_shape=jax.ShapeDtypeStruct((rows, n), F32),
        grid=(n // tn,),
        in_specs=[pl.BlockSpec((rows, d), lambda j: (0, 0)),
                  pl.BlockSpec((d, tn), lambda j: (0, j)),
                  pl.BlockSpec((1, tn), lambda j: (0, j))],
        out_specs=pl.BlockSpec((rows, tn), lambda j: (0, j)),
        compiler_params=pltpu.CompilerParams(vmem_limit_bytes=VMEM_LIMIT),
        name="modulation",
    )(cvec, mod_w, mod_b.reshape(1, n))


def _prenorm(x, g, scale1p, shift):
    ms = jnp.mean(x * x, axis=-1, keepdims=True)
    return x * lax.rsqrt(ms + EPS) * g * scale1p + shift


def _seg_sum(x, bd_ref):
    hi = x.astype(BF16)
    lo = (x - hi.astype(F32)).astype(BF16)
    bd = bd_ref[...]
    return jnp.dot(hi, bd, preferred_element_type=F32) + jnp.dot(lo, bd, preferred_element_type=F32)


def _rope(x, cos, sin_signed):
    w = x.shape[-1]
    lane = lax.broadcasted_iota(jnp.int32, (1, w), 1)
    first = (lane & 31) < 16
    partner = jnp.where(first, pltpu.roll(x, w - 16, axis=1), pltpu.roll(x, 16, axis=1))
    return x * cos + partner * sin_signed


def _tile_lanes(x, reps):
    return jnp.concatenate([x] * reps, axis=-1) if reps > 1 else x


P0_G_PRE, P0_G_POST, P0_MU, P0_W0, P0_A0, P0_KK, P0_KA, P0_RK, P0_GNG, P0_GNB, P0_GQ, P0_GK = range(12)
W0_COLS = (0, 1792, 2304, 2816, 2944, 3072, 3584)


def _prow(p_ref, i, n):
    return p_ref[i:i + 1, 0:n]


def _in0_kernel(use_rope, x_ref, sc_ref, sh_ref, p_ref, w_ref, bdq_ref, bdk_ref, cos_ref, sin_ref,
                slab_ref, rg_ref, q_ref, k_ref, v_ref, ag_ref):
    dm = x_ref.shape[-1]
    h = _prenorm(x_ref[0], _prow(p_ref, P0_G_PRE, dm), sc_ref[0], sh_ref[0]).astype(BF16)
    c = W0_COLS
    proj = lambda j: jnp.dot(h, w_ref[:, c[j]:c[j + 1]], preferred_element_type=F32)
    slab_ref[0] = proj(0)
    rg_ref[0] = proj(1)
    v_ref[0] = proj(4)
    ag_ref[0] = proj(5)
    q = proj(2)
    k = proj(3)
    q = q * lax.rsqrt(_seg_sum(q * q, bdq_ref) * (1.0 / HEAD64) + EPS) * _prow(p_ref, P0_GQ, q.shape[-1])
    k = k * lax.rsqrt(_seg_sum(k * k, bdk_ref) * (1.0 / HEAD64) + EPS) * _prow(p_ref, P0_GK, k.shape[-1])
    if use_rope:
        cos = cos_ref[...]
        sin = sin_ref[...]
        q = _rope(q, _tile_lanes(cos, q.shape[-1] // LANES), _tile_lanes(sin, q.shape[-1] // LANES))
        k = _rope(k, cos, sin)
    q_ref[0] = q
    k_ref[0] = k


def _const_spec(shape, single=False):
    nd = len(shape)
    if single:
        return pl.BlockSpec(shape, lambda *_: (0,) * nd, pipeline_mode=pl.Buffered(1))
    return pl.BlockSpec(shape, lambda *_: (0,) * nd)


def _mod_spec(arr, d):
    if arr.shape[0] == 1:
        return pl.BlockSpec((1, 1, d), lambda b, i: (0, 0, 0))
    return pl.BlockSpec((1, 1, d), lambda b, i: (b, 0, 0))


def _in0(x, scale1p, shift, w, use_rope, tm):
    bsz, t, d = x.shape
    widths = [W0_COLS[j + 1] - W0_COLS[j] for j in range(6)]
    outs = [jax.ShapeDtypeStruct((bsz, t, n), F32) for n in widths]
    tok = lambda n: pl.BlockSpec((1, tm, n), lambda b, i: (b, i, 0))
    in_specs = [tok(d), _mod_spec(scale1p, d), _mod_spec(shift, d), _const_spec(w["p"].shape),
                _const_spec(w["win"].shape, single=True), _const_spec(w["bdq"].shape), _const_spec(w["bdk"].shape)]
    in_specs += [pl.BlockSpec((tm, LANES), lambda b, i: (i, 0))] * 2
    return pl.pallas_call(
        functools.partial(_in0_kernel, use_rope),
        out_shape=outs,
        grid=(bsz, t // tm),
        in_specs=in_specs,
        out_specs=[tok(n) for n in widths],
        compiler_params=pltpu.CompilerParams(dimension_semantics=("parallel", "parallel"),
                                             vmem_limit_bytes=VMEM_LIMIT),
        name="in_proj0",
    )(x, scale1p, shift, w["p"], w["win"], w["bdq"], w["bdk"], w["cos"], w["sin"])


def _neighbours(x, prev_blk, next_blk, i, n_blocks):
    tm = x.shape[0]
    row = lax.broadcasted_iota(jnp.int32, (tm, 1), 0)
    prev_row = jnp.where(i == 0, 0.0, prev_blk[SUBLANES - 1:SUBLANES, :])
    next_row = jnp.where(i == n_blocks - 1, 0.0, next_blk[0:1, :])
    prev = jnp.where(row == 0, prev_row, pltpu.roll(x, 1, axis=0))
    nxt = jnp.where(row == tm - 1, next_row, pltpu.roll(x, tm - 1, axis=0))
    return prev, nxt


def _halo_specs(tm, c, t):
    r = tm // SUBLANES
    last = t // SUBLANES - 1
    main = pl.BlockSpec((1, tm, c), lambda b, i: (b, i, 0))
    prev = pl.BlockSpec((1, SUBLANES, c), lambda b, i: (b, jnp.maximum(i * r - 1, 0), 0))
    nxt = pl.BlockSpec((1, SUBLANES, c), lambda b, i: (b, jnp.minimum((i + 1) * r, last), 0))
    return [main, prev, nxt]


def _rwprep_kernel(n_blocks, s_ref, sp_ref, sn_ref, p_ref, w2_ref, a2_ref, bd_ref,
                   r_ref, k_ref, v_ref, kkn_ref, b_ref, lw_ref):
    i = pl.program_id(1)
    x = s_ref[0]
    prev, nxt = _neighbours(x, sp_ref[0], sn_ref[0], i, n_blocks)
    xs = x + _prow(p_ref, P0_MU, x.shape[-1]) * (0.5 * (prev + nxt) - x)
    c = 512
    r, kr, vr = xs[:, :c], xs[:, c:2 * c], xs[:, 2 * c:3 * c]
    lw_in = xs[:, 3 * c:3 * c + LANES]
    la_in = xs[:, 3 * c + LANES:3 * c + 2 * LANES]
    lor = _mm(jnp.tanh(lw_in), w2_ref[...]) + _prow(p_ref, P0_W0, 2 * c)
    logw = -RW_DECAY_SCALE * jax.nn.sigmoid(lor)
    a = jax.nn.sigmoid(_prow(p_ref, P0_A0, c) + _mm(la_in, a2_ref[...]))
    kkp = kr * _prow(p_ref, P0_KK, c)
    kkn = kkp * lax.rsqrt(_seg_sum(kkp * kkp, bd_ref) + EPS)
    r_ref[0] = r
    k_ref[0] = kr * (1.0 + (a - 1.0) * _prow(p_ref, P0_KA, c))
    v_ref[0] = vr
    kkn_ref[0] = kkn
    b_ref[0] = kkn * a
    lw_ref[0, 0] = logw[:, :c]
    lw_ref[1, 0] = logw[:, c:]


def _rwprep(slab, w, tm):
    bsz, t, cs = slab.shape
    c = 512
    nb = t // tm
    tok = pl.BlockSpec((1, tm, c), lambda b, i: (b, i, 0))
    outs = [jax.ShapeDtypeStruct((bsz, t, c), F32)] * 5 + [jax.ShapeDtypeStruct((2, bsz, t, c), F32)]
    names = ("p", "w2", "a2", "bd")
    return pl.pallas_call(
        functools.partial(_rwprep_kernel, nb),
        out_shape=outs,
        grid=(bsz, nb),
        in_specs=_halo_specs(tm, cs, t) + [_const_spec(w[k].shape) for k in names],
        out_specs=[tok] * 5 + [pl.BlockSpec((2, 1, tm, c), lambda b, i: (0, b, i, 0))],
        compiler_params=pltpu.CompilerParams(dimension_semantics=("parallel", "parallel"),
                                             vmem_limit_bytes=VMEM_LIMIT),
        name="rwkv_prep",
    )(slab, slab, slab, *[w[k] for k in names])


QUAD = 4 * HEAD64


RW_SUB = 4
GDN_SUB = 4
GLA_SUB = 4


def _rwscan_kernel(nc, nsub, rf, kf, vf, kkf, bf, lwf, rb, kb, vb, kkb, bb, lwb, s0_ref,
                   of_ref, ob_ref, sf_ref, st_ref):
    c = pl.program_id(1)
    L = CHUNK
    nq = rf.shape[-1] // QUAD

    @pl.when(c == 0)
    def _():
        st_ref[...] = s0_ref[0]

    hmask = _lane_group_masks(QUAD, HEAD64, 4)
    rowq = lax.broadcasted_iota(jnp.int32, (QUAD, QUAD), 0)
    colq = lax.broadcasted_iota(jnp.int32, (QUAD, QUAD), 1)
    bd = (rowq >> 6) == (colq >> 6)
    masks = []
    rowc = lax.broadcasted_iota(jnp.int32, (L, QUAD), 0)
    colc = lax.broadcasted_iota(jnp.int32, (L, QUAD), 1) & (L - 1)
    for rev in (False, True):
        _, incl = _order_masks(L, rev)
        dlt = (colc - rowc) if rev else (rowc - colc)
        masks.append((jnp.where(incl, 1.0, 0.0).astype(BF16), dlt > 0, dlt >= 0))
    refs = ((rf, kf, vf, kkf, bf, lwf), (rb, kb, vb, kkb, bb, lwb))

    rows = lambda d, s: slice(s * L, (s + 1) * L) if d == 0 else slice((nsub - 1 - s) * L, (nsub - s) * L)
    units = [(s, d, qd) for s in range(nsub) for d in range(2) for qd in range(nq)]

    dat = []
    for s, d, qd in units:
        sl = slice(qd * QUAD, (qd + 1) * QUAD)
        rw = rows(d, s)
        r_, k_, v_, kk_, b_, lw_ = refs[d]
        dat.append(dict(r=r_[0, rw, sl], k=k_[0, rw, sl], v=v_[0, rw, sl], kk=kk_[0, rw, sl], b=b_[0, rw, sl],
                        lw=lw_[0, 0, rw, sl]))
    cws = [_mm_exact_l(masks[d][0], x["lw"]) for (_, d, _), x in zip(units, dat)]
    for x, cw in zip(dat, cws):
        c0 = cw[L // 2:L // 2 + 1, :]
        tot = jnp.sum(x["lw"], axis=0, keepdims=True)
        e_rel = jnp.exp(cw - c0)
        e_inv = jnp.exp(c0 - cw)
        ec0 = jnp.exp(c0)
        e_fin = jnp.exp(tot - c0)
        a_rel = x["kk"] * e_rel * jnp.exp(-x["lw"])
        r_rel = x["r"] * e_rel
        b_rel = x["b"] * e_inv
        k_rel = x["k"] * e_inv
        x.update(ar_abs=jnp.concatenate([a_rel * ec0, r_rel * ec0], axis=0),
                 ar_rel=jnp.concatenate([a_rel, r_rel], axis=0),
                 sbk=jnp.concatenate([_stack_heads(b_rel, hmask), _stack_heads(k_rel, hmask)], axis=0),
                 bk_fin=jnp.concatenate([b_rel * e_fin, k_rel * e_fin], axis=0), w_tot=jnp.exp(tot),
                 sv=_stack_heads(x["v"], hmask))
    for (_, d, _), x in zip(units, dat):
        _, strict_c, incl_c = masks[d]
        xx = _mm_nt(x["ar_rel"], x["sbk"])
        x["n_c"] = jnp.where(strict_c, xx[0:L, 0:QUAD], 0.0)
        x["mp"] = jnp.concatenate([jnp.where(strict_c, xx[0:L, QUAD:], 0.0),
                                   jnp.where(incl_c, xx[L:, QUAD:], 0.0)], axis=0)
        x["p_rb"] = jnp.where(incl_c, xx[L:, 0:QUAD], 0.0)
    tinv = _tri_inv_pairs([x["n_c"][:, p * 2 * L:(p + 1) * 2 * L] for x in dat for p in range(2)], _mm)
    for i, x in enumerate(dat):
        x["t_c"] = jnp.concatenate(tinv[2 * i:2 * i + 2], axis=-1)
        x["mpv"] = _mm(x["mp"], x["sv"])

    state = {(d, qd): st_ref[d, qd] for d in range(2) for qd in range(nq)}
    for s in range(nsub):
        cur = [(d, qd, dat[(s * 2 + d) * nq + qd]) for d in range(2) for qd in range(nq)]
        for d, qd, x in cur:
            x["ars"] = _mm_nt(x["ar_abs"], state[d, qd])
        for d, qd, x in cur:
            rhs = -x["ars"][0:L] - x["mpv"][0:L]
            x["e"] = _mm(x["t_c"], _stack_heads(rhs, hmask))
        for d, qd, x in cur:
            x["o"] = x["ars"][L:2 * L] + x["mpv"][L:2 * L] + _mm(x["p_rb"], _stack_heads(x["e"], hmask))
            ev = jnp.concatenate([x["e"], x["v"]], axis=0)
            state[d, qd] = state[d, qd] * x["w_tot"] + jnp.where(bd, _mm_tn(ev, x["bk_fin"]), 0.0)
        for d, o_ref in ((0, of_ref), (1, ob_ref)):
            o_ref[0, rows(d, s), :] = jnp.concatenate([x["o"] for dd, _, x in cur if dd == d], axis=-1)
    for (d, qd), val in state.items():
        st_ref[d, qd] = val

    @pl.when(c == nc - 1)
    def _():
        sf_ref[0] = st_ref[...]


def _dir_specs(block, nc, lead=None):
    specs = []
    for d in range(2):
        idx = (lambda b_, c: c) if d == 0 else (lambda b_, c: nc - 1 - c)
        if lead is None:
            specs.append(pl.BlockSpec((1,) + block, functools.partial(
                lambda f, b_, c: (b_, f(b_, c)) + (0,) * (len(block) - 1), idx)))
        else:
            specs.append(pl.BlockSpec((1, 1) + block, functools.partial(
                lambda f, dd, b_, c: (dd, b_, f(b_, c)) + (0,) * (len(block) - 1), idx, d)))
    return specs


def _rwscan(r, k, v, kk, b, lw, s0):
    bsz, t, cdim = r.shape
    blk = RW_SUB * CHUNK
    nc = t // blk
    nq = cdim // QUAD
    tf, tb = _dir_specs((blk, cdim), nc)
    lf, lb = _dir_specs((blk, cdim), nc, lead=True)
    sspec = pl.BlockSpec((1, 2, nq, QUAD, QUAD), lambda b_, c: (b_, 0, 0, 0, 0))
    return pl.pallas_call(
        functools.partial(_rwscan_kernel, nc, RW_SUB),
        out_shape=[jax.ShapeDtypeStruct((bsz, t, cdim), F32)] * 2
                  + [jax.ShapeDtypeStruct((bsz, 2, nq, QUAD, QUAD), F32)],
        grid=(bsz, nc),
        in_specs=[tf] * 5 + [lf] + [tb] * 5 + [lb] + [sspec],
        out_specs=[tf, tb, sspec],
        scratch_shapes=[pltpu.VMEM((2, nq, QUAD, QUAD), F32)],
        compiler_params=pltpu.CompilerParams(dimension_semantics=("parallel", "arbitrary"),
                                             vmem_limit_bytes=VMEM_LIMIT),
        name="rwkv_scan",
    )(r, k, v, kk, b, lw, r, k, v, kk, b, lw, s0)


def _attn_kernel(n_kv, q_ref, k_ref, vt_ref, o_ref):
    n_heads = q_ref.shape[-1] // HEAD64
    group = n_heads // n_kv
    lane = lax.broadcasted_iota(jnp.int32, (1, LANES), 1)
    kmat = k_ref[0]
    ones = vt_ref[0, n_kv * HEAD64:, :]
    vts = [jnp.concatenate([vt_ref[0, kv * HEAD64:(kv + 1) * HEAD64, :], ones], axis=0) for kv in range(n_kv)]
    def scores(h):
        kv = h // group
        pair = q_ref[0, :, (h // 2) * LANES:(h // 2 + 1) * LANES] * (HEAD64 ** -0.5 * LOG2E)
        if (h % 2) != kv:
            pair = pltpu.roll(pair, HEAD64, axis=1)
        qh = jnp.where((lane >= kv * HEAD64) & (lane < (kv + 1) * HEAD64), pair, 0.0).astype(BF16)
        return lax.dot_general(kmat, qh, (((1,), (1,)), ((), ())), preferred_element_type=F32).astype(BF16)

    outs = []
    ahead = 1
    pending = [scores(h) for h in range(min(ahead, n_heads))]
    for h in range(n_heads):
        st = pending.pop(0)
        if h + ahead < n_heads:
            pending.append(scores(h + ahead))
        m = jnp.max(st, axis=0, keepdims=True)
        pt = jnp.exp2(st - m)
        res = jnp.dot(vts[h // group], pt, preferred_element_type=F32)
        outs.append(res[0:HEAD64] / res[HEAD64:HEAD64 + 1])
    o_ref[0] = jnp.concatenate(outs, axis=0).T


def _attention(q, k_all, vt_all, tq):
    bsz, t, c = q.shape
    tk = k_all.shape[1]
    n_kv = k_all.shape[-1] // HEAD64
    assert n_kv * HEAD64 == LANES
    return pl.pallas_call(
        functools.partial(_attn_kernel, n_kv),
        out_shape=jax.ShapeDtypeStruct((bsz, t, c), F32),
        grid=(bsz, t // tq),
        in_specs=[pl.BlockSpec((1, tq, c), lambda b, i: (b, i, 0)),
                  pl.BlockSpec((1, tk, LANES), lambda b, i: (b, 0, 0), pipeline_mode=pl.Buffered(1)),
                  pl.BlockSpec((1, vt_all.shape[1], tk), lambda b, i: (b, 0, 0), pipeline_mode=pl.Buffered(1))],
        out_specs=pl.BlockSpec((1, tq, c), lambda b, i: (b, i, 0)),
        compiler_params=pltpu.CompilerParams(dimension_semantics=("parallel", "parallel"),
                                             vmem_limit_bytes=VMEM_LIMIT),
        name="attention",
    )(q, k_all, vt_all)


def _postnorm_residual(x, out, g_post, gate):
    ms = jnp.mean(out * out, axis=-1, keepdims=True)
    return x + gate * (out * lax.rsqrt(ms + EPS) * g_post)


def _out0_kernel(x_ref, gate_ref, of_ref, ob_ref, r_ref, k_ref, v_ref, rg_ref, oat_ref, ag_ref,
                 p_ref, bd_ref, wo_ref, y_ref):
    o = of_ref[0] + ob_ref[0]
    c = o.shape[-1]
    inv = 1.0 / HEAD64
    mu = _seg_sum(o, bd_ref) * inv
    dlt = o - mu
    var = _seg_sum(dlt * dlt, bd_ref) * inv
    gn = dlt * lax.rsqrt(var + GN_EPS) * _prow(p_ref, P0_GNG, c) + _prow(p_ref, P0_GNB, c)
    bonus = _seg_sum(r_ref[0] * k_ref[0] * _prow(p_ref, P0_RK, c), bd_ref) * v_ref[0]
    o_rw = (gn + bonus) * _silu(rg_ref[0])
    o_at = oat_ref[0] * _silu(ag_ref[0])
    out = _mm(o_rw, wo_ref[0:c, :]) + _mm(o_at, wo_ref[c:2 * c, :])
    y_ref[0] = _postnorm_residual(x_ref[0], out, _prow(p_ref, P0_G_POST, x_ref.shape[-1]), gate_ref[0])


def _out0(x, gate, o_f, o_b, r, k, v, rg, o_at, ag, w, tm):
    bsz, t, d = x.shape
    c = 512
    tokd = pl.BlockSpec((1, tm, d), lambda b, i: (b, i, 0))
    tok = pl.BlockSpec((1, tm, c), lambda b, i: (b, i, 0))
    names = ("p", "bd", "wout")
    return pl.pallas_call(
        _out0_kernel,
        out_shape=jax.ShapeDtypeStruct((bsz, t, d), F32),
        grid=(bsz, t // tm),
        in_specs=[tokd, _mod_spec(gate, d)] + [tok] * 8
                 + [_const_spec(w[n].shape) for n in names],
        out_specs=tokd,
        compiler_params=pltpu.CompilerParams(dimension_semantics=("parallel", "parallel"),
                                             vmem_limit_bytes=VMEM_LIMIT),
        name="out_proj0",
    )(x, gate, o_f, o_b, r, k, v, rg, o_at, ag, *[w[n] for n in names])


(P1_G_PRE, P1_G_POST, P1_GB, P1_ALOG_F, P1_ALOG_B, P1_DTB_F, P1_DTB_B, P1_GLA_G, P1_GDN_G, P1_CONV) = range(10)
W1_COLS = (0, 256, 512, 1024, 1152, 1664, 3200, 3328, 3456, 3968)


def _in1_kernel(x_ref, sc_ref, sh_ref, p_ref, w_ref, g2_ref,
                q_ref, k_ref, v_ref, lw_ref, gg_ref, dqkv_ref, small_ref, dg_ref):
    dm = x_ref.shape[-1]
    h = _prenorm(x_ref[0], _prow(p_ref, P1_G_PRE, dm), sc_ref[0], sh_ref[0]).astype(BF16)
    c = W1_COLS
    proj = lambda j: jnp.dot(h, w_ref[:, c[j]:c[j + 1]], preferred_element_type=F32)
    q_ref[0] = proj(0) * (HEAD64 ** -0.5)
    k_ref[0] = proj(1)
    v_ref[0] = proj(2)
    gg_ref[0] = proj(4)
    dqkv_ref[0] = proj(5)
    small_ref[0, 0] = proj(6)
    small_ref[1, 0] = proj(7)
    dg_ref[0] = proj(8)
    lw = _log_sigmoid(_mm(proj(3), g2_ref[...]) + _prow(p_ref, P1_GB, g2_ref.shape[-1])) * (1.0 / GLA_GATE_NORM)
    half = lw.shape[-1] // 2
    lw_ref[0, 0] = lw[:, :half]
    lw_ref[1, 0] = lw[:, half:]


def _in1(x, scale1p, shift, w, tm):
    bsz, t, d = x.shape
    widths = [256, 256, 512, -256, 512, 1536, -LANES, 512]
    tok = lambda n: pl.BlockSpec((1, tm, n), lambda b, i: (b, i, 0))
    outs, ospecs = [], []
    for n in widths:
        if n < 0:
            outs.append(jax.ShapeDtypeStruct((2, bsz, t, -n), F32))
            ospecs.append(pl.BlockSpec((2, 1, tm, -n), lambda b, i: (0, b, i, 0)))
        else:
            outs.append(jax.ShapeDtypeStruct((bsz, t, n), F32))
            ospecs.append(tok(n))
    return pl.pallas_call(
        _in1_kernel,
        out_shape=outs,
        grid=(bsz, t // tm),
        in_specs=[tok(d), _mod_spec(scale1p, d), _mod_spec(shift, d), _const_spec(w["p"].shape),
                  _const_spec(w["win"].shape, single=True), _const_spec(w["g2"].shape)],
        out_specs=ospecs,
        compiler_params=pltpu.CompilerParams(dimension_semantics=("parallel", "parallel"),
                                             vmem_limit_bytes=VMEM_LIMIT),
        name="in_proj1",
    )(x, scale1p, shift, w["p"], w["win"], w["g2"])


def _gdnprep_kernel(n_blocks, x_ref, xp_ref, xn_ref, s_ref, p_ref, q_ref, k_ref, v_ref, g_ref):
    i = pl.program_id(1)
    x = x_ref[0]
    prev, nxt = _neighbours(x, xp_ref[0], xn_ref[0], i, n_blocks)
    cw = lambda j: _prow(p_ref, P1_CONV + j, x.shape[-1])
    y = _silu(prev * cw(0) + x * cw(1) + nxt * cw(2))
    c = 512
    qs, ks = [], []
    for h in range(c // HEAD128):
        qh = y[:, h * HEAD128:(h + 1) * HEAD128]
        kh = y[:, c + h * HEAD128:c + (h + 1) * HEAD128]
        qs.append(qh * lax.rsqrt(jnp.sum(qh * qh, axis=-1, keepdims=True) + EPS) * (HEAD128 ** -0.5))
        ks.append(kh * lax.rsqrt(jnp.sum(kh * kh, axis=-1, keepdims=True) + EPS))
    q_ref[0] = jnp.concatenate(qs, axis=-1)
    k_ref[0] = jnp.concatenate(ks, axis=-1)
    v_ref[0] = y[:, 2 * c:]
    lane = lax.broadcasted_iota(jnp.int32, (1, LANES), 1)
    nh = c // HEAD128
    for d, (ia, ib) in enumerate(((P1_ALOG_F, P1_DTB_F), (P1_ALOG_B, P1_DTB_B))):
        s = s_ref[d, 0]
        loga = -jnp.exp(_prow(p_ref, ia, LANES)) * _softplus(s + _prow(p_ref, ib, LANES))
        g_ref[d, 0] = jnp.where(lane < nh, loga, jnp.where(lane < 2 * nh, jax.nn.sigmoid(s), 0.0))


def _gdnprep(dqkv, small, w, tm):
    bsz, t, cs = dqkv.shape
    c = 512
    nb = t // tm
    tok = pl.BlockSpec((1, tm, c), lambda b, i: (b, i, 0))
    tokl = pl.BlockSpec((2, 1, tm, LANES), lambda b, i: (0, b, i, 0))
    return pl.pallas_call(
        functools.partial(_gdnprep_kernel, nb),
        out_shape=[jax.ShapeDtypeStruct((bsz, t, c), F32)] * 3 + [jax.ShapeDtypeStruct((2, bsz, t, LANES), F32)],
        grid=(bsz, nb),
        in_specs=_halo_specs(tm, cs, t) + [tokl, _const_spec(w["p"].shape)],
        out_specs=[tok] * 3 + [tokl],
        compiler_params=pltpu.CompilerParams(dimension_semantics=("parallel", "parallel"),
                                             vmem_limit_bytes=VMEM_LIMIT),
        name="gdn_prep",
    )(dqkv, dqkv, dqkv, small, w["p"])


def _glascan_kernel(nc, nsub, qf, kf, vf, lwf, qb, kb, vb, lwb, s0_ref, of_ref, ob_ref, sf_ref, st_ref):
    c = pl.program_id(1)
    L = CHUNK
    nh = qf.shape[-1] // HEAD64

    @pl.when(c == 0)
    def _():
        st_ref[...] = s0_ref[0]

    hmask = _lane_group_masks(nh * HEAD64, HEAD64, nh)
    rowv = lax.broadcasted_iota(jnp.int32, st_ref.shape[1:], 0)
    colk = lax.broadcasted_iota(jnp.int32, st_ref.shape[1:], 1)
    bd = (rowv >> 7) == (colk >> 6)
    refs = ((qf, kf, vf, lwf, of_ref), (qb, kb, vb, lwb, ob_ref))
    rows = lambda d, s: slice(s * L, (s + 1) * L) if d == 0 else slice((nsub - 1 - s) * L, (nsub - s) * L)
    masks = []
    for d in range(2):
        _, incl = _order_masks(L, d == 1)
        _, incl4 = _order_masks(L, d == 1, reps=nh)
        masks.append((jnp.where(incl, 1.0, 0.0).astype(BF16), incl4))

    units = [(s, d) for s in range(nsub) for d in range(2)]
    dat = []
    for s, d in units:
        q_, k_, v_, lw_, _ = refs[d]
        rw = rows(d, s)
        dat.append(dict(q=q_[0, rw, :], k=k_[0, rw, :], v=v_[0, rw, :], lw=lw_[0, 0, rw, :]))
    cws = [_mm_exact_l(masks[d][0], x["lw"]) for (_, d), x in zip(units, dat)]
    for x, cw in zip(dat, cws):
        c0 = cw[L // 2:L // 2 + 1, :]
        tot = jnp.sum(x["lw"], axis=0, keepdims=True)
        q_rel = x["q"] * jnp.exp(cw - c0)
        k_rel = x["k"] * jnp.exp(c0 - cw)
        x.update(sq=_stack_heads(q_rel, hmask), k_rel=k_rel, q_abs=q_rel * jnp.exp(c0),
                 k_fin=k_rel * jnp.exp(tot - c0), w_tot=jnp.exp(tot))
    for (_, d), x in zip(units, dat):
        x["att"] = jnp.where(masks[d][1], _mm_nt(x["sq"], x["k_rel"]), 0.0)
        x["vk"] = jnp.where(bd, _mm_tn(x["v"], x["k_fin"]), 0.0)
    for x in dat:
        x["intra"] = jnp.concatenate(
            [_mm(x["att"][h * L:(h + 1) * L], x["v"][:, h * HEAD128:(h + 1) * HEAD128]) for h in range(nh)], axis=-1)

    state = [st_ref[0], st_ref[1]]
    for s in range(nsub):
        for d in range(2):
            x = dat[s * 2 + d]
            refs[d][4][0, rows(d, s), :] = x["intra"] + _mm_nt(x["q_abs"], state[d])
            state[d] = state[d] * x["w_tot"] + x["vk"]
    st_ref[0] = state[0]
    st_ref[1] = state[1]

    @pl.when(c == nc - 1)
    def _():
        sf_ref[0] = st_ref[...]


def _glascan(q, k, v, lw, s0):
    bsz, t, ck = q.shape
    cv = v.shape[-1]
    blk = GLA_SUB * CHUNK
    nc = t // blk
    kf, kb = _dir_specs((blk, ck), nc)
    vf, vb = _dir_specs((blk, cv), nc)
    lf, lb = _dir_specs((blk, ck), nc, lead=True)
    sspec = pl.BlockSpec((1, 2, cv, ck), lambda b_, c: (b_, 0, 0, 0))
    return pl.pallas_call(
        functools.partial(_glascan_kernel, nc, GLA_SUB),
        out_shape=[jax.ShapeDtypeStruct((bsz, t, cv), F32)] * 2 + [jax.ShapeDtypeStruct((bsz, 2, cv, ck), F32)],
        grid=(bsz, nc),
        in_specs=[kf, kf, vf, lf, kb, kb, vb, lb, sspec],
        out_specs=[vf, vb, sspec],
        scratch_shapes=[pltpu.VMEM((2, cv, ck), F32)],
        compiler_params=pltpu.CompilerParams(dimension_semantics=("parallel", "arbitrary"),
                                             vmem_limit_bytes=VMEM_LIMIT),
        name="gla_scan",
    )(q, k, v, lw, q, k, v, lw, s0)


def _gdnscan_kernel(nc, nsub, qf, kf, vf, gcf, qb, kb, vb, gcb, s0_ref, of_ref, ob_ref, sf_ref, st_ref):
    c = pl.program_id(1)
    L = CHUNK
    nh = qf.shape[-1] // HEAD128

    @pl.when(c == 0)
    def _():
        st_ref[...] = s0_ref[0]

    refs = ((qf, kf, vf, gcf), (qb, kb, vb, gcb))
    rows = lambda d, s: slice(s * L, (s + 1) * L) if d == 0 else slice((nsub - 1 - s) * L, (nsub - s) * L)
    per_dir = {}
    for d in range(2):
        strict, incl = _order_masks(L, d == 1)
        _, incl_t = _order_masks(L, d == 0)
        incl_bf = jnp.where(incl, 1.0, 0.0).astype(BF16)
        incl_t_bf = jnp.where(incl_t, 1.0, 0.0).astype(BF16)
        for s in range(nsub):
            gcol = refs[d][3][0, 0, rows(d, s), :]
            per_dir[d, s] = dict(strict=strict, incl=incl, gcol=gcol,
                                 cum_c=_mm_exact_l(incl_bf, gcol),
                                 cum_r=_mm_exact_tn(gcol, incl_t_bf),
                                 tot=jnp.sum(gcol, axis=0, keepdims=True))

    units = [(s, d, h) for s in range(nsub) for d in range(2) for h in range(nh)]
    dat = []
    for s, d, h in units:
        hs = slice(h * HEAD128, (h + 1) * HEAD128)
        pd = per_dir[d, s]
        rw = rows(d, s)
        q, k, v = refs[d][0][0, rw, hs], refs[d][1][0, rw, hs], refs[d][2][0, rw, hs]
        g = pd["cum_c"][:, h:h + 1]
        beta = pd["gcol"][:, nh + h:nh + h + 1]
        g_last = pd["tot"][:, h:h + 1]
        gam = jnp.exp(g)
        dat.append(dict(q=q, k=k, qg=q * gam, beta=beta, pd=pd,
                        dec=jnp.exp(jnp.minimum(g - pd["cum_r"][h:h + 1, :], 0.0)),
                        rhs=jnp.concatenate([(beta * gam) * k, beta * v], axis=-1),
                        k_dec=k * jnp.exp(g_last - g), gl=jnp.exp(g_last)))
    for x in dat:
        x["kk"] = _mm_nt(x["k"], x["k"])
        x["aqk"] = jnp.where(x["pd"]["incl"], x["dec"] * _mm_nt(x["q"], x["k"]), 0.0)
    tinv = _tri_inv_many([jnp.where(x["pd"]["strict"], x["dec"] * x["kk"] * x["beta"], 0.0) for x in dat], _mm)
    for x, t in zip(dat, tinv):
        x["wu"] = _mm(t, x["rhs"])
        x["wq"] = jnp.concatenate([x["wu"][:, :HEAD128], x["qg"]], axis=0)

    state = {(d, h): st_ref[d, h] for d in range(2) for h in range(nh)}
    for s in range(nsub):
        cur = [(d, h, dat[(s * 2 + d) * nh + h]) for d in range(2) for h in range(nh)]
        for d, h, x in cur:
            x["ws"] = _mm(x["wq"], state[d, h])
        for d, h, x in cur:
            x["u"] = x["wu"][:, HEAD128:] - x["ws"][0:L]
        for d, h, x in cur:
            x["o"] = x["ws"][L:2 * L] + _mm(x["aqk"], x["u"])
            state[d, h] = state[d, h] * x["gl"] + _mm_tn(x["k_dec"], x["u"])
        for d, o_ref in ((0, of_ref), (1, ob_ref)):
            o_ref[0, rows(d, s), :] = jnp.concatenate([x["o"] for dd, _, x in cur if dd == d], axis=-1)
    for (d, h), val in state.items():
        st_ref[d, h] = val

    @pl.when(c == nc - 1)
    def _():
        sf_ref[0] = st_ref[...]


def _gdnscan(q, k, v, gcol, s0):
    bsz, t, cdim = q.shape
    nh = cdim // HEAD128
    blk = GDN_SUB * CHUNK
    nc = t // blk
    tf, tb = _dir_specs((blk, cdim), nc)
    gcf, gcb = _dir_specs((blk, LANES), nc, lead=True)
    sspec = pl.BlockSpec((1, 2, nh, HEAD128, HEAD128), lambda b_, c: (b_, 0, 0, 0, 0))
    return pl.pallas_call(
        functools.partial(_gdnscan_kernel, nc, GDN_SUB),
        out_shape=[jax.ShapeDtypeStruct((bsz, t, cdim), F32)] * 2
                  + [jax.ShapeDtypeStruct((bsz, 2, nh, HEAD128, HEAD128), F32)],
        grid=(bsz, nc),
        in_specs=[tf, tf, tf, gcf, tb, tb, tb, gcb, sspec],
        out_specs=[tf, tb, sspec],
        scratch_shapes=[pltpu.VMEM((2, nh, HEAD128, HEAD128), F32)],
        compiler_params=pltpu.CompilerParams(dimension_semantics=("parallel", "arbitrary"),
                                             vmem_limit_bytes=VMEM_LIMIT),
        name="gdn_scan",
    )(q, k, v, gcol, q, k, v, gcol, s0)


def _head_rmsnorm(o, g):
    parts = []
    for h in range(o.shape[-1] // HEAD128):
        oh = o[:, h * HEAD128:(h + 1) * HEAD128]
        parts.append(oh * lax.rsqrt(jnp.mean(oh * oh, axis=-1, keepdims=True) + EPS) * g)
    return jnp.concatenate(parts, axis=-1)


def _out1_kernel(x_ref, gate_ref, glf_ref, glb_ref, gdf_ref, gdb_ref, gg_ref, dg_ref, p_ref, wo_ref, y_ref):
    c = gg_ref.shape[-1]
    o_gla = _head_rmsnorm(glf_ref[0] + glb_ref[0], _prow(p_ref, P1_GLA_G, HEAD128)) * _silu(gg_ref[0])
    o_gdn = _head_rmsnorm(gdf_ref[0] + gdb_ref[0], _prow(p_ref, P1_GDN_G, HEAD128)) * _silu(dg_ref[0])
    out = _mm(o_gla, wo_ref[0:c, :]) + _mm(o_gdn, wo_ref[c:2 * c, :])
    y_ref[0] = _postnorm_residual(x_ref[0], out, _prow(p_ref, P1_G_POST, x_ref.shape[-1]), gate_ref[0])


def _out1(x, gate, gla_f, gla_b, gdn_f, gdn_b, gg, dg, w, tm):
    bsz, t, d = x.shape
    c = 512
    tokd = pl.BlockSpec((1, tm, d), lambda b, i: (b, i, 0))
    tok = pl.BlockSpec((1, tm, c), lambda b, i: (b, i, 0))
    names = ("p", "wout")
    return pl.pallas_call(
        _out1_kernel,
        out_shape=jax.ShapeDtypeStruct((bsz, t, d), F32),
        grid=(bsz, t // tm),
        in_specs=[tokd, _mod_spec(gate, d)] + [tok] * 6
                 + [_const_spec(w[n].shape) for n in names],
        out_specs=tokd,
        compiler_params=pltpu.CompilerParams(dimension_semantics=("parallel", "parallel"),
                                             vmem_limit_bytes=VMEM_LIMIT),
        name="out_proj1",
    )(x, gate, gla_f, gla_b, gdn_f, gdn_b, gg, dg, *[w[n] for n in names])


def _block_ones(n_groups, width):
    return jnp.asarray(np.kron(np.eye(n_groups, dtype=np.float32), np.ones((width, width), np.float32)), BF16)


def _rope_tables(t):
    rows = t // GRID_W
    row_id = np.repeat(np.arange(rows, dtype=np.float32), GRID_W)
    col_id = np.tile(np.arange(GRID_W, dtype=np.float32), rows)
    nf = HEAD64 // 4
    inv = np.float32(ROPE_THETA) ** (-np.arange(nf, dtype=np.float32) / np.float32(nf))
    ang_r = (row_id[:, None] * inv[None, :]).astype(np.float32)
    ang_c = (col_id[:, None] * inv[None, :]).astype(np.float32)
    cos = np.concatenate([np.cos(ang_r)] * 2 + [np.cos(ang_c)] * 2, axis=-1)
    sin = np.concatenate([-np.sin(ang_r), np.sin(ang_r), -np.sin(ang_c), np.sin(ang_c)], axis=-1)
    tile2 = lambda a: jnp.asarray(np.concatenate([a, a], axis=-1).astype(np.float32))
    return tile2(cos), tile2(sin)


def _pack_rows(rows, width):
    return jnp.stack([jnp.pad(r.reshape(-1).astype(F32), (0, width - r.size)) for r in rows])


def _layer0_weights(p):
    w_in = p["w_in"]
    rw_w = 512
    slab_w = 3 * rw_w + 192
    d = w_in.shape[0]
    w = {}
    w["win"] = jnp.concatenate([w_in[:, :slab_w], jnp.zeros((d, W0_COLS[1] - slab_w), F32), w_in[:, slab_w:]],
                               axis=1).astype(BF16)
    w["wout"] = p["w_out"].astype(BF16)
    w["bdq"] = _block_ones(8, HEAD64)
    w["bdk"] = _block_ones(2, HEAD64)
    w["bd"] = w["bdq"]
    w["p"] = _pack_rows([p["g_pre"], p["g_post"], p["rw_mu"], jnp.concatenate([p["rw_w0_f"], p["rw_w0_b"]]),
                         p["rw_a0"], p["rw_k_k"], p["rw_k_a"], p["rw_r_k"], p["rw_gn_g"], p["rw_gn_b"],
                         jnp.tile(p["at_gq"], 8), jnp.tile(p["at_gk"], 2)], W0_COLS[1])
    z = jnp.zeros((64, rw_w), F32)
    w["w2"] = jnp.concatenate([jnp.concatenate([p["rw_w2_f"], z], axis=1),
                               jnp.concatenate([z, p["rw_w2_b"]], axis=1)], axis=0).astype(BF16)
    w["a2"] = jnp.concatenate([p["rw_a2"], z], axis=0).astype(BF16)
    return w


def _layer1_weights(p):
    w_in = p["w_in"]
    d = w_in.shape[0]
    zc = lambda n: jnp.zeros((d, n), F32)
    af, ab, be = w_in[:, 3104:3108], w_in[:, 3108:3112], w_in[:, 3112:3116]
    w = {}
    w["win"] = jnp.concatenate([w_in[:, :1024], w_in[:, 1024:1056], zc(LANES - 32), w_in[:, 1056:3104],
                                af, be, zc(LANES - 8), ab, be, zc(LANES - 8), w_in[:, 3116:]],
                               axis=1).astype(BF16)
    w["wout"] = p["w_out"].astype(BF16)
    z = jnp.zeros((16, 256), F32)
    g2 = jnp.concatenate([jnp.concatenate([p["gla_g2_f"], z], axis=1),
                          jnp.concatenate([z, p["gla_g2_b"]], axis=1)], axis=0)
    w["g2"] = jnp.pad(g2, ((0, LANES - 32), (0, 0))).astype(BF16)
    conv = p["gdn_conv"]
    w["p"] = _pack_rows([p["g_pre"], p["g_post"], jnp.concatenate([p["gla_gb_f"], p["gla_gb_b"]]),
                         p["gdn_A_log_f"], p["gdn_A_log_b"], p["gdn_dt_bias_f"], p["gdn_dt_bias_b"],
                         p["gla_norm_g"], p["gdn_norm_g"], conv[0], conv[1], conv[2]], conv.shape[1])
    return w


def _rw_state_to_bd(s):
    bsz = s.shape[0]
    s = s.reshape(bsz, 2, 4, HEAD64, HEAD64)
    eye = jnp.eye(4, dtype=s.dtype)
    return jnp.einsum("bqhvk,hg->bqhvgk", s, eye).reshape(bsz, 2, QUAD, QUAD)


def _rw_state_from_bd(s):
    bsz = s.shape[0]
    s = s.reshape(bsz, 2, 4, HEAD64, 4, HEAD64)
    return jnp.stack([s[:, :, h, :, h, :] for h in range(4)], axis=2).reshape(bsz, 8, HEAD64, HEAD64)


def _gla_state_to_bd(s):
    bsz = s.shape[0]
    eye = jnp.eye(4, dtype=s.dtype)
    return jnp.einsum("bhkv,hg->bhvgk", s, eye).reshape(bsz, 4 * HEAD128, 4 * HEAD64)


def _gla_state_from_bd(s):
    bsz = s.shape[0]
    s = s.reshape(bsz, 4, HEAD128, 4, HEAD64)
    return jnp.stack([jnp.swapaxes(s[:, h, :, h, :], -1, -2) for h in range(4)], axis=1)


def _trunk(x, mods, w0, w1, ctx, tm, tq):
    bsz, t, dm = x.shape
    latent = ctx is not None
    ts = min(tm, t)
    shared = mods[0][0].shape[0] == 1
    flat = (lambda a: a.reshape(a.shape[:-3] + (1, bsz * t, a.shape[-1]))) if shared else (lambda a: a)
    unflat = lambda a: a.reshape(a.shape[:-3] + (bsz, t, a.shape[-1]))

    sc, sh, gt = mods[0]
    slab, rg, q, k, v, ag = [unflat(a) for a in _in0(flat(x), sc, sh, w0, latent, tm)]
    r, kmod, vr, kkn, bvec, lw = _rwprep(slab, w0, ts)
    if latent:
        s0f, s0b, k_ctx, v_ctx = ctx[:4]
        s0 = jnp.stack([_rw_state_to_bd(s0f), _rw_state_to_bd(s0b)], axis=1)
        k_all = jnp.concatenate([k_ctx.reshape(bsz, -1, 2 * HEAD64), k], axis=1)
        v_all = jnp.concatenate([v_ctx.reshape(bsz, -1, 2 * HEAD64), v], axis=1)
    else:
        s0 = jnp.zeros((bsz, 2, 2, QUAD, QUAD), F32)
        k_all, v_all = k, v
    o_rwf, o_rwb, s_rw = _rwscan(r, kmod, vr, kkn, bvec, lw, s0)
    vt_all = jnp.concatenate([jnp.swapaxes(v_all, 1, 2), jnp.ones((bsz, 16, v_all.shape[1]), F32)], axis=1)
    o_at = _attention(q, k_all.astype(BF16), vt_all.astype(BF16), tq)
    x1 = _out0(flat(x), gt, *[flat(a) for a in (o_rwf, o_rwb, r, kmod, vr, rg, o_at, ag)], w0, tm)

    sc, sh, gt = mods[1]
    gq, gk, gv, glw, gg, dqkv, small, dg = [unflat(a) for a in _in1(x1, sc, sh, w1, tm)]
    dq, dk, dv, gcol = _gdnprep(dqkv, small, w1, ts)
    if latent:
        sgf, sgb, sdf, sdb = ctx[4:]
        s0_gla = jnp.stack([_gla_state_to_bd(sgf), _gla_state_to_bd(sgb)], axis=1)
        s0_gdn = jnp.stack([sdf, sdb], axis=1)
    else:
        s0_gla = jnp.zeros((bsz, 2, 4 * HEAD128, 4 * HEAD64), F32)
        s0_gdn = jnp.zeros((bsz, 2, 4, HEAD128, HEAD128), F32)
    gla_f, gla_b, s_gla = _glascan(gq, gk, gv, glw, s0_gla)
    gdn_f, gdn_b, s_gdn = _gdnscan(dq, dk, dv, gcol, s0_gdn)
    y = unflat(_out1(x1, gt, *[flat(a) for a in (gla_f, gla_b, gdn_f, gdn_b, gg, dg)], w1, tm))

    new = None
    if not latent:
        new = (_rw_state_from_bd(s_rw[:, 0]), _rw_state_from_bd(s_rw[:, 1]),
               k.reshape(bsz, t, 2, HEAD64), v.reshape(bsz, t, 2, HEAD64),
               _gla_state_from_bd(s_gla[:, 0]), _gla_state_from_bd(s_gla[:, 1]), s_gdn[:, 0], s_gdn[:, 1])
    return y, new


def _split_mod(m, d):
    shift, scale, gate = m[..., :d], m[..., d:2 * d], m[..., 2 * d:]
    return (1.0 + scale)[:, None, :], shift[:, None, :], gate[:, None, :]


def kernel(x_prompt, x_sample, state_l0_rwkv_fwd, state_l0_rwkv_bwd, cache_l0_k, cache_l0_v, state_l1_gla_fwd, state_l1_gla_bwd, state_l1_gdn_fwd, state_l1_gdn_bwd, c, c_ctx, l0_mod_w, l0_mod_b, l0_g_pre, l0_g_post, l0_w_in, l0_w_out, l0_rw_mu, l0_rw_w0_f, l0_rw_w2_f, l0_rw_w0_b, l0_rw_w2_b, l0_rw_a0, l0_rw_a2, l0_rw_k_k, l0_rw_k_a, l0_rw_r_k, l0_rw_gn_g, l0_rw_gn_b, l0_at_gq, l0_at_gk, l1_mod_w, l1_mod_b, l1_g_pre, l1_g_post, l1_w_in, l1_w_out, l1_gla_g2_f, l1_gla_gb_f, l1_gla_g2_b, l1_gla_gb_b, l1_gla_norm_g, l1_gdn_conv, l1_gdn_A_log_f, l1_gdn_dt_bias_f, l1_gdn_A_log_b, l1_gdn_dt_bias_b, l1_gdn_norm_g):
    p0 = {"g_pre": l0_g_pre, "g_post": l0_g_post, "w_in": l0_w_in, "w_out": l0_w_out, "rw_mu": l0_rw_mu,
          "rw_w0_f": l0_rw_w0_f, "rw_w2_f": l0_rw_w2_f, "rw_w0_b": l0_rw_w0_b, "rw_w2_b": l0_rw_w2_b,
          "rw_a0": l0_rw_a0, "rw_a2": l0_rw_a2, "rw_k_k": l0_rw_k_k, "rw_k_a": l0_rw_k_a,
          "rw_r_k": l0_rw_r_k, "rw_gn_g": l0_rw_gn_g, "rw_gn_b": l0_rw_gn_b,
          "at_gq": l0_at_gq, "at_gk": l0_at_gk}
    p1 = {"g_pre": l1_g_pre, "g_post": l1_g_post, "w_in": l1_w_in, "w_out": l1_w_out,
          "gla_g2_f": l1_gla_g2_f, "gla_gb_f": l1_gla_gb_f, "gla_g2_b": l1_gla_g2_b,
          "gla_gb_b": l1_gla_gb_b, "gla_norm_g": l1_gla_norm_g, "gdn_conv": l1_gdn_conv,
          "gdn_A_log_f": l1_gdn_A_log_f, "gdn_dt_bias_f": l1_gdn_dt_bias_f,
          "gdn_A_log_b": l1_gdn_A_log_b, "gdn_dt_bias_b": l1_gdn_dt_bias_b,
          "gdn_norm_g": l1_gdn_norm_g}
    d = x_prompt.shape[-1]
    nb = c.shape[0]
    w0 = _layer0_weights(p0)
    w1 = _layer1_weights(p1)
    cos, sin = _rope_tables(x_sample.shape[1])
    w0["cos"], w0["sin"] = cos, sin

    cvec = jnp.concatenate([c, c_ctx[None, :], jnp.zeros((SUBLANES - nb - 1, d), F32)], axis=0)
    m0 = _modulation(cvec, l0_mod_w, l0_mod_b)
    m1 = _modulation(cvec, l1_mod_w, l1_mod_b)
    mods_lat = [_split_mod(m[:nb], d) for m in (m0, m1)]
    mods_ctx = [_split_mod(m[nb:nb + 1], d) for m in (m0, m1)]

    tm = 512
    y_prompt, new = _trunk(x_prompt, mods_ctx, w0, w1, None, tm, min(512, x_prompt.shape[1]))
    ctx = (state_l0_rwkv_fwd, state_l0_rwkv_bwd, cache_l0_k, cache_l0_v,
           state_l1_gla_fwd, state_l1_gla_bwd, state_l1_gdn_fwd, state_l1_gdn_bwd)
    y_sample, _ = _trunk(x_sample, mods_lat, w0, w1, ctx, tm, min(512, x_sample.shape[1]))
    return (y_prompt, y_sample) + tuple(new)
```

```python
import functools

import numpy as np
import jax
import jax.numpy as jnp
from jax import lax
from jax.experimental import pallas as pl
from jax.experimental.pallas import tpu as pltpu

F32 = jnp.float32
BF16 = jnp.bfloat16

EPS = 1e-6
GN_EPS = 64e-5
CHUNK = 64
GRID_W = 64
ROPE_THETA = 10000.0
RW_DECAY_SCALE = 0.6065306597126334
GLA_GATE_NORM = 16.0
LOG2E = 1.4426950408889634
HEAD64 = 64
HEAD128 = 128
LANES = 128
SUBLANES = 8
VMEM_LIMIT = 56 * 1024 * 1024


def _mm(a, b):
    return jnp.dot(a.astype(BF16), b.astype(BF16), preferred_element_type=F32)


def _mm_nt(a, b):
    return lax.dot_general(a.astype(BF16), b.astype(BF16), (((1,), (1,)), ((), ())),
                           preferred_element_type=F32)


def _mm_tn(a, b):
    return lax.dot_general(a.astype(BF16), b.astype(BF16), (((0,), (0,)), ((), ())),
                           preferred_element_type=F32)


def _split3(x):
    hi = x.astype(BF16)
    r1 = x - hi.astype(F32)
    mid = r1.astype(BF16)
    lo = (r1 - mid.astype(F32)).astype(BF16)
    return hi, mid, lo


def _mm_exact_l(mask_bf16, x):
    hi, mid, lo = _split3(x)
    d = functools.partial(jnp.dot, preferred_element_type=F32)
    return d(mask_bf16, hi) + d(mask_bf16, mid) + d(mask_bf16, lo)


def _mm_exact_tn(x, mask_bf16):
    hi, mid, lo = _split3(x)
    d = lambda a: lax.dot_general(a, mask_bf16, (((0,), (0,)), ((), ())), preferred_element_type=F32)
    return d(hi) + d(mid) + d(lo)


def _mm3(a, b):
    ah = a.astype(BF16)
    al = (a - ah.astype(F32)).astype(BF16)
    bh = b.astype(BF16)
    bl = (b - bh.astype(F32)).astype(BF16)
    d = functools.partial(jnp.dot, preferred_element_type=F32)
    return d(ah, bh) + d(ah, bl) + d(al, bh)


def _silu(x):
    return x * jax.nn.sigmoid(x)


def _softplus(x):
    return jnp.maximum(x, 0.0) + jnp.log(1.0 + jnp.exp(-jnp.abs(x)))


def _log_sigmoid(x):
    return jnp.minimum(x, 0.0) - jnp.log(1.0 + jnp.exp(-jnp.abs(x)))


def _order_masks(n, reverse, reps=1):
    row = lax.broadcasted_iota(jnp.int32, (reps * n, n), 0) & (n - 1)
    col = lax.broadcasted_iota(jnp.int32, (reps * n, n), 1)
    d = (col - row) if reverse else (row - col)
    return d > 0, d >= 0


def _tri_inv_many(nmats, mm):
    n = nmats[0].shape[0]
    row = lax.broadcasted_iota(jnp.int32, (n, n), 0)
    col = lax.broadcasted_iota(jnp.int32, (n, n), 1)
    x = row ^ col
    eye = jnp.where(row == col, 1.0, 0.0).astype(F32)
    ts = [eye - jnp.where(x == 1, nm, 0.0) for nm in nmats]
    s = 1
    while (2 << s) <= n:
        lvl = (x >> s) == 1
        tcs = [mm(t, jnp.where(lvl, nm, 0.0)) for t, nm in zip(ts, nmats)]
        ts = [t - mm(tc, t) for t, tc in zip(ts, tcs)]
        s += 1
    return ts


def _tri_inv_pairs(npairs, mm):
    n = npairs[0].shape[0]
    row = lax.broadcasted_iota(jnp.int32, (n, 2 * n), 0)
    col = lax.broadcasted_iota(jnp.int32, (n, 2 * n), 1) & (n - 1)
    x = row ^ col
    row2 = lax.broadcasted_iota(jnp.int32, (2 * n, 2 * n), 0)
    col2 = lax.broadcasted_iota(jnp.int32, (2 * n, 2 * n), 1)
    same = (row2 >= n) == (col2 >= n)
    blockdiag = lambda a: jnp.where(same, jnp.concatenate([a, a], axis=0), 0.0)
    eye = jnp.where(x == 0, 1.0, 0.0).astype(F32)
    ts = [eye - jnp.where(x == 1, nm, 0.0) for nm in npairs]
    s = 1
    while (2 << s) <= n:
        lvl = (x >> s) == 1
        tcs = [mm(t, blockdiag(jnp.where(lvl, nm, 0.0))) for t, nm in zip(ts, npairs)]
        ts = [t - mm(tc, blockdiag(t)) for t, tc in zip(ts, tcs)]
        s += 1
    return ts


def _lane_group_masks(width, group, count):
    lane = lax.broadcasted_iota(jnp.int32, (1, width), 1)
    return [((lane >= g * group) & (lane < (g + 1) * group)) for g in range(count)]


def _stack_heads(x, masks):
    return jnp.concatenate([jnp.where(m, x, 0.0) for m in masks], axis=0)


def _mod_kernel(c_ref, w_ref, b_ref, o_ref):
    s = _silu(c_ref[...])
    o_ref[...] = _mm3(s, w_ref[...]) + b_ref[...]


def _modulation(cvec, mod_w, mod_b):
    rows, d = cvec.shape
    n = mod_w.shape[1]
    tn = 1024
    return pl.pallas_call(
        _mod_kernel,
        out_shape=jax.ShapeDtypeStruct((rows, n), F32),
        grid=(n // tn,),
        in_specs=[pl.BlockSpec((rows, d), lambda j: (0, 0)),
                  pl.BlockSpec((d, tn), lambda j: (0, j)),
                  pl.BlockSpec((1, tn), lambda j: (0, j))],
        out_specs=pl.BlockSpec((rows, tn), lambda j: (0, j)),
        compiler_params=pltpu.CompilerParams(vmem_limit_bytes=VMEM_LIMIT),
        name="modulation",
    )(cvec, mod_w, mod_b.reshape(1, n))


def _prenorm(x, g, scale1p, shift):
    ms = jnp.mean(x * x, axis=-1, keepdims=True)
    return x * lax.rsqrt(ms + EPS) * g * scale1p + shift


def _seg_sum(x, bd_ref):
    hi = x.astype(BF16)
    lo = (x - hi.astype(F32)).astype(BF16)
    bd = bd_ref[...]
    return jnp.dot(hi, bd, preferred_element_type=F32) + jnp.dot(lo, bd, preferred_element_type=F32)


def _rope(x, cos, sin_signed):
    w = x.shape[-1]
    lane = lax.broadcasted_iota(jnp.int32, (1, w), 1)
    first = (lane & 31) < 16
    partner = jnp.where(first, pltpu.roll(x, w - 16, axis=1), pltpu.roll(x, 16, axis=1))
    return x * cos + partner * sin_signed


def _tile_lanes(x, reps):
    return jnp.concatenate([x] * reps, axis=-1) if reps > 1 else x


P0_G_PRE, P0_G_POST, P0_MU, P0_W0, P0_A0, P0_KK, P0_KA, P0_RK, P0_GNG, P0_GNB, P0_GQ, P0_GK = range(12)
W0_COLS = (0, 1792, 2304, 2816, 2944, 3072, 3584)


def _prow(p_ref, i, n):
    return p_ref[i:i + 1, 0:n]


def _in0_kernel(use_rope, x_ref, sc_ref, sh_ref, p_ref, w_ref, bdq_ref, bdk_ref, cos_ref, sin_ref,
                slab_ref, rg_ref, q_ref, k_ref, v_ref, ag_ref):
    dm = x_ref.shape[-1]
    h = _prenorm(x_ref[0], _prow(p_ref, P0_G_PRE, dm), sc_ref[0], sh_ref[0]).astype(BF16)
    c = W0_COLS
    proj = lambda j: jnp.dot(h, w_ref[:, c[j]:c[j + 1]], preferred_element_type=F32)
    slab_ref[0] = proj(0)
    rg_ref[0] = proj(1)
    v_ref[0] = proj(4)
    ag_ref[0] = proj(5)
    q = proj(2)
    k = proj(3)
    q = q * lax.rsqrt(_seg_sum(q * q, bdq_ref) * (1.0 / HEAD64) + EPS) * _prow(p_ref, P0_GQ, q.shape[-1])
    k = k * lax.rsqrt(_seg_sum(k * k, bdk_ref) * (1.0 / HEAD64) + EPS) * _prow(p_ref, P0_GK, k.shape[-1])
    if use_rope:
        cos = cos_ref[...]
        sin = sin_ref[...]
        q = _rope(q, _tile_lanes(cos, q.shape[-1] // LANES), _tile_lanes(sin, q.shape[-1] // LANES))
        k = _rope(k, cos, sin)
    q_ref[0] = q
    k_ref[0] = k


def _const_spec(shape, single=False):
    nd = len(shape)
    if single:
        return pl.BlockSpec(shape, lambda *_: (0,) * nd, pipeline_mode=pl.Buffered(1))
    return pl.BlockSpec(shape, lambda *_: (0,) * nd)


def _mod_spec(arr, d):
    if arr.shape[0] == 1:
        return pl.BlockSpec((1, 1, d), lambda b, i: (0, 0, 0))
    return pl.BlockSpec((1, 1, d), lambda b, i: (b, 0, 0))


def _in0(x, scale1p, shift, w, use_rope, tm):
    bsz, t, d = x.shape
    widths = [W0_COLS[j + 1] - W0_COLS[j] for j in range(6)]
    outs = [jax.ShapeDtypeStruct((bsz, t, n), F32) for n in widths]
    tok = lambda n: pl.BlockSpec((1, tm, n), lambda b, i: (b, i, 0))
    in_specs = [tok(d), _mod_spec(scale1p, d), _mod_spec(shift, d), _const_spec(w["p"].shape),
                _const_spec(w["win"].shape, single=True), _const_spec(w["bdq"].shape), _const_spec(w["bdk"].shape)]
    in_specs += [pl.BlockSpec((tm, LANES), lambda b, i: (i, 0))] * 2
    return pl.pallas_call(
        functools.partial(_in0_kernel, use_rope),
        out_shape=outs,
        grid=(bsz, t // tm),
        in_specs=in_specs,
        out_specs=[tok(n) for n in widths],
        compiler_params=pltpu.CompilerParams(dimension_semantics=("parallel", "parallel"),
                                             vmem_limit_bytes=VMEM_LIMIT),
        name="in_proj0",
    )(x, scale1p, shift, w["p"], w["win"], w["bdq"], w["bdk"], w["cos"], w["sin"])


def _neighbours(x, prev_blk, next_blk, i, n_blocks):
    tm = x.shape[0]
    row = lax.broadcasted_iota(jnp.int32, (tm, 1), 0)
    prev_row = jnp.where(i == 0, 0.0, prev_blk[SUBLANES - 1:SUBLANES, :])
    next_row = jnp.where(i == n_blocks - 1, 0.0, next_blk[0:1, :])
    prev = jnp.where(row == 0, prev_row, pltpu.roll(x, 1, axis=0))
    nxt = jnp.where(row == tm - 1, next_row, pltpu.roll(x, tm - 1, axis=0))
    return prev, nxt


def _halo_specs(tm, c, t):
    r = tm // SUBLANES
    last = t // SUBLANES - 1
    main = pl.BlockSpec((1, tm, c), lambda b, i: (b, i, 0))
    prev = pl.BlockSpec((1, SUBLANES, c), lambda b, i: (b, jnp.maximum(i * r - 1, 0), 0))
    nxt = pl.BlockSpec((1, SUBLANES, c), lambda b, i: (b, jnp.minimum((i + 1) * r, last), 0))
    return [main, prev, nxt]


def _rwprep_kernel(n_blocks, s_ref, sp_ref, sn_ref, p_ref, w2_ref, a2_ref, bd_ref,
                   r_ref, k_ref, v_ref, kkn_ref, b_ref, lw_ref):
    i = pl.program_id(1)
    x = s_ref[0]
    prev, nxt = _neighbours(x, sp_ref[0], sn_ref[0], i, n_blocks)
    xs = x + _prow(p_ref, P0_MU, x.shape[-1]) * (0.5 * (prev + nxt) - x)
    c = 512
    r, kr, vr = xs[:, :c], xs[:, c:2 * c], xs[:, 2 * c:3 * c]
    lw_in = xs[:, 3 * c:3 * c + LANES]
    la_in = xs[:, 3 * c + LANES:3 * c + 2 * LANES]
    lor = _mm(jnp.tanh(lw_in), w2_ref[...]) + _prow(p_ref, P0_W0, 2 * c)
    logw = -RW_DECAY_SCALE * jax.nn.sigmoid(lor)
    a = jax.nn.sigmoid(_prow(p_ref, P0_A0, c) + _mm(la_in, a2_ref[...]))
    kkp = kr * _prow(p_ref, P0_KK, c)
    kkn = kkp * lax.rsqrt(_seg_sum(kkp * kkp, bd_ref) + EPS)
    r_ref[0] = r
    k_ref[0] = kr * (1.0 + (a - 1.0) * _prow(p_ref, P0_KA, c))
    v_ref[0] = vr
    kkn_ref[0] = kkn
    b_ref[0] = kkn * a
    lw_ref[0, 0] = logw[:, :c]
    lw_ref[1, 0] = logw[:, c:]


def _rwprep(slab, w, tm):
    bsz, t, cs = slab.shape
    c = 512
    nb = t // tm
    tok = pl.BlockSpec((1, tm, c), lambda b, i: (b, i, 0))
    outs = [jax.ShapeDtypeStruct((bsz, t, c), F32)] * 5 + [jax.ShapeDtypeStruct((2, bsz, t, c), F32)]
    names = ("p", "w2", "a2", "bd")
    return pl.pallas_call(
        functools.partial(_rwprep_kernel, nb),
        out_shape=outs,
        grid=(bsz, nb),
        in_specs=_halo_specs(tm, cs, t) + [_const_spec(w[k].shape) for k in names],
        out_specs=[tok] * 5 + [pl.BlockSpec((2, 1, tm, c), lambda b, i: (0, b, i, 0))],
        compiler_params=pltpu.CompilerParams(dimension_semantics=("parallel", "parallel"),
                                             vmem_limit_bytes=VMEM_LIMIT),
        name="rwkv_prep",
    )(slab, slab, slab, *[w[k] for k in names])


QUAD = 4 * HEAD64


RW_SUB = 4
GDN_SUB = 8
GLA_SUB = 8


def _rwscan_kernel(nc, nsub, rf, kf, vf, kkf, bf, lwf, rb, kb, vb, kkb, bb, lwb, s0_ref,
                   of_ref, ob_ref, sf_ref, st_ref):
    c = pl.program_id(1)
    L = CHUNK
    nq = rf.shape[-1] // QUAD

    @pl.when(c == 0)
    def _():
        st_ref[...] = s0_ref[0]

    hmask = _lane_group_masks(QUAD, HEAD64, 4)
    rowq = lax.broadcasted_iota(jnp.int32, (QUAD, QUAD), 0)
    colq = lax.broadcasted_iota(jnp.int32, (QUAD, QUAD), 1)
    bd = (rowq >> 6) == (colq >> 6)
    masks = []
    rowc = lax.broadcasted_iota(jnp.int32, (L, QUAD), 0)
    colc = lax.broadcasted_iota(jnp.int32, (L, QUAD), 1) & (L - 1)
    for rev in (False, True):
        _, incl = _order_masks(L, rev)
        dlt = (colc - rowc) if rev else (rowc - colc)
        masks.append((jnp.where(incl, 1.0, 0.0).astype(BF16), dlt > 0, dlt >= 0))
    refs = ((rf, kf, vf, kkf, bf, lwf), (rb, kb, vb, kkb, bb, lwb))

    rows = lambda d, s: slice(s * L, (s + 1) * L) if d == 0 else slice((nsub - 1 - s) * L, (nsub - s) * L)
    units = [(s, d, qd) for s in range(nsub) for d in range(2) for qd in range(nq)]

    dat = []
    for s, d, qd in units:
        sl = slice(qd * QUAD, (qd + 1) * QUAD)
        rw = rows(d, s)
        r_, k_, v_, kk_, b_, lw_ = refs[d]
        dat.append(dict(r=r_[0, rw, sl], k=k_[0, rw, sl], v=v_[0, rw, sl], kk=kk_[0, rw, sl], b=b_[0, rw, sl],
                        lw=lw_[0, 0, rw, sl]))
    cws = [_mm_exact_l(masks[d][0], x["lw"]) for (_, d, _), x in zip(units, dat)]
    for x, cw in zip(dat, cws):
        c0 = cw[L // 2:L // 2 + 1, :]
        tot = jnp.sum(x["lw"], axis=0, keepdims=True)
        e_rel = jnp.exp(cw - c0)
        e_inv = jnp.exp(c0 - cw)
        ec0 = jnp.exp(c0)
        e_fin = jnp.exp(tot - c0)
        a_rel = x["kk"] * e_rel * jnp.exp(-x["lw"])
        r_rel = x["r"] * e_rel
        b_rel = x["b"] * e_inv
        k_rel = x["k"] * e_inv
        x.update(ar_abs=jnp.concatenate([a_rel * ec0, r_rel * ec0], axis=0),
                 ar_rel=jnp.concatenate([a_rel, r_rel], axis=0),
                 sbk=jnp.concatenate([_stack_heads(b_rel, hmask), _stack_heads(k_rel, hmask)], axis=0),
                 bk_fin=jnp.concatenate([b_rel * e_fin, k_rel * e_fin], axis=0), w_tot=jnp.exp(tot),
                 sv=_stack_heads(x["v"], hmask))
    for (_, d, _), x in zip(units, dat):
        _, strict_c, incl_c = masks[d]
        xx = _mm_nt(x["ar_rel"], x["sbk"])
        x["n_c"] = jnp.where(strict_c, xx[0:L, 0:QUAD], 0.0)
        x["mp"] = jnp.concatenate([jnp.where(strict_c, xx[0:L, QUAD:], 0.0),
                                   jnp.where(incl_c, xx[L:, QUAD:], 0.0)], axis=0)
        x["p_rb"] = jnp.where(incl_c, xx[L:, 0:QUAD], 0.0)
    tinv = _tri_inv_pairs([x["n_c"][:, p * 2 * L:(p + 1) * 2 * L] for x in dat for p in range(2)], _mm)
    for i, x in enumerate(dat):
        x["t_c"] = jnp.concatenate(tinv[2 * i:2 * i + 2], axis=-1)
        x["mpv"] = _mm(x["mp"], x["sv"])

    state = {(d, qd): st_ref[d, qd] for d in range(2) for qd in range(nq)}
    for s in range(nsub):
        cur = [(d, qd, dat[(s * 2 + d) * nq + qd]) for d in range(2) for qd in range(nq)]
        for d, qd, x in cur:
            x["ars"] = _mm_nt(x["ar_abs"], state[d, qd])
        for d, qd, x in cur:
            rhs = -x["ars"][0:L] - x["mpv"][0:L]
            x["e"] = _mm(x["t_c"], _stack_heads(rhs, hmask))
        for d, qd, x in cur:
            x["o"] = x["ars"][L:2 * L] + x["mpv"][L:2 * L] + _mm(x["p_rb"], _stack_heads(x["e"], hmask))
            ev = jnp.concatenate([x["e"], x["v"]], axis=0)
            state[d, qd] = state[d, qd] * x["w_tot"] + jnp.where(bd, _mm_tn(ev, x["bk_fin"]), 0.0)
        for d, o_ref in ((0, of_ref), (1, ob_ref)):
            o_ref[0, rows(d, s), :] = jnp.concatenate([x["o"] for dd, _, x in cur if dd == d], axis=-1)
    for (d, qd), val in state.items():
        st_ref[d, qd] = val

    @pl.when(c == nc - 1)
    def _():
        sf_ref[0] = st_ref[...]


def _dir_specs(block, nc, lead=None):
    specs = []
    for d in range(2):
        idx = (lambda b_, c: c) if d == 0 else (lambda b_, c: nc - 1 - c)
        if lead is None:
            specs.append(pl.BlockSpec((1,) + block, functools.partial(
                lambda f, b_, c: (b_, f(b_, c)) + (0,) * (len(block) - 1), idx)))
        else:
            specs.append(pl.BlockSpec((1, 1) + block, functools.partial(
                lambda f, dd, b_, c: (dd, b_, f(b_, c)) + (0,) * (len(block) - 1), idx, d)))
    return specs


def _rwscan(r, k, v, kk, b, lw, s0):
    bsz, t, cdim = r.shape
    blk = RW_SUB * CHUNK
    nc = t // blk
    nq = cdim // QUAD
    tf, tb = _dir_specs((blk, cdim), nc)
    lf, lb = _dir_specs((blk, cdim), nc, lead=True)
    sspec = pl.BlockSpec((1, 2, nq, QUAD, QUAD), lambda b_, c: (b_, 0, 0, 0, 0))
    return pl.pallas_call(
        functools.partial(_rwscan_kernel, nc, RW_SUB),
        out_shape=[jax.ShapeDtypeStruct((bsz, t, cdim), F32)] * 2
                  + [jax.ShapeDtypeStruct((bsz, 2, nq, QUAD, QUAD), F32)],
        grid=(bsz, nc),
        in_specs=[tf] * 5 + [lf] + [tb] * 5 + [lb] + [sspec],
        out_specs=[tf, tb, sspec],
        scratch_shapes=[pltpu.VMEM((2, nq, QUAD, QUAD), F32)],
        compiler_params=pltpu.CompilerParams(dimension_semantics=("parallel", "arbitrary"),
                                             vmem_limit_bytes=VMEM_LIMIT),
        name="rwkv_scan",
    )(r, k, v, kk, b, lw, r, k, v, kk, b, lw, s0)


def _attn_kernel(n_kv, q_ref, k_ref, vt_ref, o_ref):
    n_heads = q_ref.shape[-1] // HEAD64
    group = n_heads // n_kv
    lane = lax.broadcasted_iota(jnp.int32, (1, LANES), 1)
    kmat = k_ref[0]
    ones = vt_ref[0, n_kv * HEAD64:, :]
    vts = [jnp.concatenate([vt_ref[0, kv * HEAD64:(kv + 1) * HEAD64, :], ones], axis=0) for kv in range(n_kv)]
    def scores(h):
        kv = h // group
        pair = q_ref[0, :, (h // 2) * LANES:(h // 2 + 1) * LANES] * (HEAD64 ** -0.5 * LOG2E)
        if (h % 2) != kv:
            pair = pltpu.roll(pair, HEAD64, axis=1)
        qh = jnp.where((lane >= kv * HEAD64) & (lane < (kv + 1) * HEAD64), pair, 0.0).astype(BF16)
        return lax.dot_general(kmat, qh, (((1,), (1,)), ((), ())), preferred_element_type=F32).astype(BF16)

    outs = []
    ahead = 1
    pending = [scores(h) for h in range(min(ahead, n_heads))]
    for h in range(n_heads):
        st = pending.pop(0)
        if h + ahead < n_heads:
            pending.append(scores(h + ahead))
        m = jnp.max(st, axis=0, keepdims=True)
        pt = jnp.exp2(st - m)
        res = jnp.dot(vts[h // group], pt, preferred_element_type=F32)
        outs.append(res[0:HEAD64] / res[HEAD64:HEAD64 + 1])
    o_ref[0] = jnp.concatenate(outs, axis=0).T


def _attention(q, k_all, vt_all, tq):
    bsz, t, c = q.shape
    tk = k_all.shape[1]
    n_kv = k_all.shape[-1] // HEAD64
    assert n_kv * HEAD64 == LANES
    return pl.pallas_call(
        functools.partial(_attn_kernel, n_kv),
        out_shape=jax.ShapeDtypeStruct((bsz, t, c), F32),
        grid=(bsz, t // tq),
        in_specs=[pl.BlockSpec((1, tq, c), lambda b, i: (b, i, 0)),
                  pl.BlockSpec((1, tk, LANES), lambda b, i: (b, 0, 0), pipeline_mode=pl.Buffered(1)),
                  pl.BlockSpec((1, vt_all.shape[1], tk), lambda b, i: (b, 0, 0), pipeline_mode=pl.Buffered(1))],
        out_specs=pl.BlockSpec((1, tq, c), lambda b, i: (b, i, 0)),
        compiler_params=pltpu.CompilerParams(dimension_semantics=("parallel", "parallel"),
                                             vmem_limit_bytes=VMEM_LIMIT),
        name="attention",
    )(q, k_all, vt_all)


def _postnorm_residual(x, out, g_post, gate):
    ms = jnp.mean(out * out, axis=-1, keepdims=True)
    return x + gate * (out * lax.rsqrt(ms + EPS) * g_post)


def _out0_kernel(x_ref, gate_ref, of_ref, ob_ref, r_ref, k_ref, v_ref, rg_ref, oat_ref, ag_ref,
                 p_ref, bd_ref, wo_ref, y_ref):
    o = of_ref[0] + ob_ref[0]
    c = o.shape[-1]
    inv = 1.0 / HEAD64
    mu = _seg_sum(o, bd_ref) * inv
    dlt = o - mu
    var = _seg_sum(dlt * dlt, bd_ref) * inv
    gn = dlt * lax.rsqrt(var + GN_EPS) * _prow(p_ref, P0_GNG, c) + _prow(p_ref, P0_GNB, c)
    bonus = _seg_sum(r_ref[0] * k_ref[0] * _prow(p_ref, P0_RK, c), bd_ref) * v_ref[0]
    o_rw = (gn + bonus) * _silu(rg_ref[0])
    o_at = oat_ref[0] * _silu(ag_ref[0])
    out = _mm(o_rw, wo_ref[0:c, :]) + _mm(o_at, wo_ref[c:2 * c, :])
    y_ref[0] = _postnorm_residual(x_ref[0], out, _prow(p_ref, P0_G_POST, x_ref.shape[-1]), gate_ref[0])


def _out0(x, gate, o_f, o_b, r, k, v, rg, o_at, ag, w, tm):
    bsz, t, d = x.shape
    c = 512
    tokd = pl.BlockSpec((1, tm, d), lambda b, i: (b, i, 0))
    tok = pl.BlockSpec((1, tm, c), lambda b, i: (b, i, 0))
    names = ("p", "bd", "wout")
    return pl.pallas_call(
        _out0_kernel,
        out_shape=jax.ShapeDtypeStruct((bsz, t, d), F32),
        grid=(bsz, t // tm),
        in_specs=[tokd, _mod_spec(gate, d)] + [tok] * 8
                 + [_const_spec(w[n].shape) for n in names],
        out_specs=tokd,
        compiler_params=pltpu.CompilerParams(dimension_semantics=("parallel", "parallel"),
                                             vmem_limit_bytes=VMEM_LIMIT),
        name="out_proj0",
    )(x, gate, o_f, o_b, r, k, v, rg, o_at, ag, *[w[n] for n in names])


(P1_G_PRE, P1_G_POST, P1_GB, P1_ALOG_F, P1_ALOG_B, P1_DTB_F, P1_DTB_B, P1_GLA_G, P1_GDN_G, P1_CONV) = range(10)
W1_COLS = (0, 256, 512, 1024, 1152, 1664, 3200, 3328, 3456, 3968)


def _in1_kernel(x_ref, sc_ref, sh_ref, p_ref, w_ref, g2_ref,
                q_ref, k_ref, v_ref, lw_ref, gg_ref, dqkv_ref, small_ref, dg_ref):
    dm = x_ref.shape[-1]
    h = _prenorm(x_ref[0], _prow(p_ref, P1_G_PRE, dm), sc_ref[0], sh_ref[0]).astype(BF16)
    c = W1_COLS
    proj = lambda j: jnp.dot(h, w_ref[:, c[j]:c[j + 1]], preferred_element_type=F32)
    q_ref[0] = proj(0) * (HEAD64 ** -0.5)
    k_ref[0] = proj(1)
    v_ref[0] = proj(2)
    gg_ref[0] = proj(4)
    dqkv_ref[0] = proj(5)
    small_ref[0, 0] = proj(6)
    small_ref[1, 0] = proj(7)
    dg_ref[0] = proj(8)
    lw = _log_sigmoid(_mm(proj(3), g2_ref[...]) + _prow(p_ref, P1_GB, g2_ref.shape[-1])) * (1.0 / GLA_GATE_NORM)
    half = lw.shape[-1] // 2
    lw_ref[0, 0] = lw[:, :half]
    lw_ref[1, 0] = lw[:, half:]


def _in1(x, scale1p, shift, w, tm):
    bsz, t, d = x.shape
    widths = [256, 256, 512, -256, 512, 1536, -LANES, 512]
    tok = lambda n: pl.BlockSpec((1, tm, n), lambda b, i: (b, i, 0))
    outs, ospecs = [], []
    for n in widths:
        if n < 0:
            outs.append(jax.ShapeDtypeStruct((2, bsz, t, -n), F32))
            ospecs.append(pl.BlockSpec((2, 1, tm, -n), lambda b, i: (0, b, i, 0)))
        else:
            outs.append(jax.ShapeDtypeStruct((bsz, t, n), F32))
            ospecs.append(tok(n))
    return pl.pallas_call(
        _in1_kernel,
        out_shape=outs,
        grid=(bsz, t // tm),
        in_specs=[tok(d), _mod_spec(scale1p, d), _mod_spec(shift, d), _const_spec(w["p"].shape),
                  _const_spec(w["win"].shape, single=True), _const_spec(w["g2"].shape)],
        out_specs=ospecs,
        compiler_params=pltpu.CompilerParams(dimension_semantics=("parallel", "parallel"),
                                             vmem_limit_bytes=VMEM_LIMIT),
        name="in_proj1",
    )(x, scale1p, shift, w["p"], w["win"], w["g2"])


def _gdnprep_kernel(n_blocks, x_ref, xp_ref, xn_ref, s_ref, p_ref, q_ref, k_ref, v_ref, g_ref):
    i = pl.program_id(1)
    x = x_ref[0]
    prev, nxt = _neighbours(x, xp_ref[0], xn_ref[0], i, n_blocks)
    cw = lambda j: _prow(p_ref, P1_CONV + j, x.shape[-1])
    y = _silu(prev * cw(0) + x * cw(1) + nxt * cw(2))
    c = 512
    qs, ks = [], []
    for h in range(c // HEAD128):
        qh = y[:, h * HEAD128:(h + 1) * HEAD128]
        kh = y[:, c + h * HEAD128:c + (h + 1) * HEAD128]
        qs.append(qh * lax.rsqrt(jnp.sum(qh * qh, axis=-1, keepdims=True) + EPS) * (HEAD128 ** -0.5))
        ks.append(kh * lax.rsqrt(jnp.sum(kh * kh, axis=-1, keepdims=True) + EPS))
    q_ref[0] = jnp.concatenate(qs, axis=-1)
    k_ref[0] = jnp.concatenate(ks, axis=-1)
    v_ref[0] = y[:, 2 * c:]
    lane = lax.broadcasted_iota(jnp.int32, (1, LANES), 1)
    nh = c // HEAD128
    for d, (ia, ib) in enumerate(((P1_ALOG_F, P1_DTB_F), (P1_ALOG_B, P1_DTB_B))):
        s = s_ref[d, 0]
        loga = -jnp.exp(_prow(p_ref, ia, LANES)) * _softplus(s + _prow(p_ref, ib, LANES))
        g_ref[d, 0] = jnp.where(lane < nh, loga, jnp.where(lane < 2 * nh, jax.nn.sigmoid(s), 0.0))


def _gdnprep(dqkv, small, w, tm):
    bsz, t, cs = dqkv.shape
    c = 512
    nb = t // tm
    tok = pl.BlockSpec((1, tm, c), lambda b, i: (b, i, 0))
    tokl = pl.BlockSpec((2, 1, tm, LANES), lambda b, i: (0, b, i, 0))
    return pl.pallas_call(
        functools.partial(_gdnprep_kernel, nb),
        out_shape=[jax.ShapeDtypeStruct((bsz, t, c), F32)] * 3 + [jax.ShapeDtypeStruct((2, bsz, t, LANES), F32)],
        grid=(bsz, nb),
        in_specs=_halo_specs(tm, cs, t) + [tokl, _const_spec(w["p"].shape)],
        out_specs=[tok] * 3 + [tokl],
        compiler_params=pltpu.CompilerParams(dimension_semantics=("parallel", "parallel"),
                                             vmem_limit_bytes=VMEM_LIMIT),
        name="gdn_prep",
    )(dqkv, dqkv, dqkv, small, w["p"])


def _glascan_kernel(nc, nsub, qf, kf, vf, lwf, qb, kb, vb, lwb, s0_ref, of_ref, ob_ref, sf_ref, st_ref):
    c = pl.program_id(1)
    L = CHUNK
    nh = qf.shape[-1] // HEAD64

    @pl.when(c == 0)
    def _():
        st_ref[...] = s0_ref[0]

    hmask = _lane_group_masks(nh * HEAD64, HEAD64, nh)
    rowv = lax.broadcasted_iota(jnp.int32, st_ref.shape[1:], 0)
    colk = lax.broadcasted_iota(jnp.int32, st_ref.shape[1:], 1)
    bd = (rowv >> 7) == (colk >> 6)
    refs = ((qf, kf, vf, lwf, of_ref), (qb, kb, vb, lwb, ob_ref))
    rows = lambda d, s: slice(s * L, (s + 1) * L) if d == 0 else slice((nsub - 1 - s) * L, (nsub - s) * L)
    masks = []
    for d in range(2):
        _, incl = _order_masks(L, d == 1)
        _, incl4 = _order_masks(L, d == 1, reps=nh)
        masks.append((jnp.where(incl, 1.0, 0.0).astype(BF16), incl4))

    units = [(s, d) for s in range(nsub) for d in range(2)]
    dat = []
    for s, d in units:
        q_, k_, v_, lw_, _ = refs[d]
        rw = rows(d, s)
        dat.append(dict(q=q_[0, rw, :], k=k_[0, rw, :], v=v_[0, rw, :], lw=lw_[0, 0, rw, :]))
    cws = [_mm_exact_l(masks[d][0], x["lw"]) for (_, d), x in zip(units, dat)]
    for x, cw in zip(dat, cws):
        c0 = cw[L // 2:L // 2 + 1, :]
        tot = jnp.sum(x["lw"], axis=0, keepdims=True)
        q_rel = x["q"] * jnp.exp(cw - c0)
        k_rel = x["k"] * jnp.exp(c0 - cw)
        x.update(sq=_stack_heads(q_rel, hmask), k_rel=k_rel, q_abs=q_rel * jnp.exp(c0),
                 k_fin=k_rel * jnp.exp(tot - c0), w_tot=jnp.exp(tot))
    for (_, d), x in zip(units, dat):
        x["att"] = jnp.where(masks[d][1], _mm_nt(x["sq"], x["k_rel"]), 0.0)
        x["vk"] = jnp.where(bd, _mm_tn(x["v"], x["k_fin"]), 0.0)
    for x in dat:
        x["intra"] = jnp.concatenate(
            [_mm(x["att"][h * L:(h + 1) * L], x["v"][:, h * HEAD128:(h + 1) * HEAD128]) for h in range(nh)], axis=-1)

    state = [st_ref[0], st_ref[1]]
    for s in range(nsub):
        for d in range(2):
            x = dat[s * 2 + d]
            refs[d][4][0, rows(d, s), :] = x["intra"] + _mm_nt(x["q_abs"], state[d])
            state[d] = state[d] * x["w_tot"] + x["vk"]
    st_ref[0] = state[0]
    st_ref[1] = state[1]

    @pl.when(c == nc - 1)
    def _():
        sf_ref[0] = st_ref[...]


def _glascan(q, k, v, lw, s0):
    bsz, t, ck = q.shape
    cv = v.shape[-1]
    nsub = min(GLA_SUB, t // CHUNK)
    blk = nsub * CHUNK
    nc = t // blk
    kf, kb = _dir_specs((blk, ck), nc)
    vf, vb = _dir_specs((blk, cv), nc)
    lf, lb = _dir_specs((blk, ck), nc, lead=True)
    sspec = pl.BlockSpec((1, 2, cv, ck), lambda b_, c: (b_, 0, 0, 0))
    return pl.pallas_call(
        functools.partial(_glascan_kernel, nc, nsub),
        out_shape=[jax.ShapeDtypeStruct((bsz, t, cv), F32)] * 2 + [jax.ShapeDtypeStruct((bsz, 2, cv, ck), F32)],
        grid=(bsz, nc),
        in_specs=[kf, kf, vf, lf, kb, kb, vb, lb, sspec],
        out_specs=[vf, vb, sspec],
        scratch_shapes=[pltpu.VMEM((2, cv, ck), F32)],
        compiler_params=pltpu.CompilerParams(dimension_semantics=("parallel", "arbitrary"),
                                             vmem_limit_bytes=VMEM_LIMIT),
        name="gla_scan",
    )(q, k, v, lw, q, k, v, lw, s0)


def _gdnscan_kernel(nc, nsub, qf, kf, vf, gcf, qb, kb, vb, gcb, s0_ref, of_ref, ob_ref, sf_ref, st_ref):
    c = pl.program_id(1)
    L = CHUNK
    nh = qf.shape[-1] // HEAD128

    @pl.when(c == 0)
    def _():
        st_ref[...] = s0_ref[0]

    refs = ((qf, kf, vf, gcf), (qb, kb, vb, gcb))
    rows = lambda d, s: slice(s * L, (s + 1) * L) if d == 0 else slice((nsub - 1 - s) * L, (nsub - s) * L)
    per_dir = {}
    for d in range(2):
        strict, incl = _order_masks(L, d == 1)
        _, incl_t = _order_masks(L, d == 0)
        incl_bf = jnp.where(incl, 1.0, 0.0).astype(BF16)
        incl_t_bf = jnp.where(incl_t, 1.0, 0.0).astype(BF16)
        for s in range(nsub):
            gcol = refs[d][3][0, 0, rows(d, s), :]
            per_dir[d, s] = dict(strict=strict, incl=incl, gcol=gcol,
                                 cum_c=_mm_exact_l(incl_bf, gcol),
                                 cum_r=_mm_exact_tn(gcol, incl_t_bf),
                                 tot=jnp.sum(gcol, axis=0, keepdims=True))

    units = [(s, d, h) for s in range(nsub) for d in range(2) for h in range(nh)]
    dat = []
    for s, d, h in units:
        hs = slice(h * HEAD128, (h + 1) * HEAD128)
        pd = per_dir[d, s]
        rw = rows(d, s)
        q, k, v = refs[d][0][0, rw, hs], refs[d][1][0, rw, hs], refs[d][2][0, rw, hs]
        g = pd["cum_c"][:, h:h + 1]
        beta = pd["gcol"][:, nh + h:nh + h + 1]
        g_last = pd["tot"][:, h:h + 1]
        gam = jnp.exp(g)
        dat.append(dict(q=q, k=k, qg=q * gam, beta=beta, pd=pd,
                        dec=jnp.exp(jnp.minimum(g - pd["cum_r"][h:h + 1, :], 0.0)),
                        rhs=jnp.concatenate([(beta * gam) * k, beta * v], axis=-1),
                        k_dec=k * jnp.exp(g_last - g), gl=jnp.exp(g_last)))
    for x in dat:
        x["kk"] = _mm_nt(x["k"], x["k"])
        x["aqk"] = jnp.where(x["pd"]["incl"], x["dec"] * _mm_nt(x["q"], x["k"]), 0.0)
    tinv = _tri_inv_many([jnp.where(x["pd"]["strict"], x["dec"] * x["kk"] * x["beta"], 0.0) for x in dat], _mm)
    for x, t in zip(dat, tinv):
        x["wu"] = _mm(t, x["rhs"])
        x["wq"] = jnp.concatenate([x["wu"][:, :HEAD128], x["qg"]], axis=0)

    state = {(d, h): st_ref[d, h] for d in range(2) for h in range(nh)}
    for s in range(nsub):
        cur = [(d, h, dat[(s * 2 + d) * nh + h]) for d in range(2) for h in range(nh)]
        for d, h, x in cur:
            x["ws"] = _mm(x["wq"], state[d, h])
        for d, h, x in cur:
            x["u"] = x["wu"][:, HEAD128:] - x["ws"][0:L]
        for d, h, x in cur:
            x["o"] = x["ws"][L:2 * L] + _mm(x["aqk"], x["u"])
            state[d, h] = state[d, h] * x["gl"] + _mm_tn(x["k_dec"], x["u"])
        for d, o_ref in ((0, of_ref), (1, ob_ref)):
            o_ref[0, rows(d, s), :] = jnp.concatenate([x["o"] for dd, _, x in cur if dd == d], axis=-1)
    for (d, h), val in state.items():
        st_ref[d, h] = val

    @pl.when(c == nc - 1)
    def _():
        sf_ref[0] = st_ref[...]


def _gdnscan(q, k, v, gcol, s0):
    bsz, t, cdim = q.shape
    nh = cdim // HEAD128
    nsub = min(GDN_SUB, t // CHUNK)
    blk = nsub * CHUNK
    nc = t // blk
    tf, tb = _dir_specs((blk, cdim), nc)
    gcf, gcb = _dir_specs((blk, LANES), nc, lead=True)
    sspec = pl.BlockSpec((1, 2, nh, HEAD128, HEAD128), lambda b_, c: (b_, 0, 0, 0, 0))
    return pl.pallas_call(
        functools.partial(_gdnscan_kernel, nc, nsub),
        out_shape=[jax.ShapeDtypeStruct((bsz, t, cdim), F32)] * 2
                  + [jax.ShapeDtypeStruct((bsz, 2, nh, HEAD128, HEAD128), F32)],
        grid=(bsz, nc),
        in_specs=[tf, tf, tf, gcf, tb, tb, tb, gcb, sspec],
        out_specs=[tf, tb, sspec],
        scratch_shapes=[pltpu.VMEM((2, nh, HEAD128, HEAD128), F32)],
        compiler_params=pltpu.CompilerParams(dimension_semantics=("parallel", "arbitrary"),
                                             vmem_limit_bytes=VMEM_LIMIT),
        name="gdn_scan",
    )(q, k, v, gcol, q, k, v, gcol, s0)


def _head_rmsnorm(o, g):
    parts = []
    for h in range(o.shape[-1] // HEAD128):
        oh = o[:, h * HEAD128:(h + 1) * HEAD128]
        parts.append(oh * lax.rsqrt(jnp.mean(oh * oh, axis=-1, keepdims=True) + EPS) * g)
    return jnp.concatenate(parts, axis=-1)


def _out1_kernel(x_ref, gate_ref, glf_ref, glb_ref, gdf_ref, gdb_ref, gg_ref, dg_ref, p_ref, wo_ref, y_ref):
    c = gg_ref.shape[-1]
    o_gla = _head_rmsnorm(glf_ref[0] + glb_ref[0], _prow(p_ref, P1_GLA_G, HEAD128)) * _silu(gg_ref[0])
    o_gdn = _head_rmsnorm(gdf_ref[0] + gdb_ref[0], _prow(p_ref, P1_GDN_G, HEAD128)) * _silu(dg_ref[0])
    out = _mm(o_gla, wo_ref[0:c, :]) + _mm(o_gdn, wo_ref[c:2 * c, :])
    y_ref[0] = _postnorm_residual(x_ref[0], out, _prow(p_ref, P1_G_POST, x_ref.shape[-1]), gate_ref[0])


def _out1(x, gate, gla_f, gla_b, gdn_f, gdn_b, gg, dg, w, tm):
    bsz, t, d = x.shape
    c = 512
    tokd = pl.BlockSpec((1, tm, d), lambda b, i: (b, i, 0))
    tok = pl.BlockSpec((1, tm, c), lambda b, i: (b, i, 0))
    names = ("p", "wout")
    return pl.pallas_call(
        _out1_kernel,
        out_shape=jax.ShapeDtypeStruct((bsz, t, d), F32),
        grid=(bsz, t // tm),
        in_specs=[tokd, _mod_spec(gate, d)] + [tok] * 6
                 + [_const_spec(w[n].shape) for n in names],
        out_specs=tokd,
        compiler_params=pltpu.CompilerParams(dimension_semantics=("parallel", "parallel"),
                                             vmem_limit_bytes=VMEM_LIMIT),
        name="out_proj1",
    )(x, gate, gla_f, gla_b, gdn_f, gdn_b, gg, dg, *[w[n] for n in names])


def _block_ones(n_groups, width):
    return jnp.asarray(np.kron(np.eye(n_groups, dtype=np.float32), np.ones((width, width), np.float32)), BF16)


def _rope_tables(t):
    rows = t // GRID_W
    row_id = np.repeat(np.arange(rows, dtype=np.float32), GRID_W)
    col_id = np.tile(np.arange(GRID_W, dtype=np.float32), rows)
    nf = HEAD64 // 4
    inv = np.float32(ROPE_THETA) ** (-np.arange(nf, dtype=np.float32) / np.float32(nf))
    ang_r = (row_id[:, None] * inv[None, :]).astype(np.float32)
    ang_c = (col_id[:, None] * inv[None, :]).astype(np.float32)
    cos = np.concatenate([np.cos(ang_r)] * 2 + [np.cos(ang_c)] * 2, axis=-1)
    sin = np.concatenate([-np.sin(ang_r), np.sin(ang_r), -np.sin(ang_c), np.sin(ang_c)], axis=-1)
    tile2 = lambda a: jnp.asarray(np.concatenate([a, a], axis=-1).astype(np.float32))
    return tile2(cos), tile2(sin)


def _pack_rows(rows, width):
    return jnp.stack([jnp.pad(r.reshape(-1).astype(F32), (0, width - r.size)) for r in rows])


def _layer0_weights(p):
    w_in = p["w_in"]
    rw_w = 512
    slab_w = 3 * rw_w + 192
    d = w_in.shape[0]
    w = {}
    w["win"] = jnp.concatenate([w_in[:, :slab_w], jnp.zeros((d, W0_COLS[1] - slab_w), F32), w_in[:, slab_w:]],
                               axis=1).astype(BF16)
    w["wout"] = p["w_out"].astype(BF16)
    w["bdq"] = _block_ones(8, HEAD64)
    w["bdk"] = _block_ones(2, HEAD64)
    w["bd"] = w["bdq"]
    w["p"] = _pack_rows([p["g_pre"], p["g_post"], p["rw_mu"], jnp.concatenate([p["rw_w0_f"], p["rw_w0_b"]]),
                         p["rw_a0"], p["rw_k_k"], p["rw_k_a"], p["rw_r_k"], p["rw_gn_g"], p["rw_gn_b"],
                         jnp.tile(p["at_gq"], 8), jnp.tile(p["at_gk"], 2)], W0_COLS[1])
    z = jnp.zeros((64, rw_w), F32)
    w["w2"] = jnp.concatenate([jnp.concatenate([p["rw_w2_f"], z], axis=1),
                               jnp.concatenate([z, p["rw_w2_b"]], axis=1)], axis=0).astype(BF16)
    w["a2"] = jnp.concatenate([p["rw_a2"], z], axis=0).astype(BF16)
    return w


def _layer1_weights(p):
    w_in = p["w_in"]
    d = w_in.shape[0]
    zc = lambda n: jnp.zeros((d, n), F32)
    af, ab, be = w_in[:, 3104:3108], w_in[:, 3108:3112], w_in[:, 3112:3116]
    w = {}
    w["win"] = jnp.concatenate([w_in[:, :1024], w_in[:, 1024:1056], zc(LANES - 32), w_in[:, 1056:3104],
                                af, be, zc(LANES - 8), ab, be, zc(LANES - 8), w_in[:, 3116:]],
                               axis=1).astype(BF16)
    w["wout"] = p["w_out"].astype(BF16)
    z = jnp.zeros((16, 256), F32)
    g2 = jnp.concatenate([jnp.concatenate([p["gla_g2_f"], z], axis=1),
                          jnp.concatenate([z, p["gla_g2_b"]], axis=1)], axis=0)
    w["g2"] = jnp.pad(g2, ((0, LANES - 32), (0, 0))).astype(BF16)
    conv = p["gdn_conv"]
    w["p"] = _pack_rows([p["g_pre"], p["g_post"], jnp.concatenate([p["gla_gb_f"], p["gla_gb_b"]]),
                         p["gdn_A_log_f"], p["gdn_A_log_b"], p["gdn_dt_bias_f"], p["gdn_dt_bias_b"],
                         p["gla_norm_g"], p["gdn_norm_g"], conv[0], conv[1], conv[2]], conv.shape[1])
    return w


def _rw_state_to_bd(s):
    bsz = s.shape[0]
    s = s.reshape(bsz, 2, 4, HEAD64, HEAD64)
    eye = jnp.eye(4, dtype=s.dtype)
    return jnp.einsum("bqhvk,hg->bqhvgk", s, eye).reshape(bsz, 2, QUAD, QUAD)


def _rw_state_from_bd(s):
    bsz = s.shape[0]
    s = s.reshape(bsz, 2, 4, HEAD64, 4, HEAD64)
    return jnp.stack([s[:, :, h, :, h, :] for h in range(4)], axis=2).reshape(bsz, 8, HEAD64, HEAD64)


def _gla_state_to_bd(s):
    bsz = s.shape[0]
    eye = jnp.eye(4, dtype=s.dtype)
    return jnp.einsum("bhkv,hg->bhvgk", s, eye).reshape(bsz, 4 * HEAD128, 4 * HEAD64)


def _gla_state_from_bd(s):
    bsz = s.shape[0]
    s = s.reshape(bsz, 4, HEAD128, 4, HEAD64)
    return jnp.stack([jnp.swapaxes(s[:, h, :, h, :], -1, -2) for h in range(4)], axis=1)


def _trunk(x, mods, w0, w1, ctx, tm, tq):
    bsz, t, dm = x.shape
    latent = ctx is not None
    ts = min(tm, t)
    shared = mods[0][0].shape[0] == 1
    flat = (lambda a: a.reshape(a.shape[:-3] + (1, bsz * t, a.shape[-1]))) if shared else (lambda a: a)
    unflat = lambda a: a.reshape(a.shape[:-3] + (bsz, t, a.shape[-1]))

    sc, sh, gt = mods[0]
    slab, rg, q, k, v, ag = [unflat(a) for a in _in0(flat(x), sc, sh, w0, latent, tm)]
    r, kmod, vr, kkn, bvec, lw = _rwprep(slab, w0, ts)
    if latent:
        s0f, s0b, k_ctx, v_ctx = ctx[:4]
        s0 = jnp.stack([_rw_state_to_bd(s0f), _rw_state_to_bd(s0b)], axis=1)
        k_all = jnp.concatenate([k_ctx.reshape(bsz, -1, 2 * HEAD64), k], axis=1)
        v_all = jnp.concatenate([v_ctx.reshape(bsz, -1, 2 * HEAD64), v], axis=1)
    else:
        s0 = jnp.zeros((bsz, 2, 2, QUAD, QUAD), F32)
        k_all, v_all = k, v
    o_rwf, o_rwb, s_rw = _rwscan(r, kmod, vr, kkn, bvec, lw, s0)
    vt_all = jnp.concatenate([jnp.swapaxes(v_all, 1, 2), jnp.ones((bsz, 16, v_all.shape[1]), F32)], axis=1)
    o_at = _attention(q, k_all.astype(BF16), vt_all.astype(BF16), tq)
    x1 = _out0(flat(x), gt, *[flat(a) for a in (o_rwf, o_rwb, r, kmod, vr, rg, o_at, ag)], w0, tm)

    sc, sh, gt = mods[1]
    gq, gk, gv, glw, gg, dqkv, small, dg = [unflat(a) for a in _in1(x1, sc, sh, w1, tm)]
    dq, dk, dv, gcol = _gdnprep(dqkv, small, w1, ts)
    if latent:
        sgf, sgb, sdf, sdb = ctx[4:]
        s0_gla = jnp.stack([_gla_state_to_bd(sgf), _gla_state_to_bd(sgb)], axis=1)
        s0_gdn = jnp.stack([sdf, sdb], axis=1)
    else:
        s0_gla = jnp.zeros((bsz, 2, 4 * HEAD128, 4 * HEAD64), F32)
        s0_gdn = jnp.zeros((bsz, 2, 4, HEAD128, HEAD128), F32)
    gla_f, gla_b, s_gla = _glascan(gq, gk, gv, glw, s0_gla)
    gdn_f, gdn_b, s_gdn = _gdnscan(dq, dk, dv, gcol, s0_gdn)
    y = unflat(_out1(x1, gt, *[flat(a) for a in (gla_f, gla_b, gdn_f, gdn_b, gg, dg)], w1, tm))

    new = None
    if not latent:
        new = (_rw_state_from_bd(s_rw[:, 0]), _rw_state_from_bd(s_rw[:, 1]),
               k.reshape(bsz, t, 2, HEAD64), v.reshape(bsz, t, 2, HEAD64),
               _gla_state_from_bd(s_gla[:, 0]), _gla_state_from_bd(s_gla[:, 1]), s_gdn[:, 0], s_gdn[:, 1])
    return y, new


def _split_mod(m, d):
    shift, scale, gate = m[..., :d], m[..., d:2 * d], m[..., 2 * d:]
    return (1.0 + scale)[:, None, :], shift[:, None, :], gate[:, None, :]


def kernel(x_prompt, x_sample, state_l0_rwkv_fwd, state_l0_rwkv_bwd, cache_l0_k, cache_l0_v, state_l1_gla_fwd, state_l1_gla_bwd, state_l1_gdn_fwd, state_l1_gdn_bwd, c, c_ctx, l0_mod_w, l0_mod_b, l0_g_pre, l0_g_post, l0_w_in, l0_w_out, l0_rw_mu, l0_rw_w0_f, l0_rw_w2_f, l0_rw_w0_b, l0_rw_w2_b, l0_rw_a0, l0_rw_a2, l0_rw_k_k, l0_rw_k_a, l0_rw_r_k, l0_rw_gn_g, l0_rw_gn_b, l0_at_gq, l0_at_gk, l1_mod_w, l1_mod_b, l1_g_pre, l1_g_post, l1_w_in, l1_w_out, l1_gla_g2_f, l1_gla_gb_f, l1_gla_g2_b, l1_gla_gb_b, l1_gla_norm_g, l1_gdn_conv, l1_gdn_A_log_f, l1_gdn_dt_bias_f, l1_gdn_A_log_b, l1_gdn_dt_bias_b, l1_gdn_norm_g):
    p0 = {"g_pre": l0_g_pre, "g_post": l0_g_post, "w_in": l0_w_in, "w_out": l0_w_out, "rw_mu": l0_rw_mu,
          "rw_w0_f": l0_rw_w0_f, "rw_w2_f": l0_rw_w2_f, "rw_w0_b": l0_rw_w0_b, "rw_w2_b": l0_rw_w2_b,
          "rw_a0": l0_rw_a0, "rw_a2": l0_rw_a2, "rw_k_k": l0_rw_k_k, "rw_k_a": l0_rw_k_a,
          "rw_r_k": l0_rw_r_k, "rw_gn_g": l0_rw_gn_g, "rw_gn_b": l0_rw_gn_b,
          "at_gq": l0_at_gq, "at_gk": l0_at_gk}
    p1 = {"g_pre": l1_g_pre, "g_post": l1_g_post, "w_in": l1_w_in, "w_out": l1_w_out,
          "gla_g2_f": l1_gla_g2_f, "gla_gb_f": l1_gla_gb_f, "gla_g2_b": l1_gla_g2_b,
          "gla_gb_b": l1_gla_gb_b, "gla_norm_g": l1_gla_norm_g, "gdn_conv": l1_gdn_conv,
          "gdn_A_log_f": l1_gdn_A_log_f, "gdn_dt_bias_f": l1_gdn_dt_bias_f,
          "gdn_A_log_b": l1_gdn_A_log_b, "gdn_dt_bias_b": l1_gdn_dt_bias_b,
          "gdn_norm_g": l1_gdn_norm_g}
    d = x_prompt.shape[-1]
    nb = c.shape[0]
    w0 = _layer0_weights(p0)
    w1 = _layer1_weights(p1)
    cos, sin = _rope_tables(x_sample.shape[1])
    w0["cos"], w0["sin"] = cos, sin

    cvec = jnp.concatenate([c, c_ctx[None, :], jnp.zeros((SUBLANES - nb - 1, d), F32)], axis=0)
    m0 = _modulation(cvec, l0_mod_w, l0_mod_b)
    m1 = _modulation(cvec, l1_mod_w, l1_mod_b)
    mods_lat = [_split_mod(m[:nb], d) for m in (m0, m1)]
    mods_ctx = [_split_mod(m[nb:nb + 1], d) for m in (m0, m1)]

    tm = 512
    y_prompt, new = _trunk(x_prompt, mods_ctx, w0, w1, None, tm, min(512, x_prompt.shape[1]))
    ctx = (state_l0_rwkv_fwd, state_l0_rwkv_bwd, cache_l0_k, cache_l0_v,
           state_l1_gla_fwd, state_l1_gla_bwd, state_l1_gdn_fwd, state_l1_gdn_bwd)
    y_sample, _ = _trunk(x_sample, mods_lat, w0, w1, ctx, tm, min(512, x_sample.shape[1]))
    return (y_prompt, y_sample) + tuple(new)
```

```python
import functools

import numpy as np
import jax
import jax.numpy as jnp
from jax import lax
from jax.experimental import pallas as pl
from jax.experimental.pallas import tpu as pltpu

F32 = jnp.float32
BF16 = jnp.bfloat16

EPS = 1e-6
GN_EPS = 64e-5
CHUNK = 64
GRID_W = 64
ROPE_THETA = 10000.0
RW_DECAY_SCALE = 0.6065306597126334
GLA_GATE_NORM = 16.0
LOG2E = 1.4426950408889634
HEAD64 = 64
HEAD128 = 128
LANES = 128
SUBLANES = 8
VMEM_LIMIT = 56 * 1024 * 1024


def _mm(a, b):
    return jnp.dot(a.astype(BF16), b.astype(BF16), preferred_element_type=F32)


def _mm_nt(a, b):
    return lax.dot_general(a.astype(BF16), b.astype(BF16), (((1,), (1,)), ((), ())),
                           preferred_element_type=F32)


def _mm_tn(a, b):
    return lax.dot_general(a.astype(BF16), b.astype(BF16), (((0,), (0,)), ((), ())),
                           preferred_element_type=F32)


def _split3(x):
    hi = x.astype(BF16)
    r1 = x - hi.astype(F32)
    mid = r1.astype(BF16)
    lo = (r1 - mid.astype(F32)).astype(BF16)
    return hi, mid, lo


def _mm_exact_l(mask_bf16, x):
    hi, mid, lo = _split3(x)
    d = functools.partial(jnp.dot, preferred_element_type=F32)
    return d(mask_bf16, hi) + d(mask_bf16, mid) + d(mask_bf16, lo)


def _mm_exact_tn(x, mask_bf16):
    hi, mid, lo = _split3(x)
    d = lambda a: lax.dot_general(a, mask_bf16, (((0,), (0,)), ((), ())), preferred_element_type=F32)
    return d(hi) + d(mid) + d(lo)


def _mm3(a, b):
    ah = a.astype(BF16)
    al = (a - ah.astype(F32)).astype(BF16)
    bh = b.astype(BF16)
    bl = (b - bh.astype(F32)).astype(BF16)
    d = functools.partial(jnp.dot, preferred_element_type=F32)
    return d(ah, bh) + d(ah, bl) + d(al, bh)


def _silu(x):
    return x * jax.nn.sigmoid(x)


def _softplus(x):
    return jnp.maximum(x, 0.0) + jnp.log(1.0 + jnp.exp(-jnp.abs(x)))


def _log_sigmoid(x):
    return jnp.minimum(x, 0.0) - jnp.log(1.0 + jnp.exp(-jnp.abs(x)))


def _order_masks(n, reverse, reps=1):
    row = lax.broadcasted_iota(jnp.int32, (reps * n, n), 0) & (n - 1)
    col = lax.broadcasted_iota(jnp.int32, (reps * n, n), 1)
    d = (col - row) if reverse else (row - col)
    return d > 0, d >= 0


def _tri_inv_many(nmats, mm):
    n = nmats[0].shape[0]
    row = lax.broadcasted_iota(jnp.int32, (n, n), 0)
    col = lax.broadcasted_iota(jnp.int32, (n, n), 1)
    x = row ^ col
    eye = jnp.where(row == col, 1.0, 0.0).astype(F32)
    ts = [eye - jnp.where(x == 1, nm, 0.0) for nm in nmats]
    s = 1
    while (2 << s) <= n:
        lvl = (x >> s) == 1
        tcs = [mm(t, jnp.where(lvl, nm, 0.0)) for t, nm in zip(ts, nmats)]
        ts = [t - mm(tc, t) for t, tc in zip(ts, tcs)]
        s += 1
    return ts


def _tri_inv_pairs(npairs, mm):
    n = npairs[0].shape[0]
    row = lax.broadcasted_iota(jnp.int32, (n, 2 * n), 0)
    col = lax.broadcasted_iota(jnp.int32, (n, 2 * n), 1) & (n - 1)
    x = row ^ col
    row2 = lax.broadcasted_iota(jnp.int32, (2 * n, 2 * n), 0)
    col2 = lax.broadcasted_iota(jnp.int32, (2 * n, 2 * n), 1)
    same = (row2 >= n) == (col2 >= n)
    blockdiag = lambda a: jnp.where(same, jnp.concatenate([a, a], axis=0), 0.0)
    eye = jnp.where(x == 0, 1.0, 0.0).astype(F32)
    ts = [eye - jnp.where(x == 1, nm, 0.0) for nm in npairs]
    s = 1
    while (2 << s) <= n:
        lvl = (x >> s) == 1
        tcs = [mm(t, blockdiag(jnp.where(lvl, nm, 0.0))) for t, nm in zip(ts, npairs)]
        ts = [t - mm(tc, blockdiag(t)) for t, tc in zip(ts, tcs)]
        s += 1
    return ts


def _lane_group_masks(width, group, count):
    lane = lax.broadcasted_iota(jnp.int32, (1, width), 1)
    return [((lane >= g * group) & (lane < (g + 1) * group)) for g in range(count)]


def _stack_heads(x, masks):
    return jnp.concatenate([jnp.where(m, x, 0.0) for m in masks], axis=0)


def _mod_kernel(c_ref, w_ref, b_ref, o_ref):
    s = _silu(c_ref[...])
    o_ref[...] = _mm3(s, w_ref[...]) + b_ref[...]


def _modulation(cvec, mod_w, mod_b):
    rows, d = cvec.shape
    n = mod_w.shape[1]
    tn = 1024
    return pl.pallas_call(
        _mod_kernel,
        out_shape=jax.ShapeDtypeStruct((rows, n), F32),
        grid=(n // tn,),
        in_specs=[pl.BlockSpec((rows, d), lambda j: (0, 0)),
                  pl.BlockSpec((d, tn), lambda j: (0, j)),
                  pl.BlockSpec((1, tn), lambda j: (0, j))],
        out_specs=pl.BlockSpec((rows, tn), lambda j: (0, j)),
        compiler_params=pltpu.CompilerParams(vmem_limit_bytes=VMEM_LIMIT),
        name="modulation",
    )(cvec, mod_w, mod_b.reshape(1, n))


def _prenorm(x, g, scale1p, shift):
    ms = jnp.mean(x * x, axis=-1, keepdims=True)
    return x * lax.rsqrt(ms + EPS) * g * scale1p + shift


def _seg_sum(x, bd_ref):
    hi = x.astype(BF16)
    lo = (x - hi.astype(F32)).astype(BF16)
    bd = bd_ref[...]
    return jnp.dot(hi, bd, preferred_element_type=F32) + jnp.dot(lo, bd, preferred_element_type=F32)


def _rope(x, cos, sin_signed):
    w = x.shape[-1]
    lane = lax.broadcasted_iota(jnp.int32, (1, w), 1)
    first = (lane & 31) < 16
    partner = jnp.where(first, pltpu.roll(x, w - 16, axis=1), pltpu.roll(x, 16, axis=1))
    return x * cos + partner * sin_signed


def _tile_lanes(x, reps):
    return jnp.concatenate([x] * reps, axis=-1) if reps > 1 else x


P0_G_PRE, P0_G_POST, P0_MU, P0_W0, P0_A0, P0_KK, P0_KA, P0_RK, P0_GNG, P0_GNB, P0_GQ, P0_GK = range(12)
W0_COLS = (0, 1792, 2304, 2816, 2944, 3072, 3584)


def _prow(p_ref, i, n):
    return p_ref[i:i + 1, 0:n]


def _in0_kernel(use_rope, x_ref, sc_ref, sh_ref, p_ref, w_ref, bdq_ref, bdk_ref, cos_ref, sin_ref,
                slab_ref, rg_ref, q_ref, k_ref, v_ref, ag_ref):
    dm = x_ref.shape[-1]
    h = _prenorm(x_ref[0], _prow(p_ref, P0_G_PRE, dm), sc_ref[0], sh_ref[0]).astype(BF16)
    c = W0_COLS
    proj = lambda j: jnp.dot(h, w_ref[:, c[j]:c[j + 1]], preferred_element_type=F32)
    slab_ref[0] = proj(0)
    rg_ref[0] = proj(1)
    v_ref[0] = proj(4)
    ag_ref[0] = proj(5)
    q = proj(2)
    k = proj(3)
    q = q * lax.rsqrt(_seg_sum(q * q, bdq_ref) * (1.0 / HEAD64) + EPS) * _prow(p_ref, P0_GQ, q.shape[-1])
    k = k * lax.rsqrt(_seg_sum(k * k, bdk_ref) * (1.0 / HEAD64) + EPS) * _prow(p_ref, P0_GK, k.shape[-1])
    if use_rope:
        cos = cos_ref[...]
        sin = sin_ref[...]
        q = _rope(q, _tile_lanes(cos, q.shape[-1] // LANES), _tile_lanes(sin, q.shape[-1] // LANES))
        k = _rope(k, cos, sin)
    q_ref[0] = q
    k_ref[0] = k


def _const_spec(shape, single=False):
    nd = len(shape)
    if single:
        return pl.BlockSpec(shape, lambda *_: (0,) * nd, pipeline_mode=pl.Buffered(1))
    return pl.BlockSpec(shape, lambda *_: (0,) * nd)


def _mod_spec(arr, d):
    if arr.shape[0] == 1:
        return pl.BlockSpec((1, 1, d), lambda b, i: (0, 0, 0))
    return pl.BlockSpec((1, 1, d), lambda b, i: (b, 0, 0))


def _in0(x, scale1p, shift, w, use_rope, tm):
    bsz, t, d = x.shape
    widths = [W0_COLS[j + 1] - W0_COLS[j] for j in range(6)]
    outs = [jax.ShapeDtypeStruct((bsz, t, n), F32) for n in widths]
    tok = lambda n: pl.BlockSpec((1, tm, n), lambda b, i: (b, i, 0))
    in_specs = [tok(d), _mod_spec(scale1p, d), _mod_spec(shift, d), _const_spec(w["p"].shape),
                _const_spec(w["win"].shape, single=True), _const_spec(w["bdq"].shape), _const_spec(w["bdk"].shape)]
    in_specs += [pl.BlockSpec((tm, LANES), lambda b, i: (i, 0))] * 2
    return pl.pallas_call(
        functools.partial(_in0_kernel, use_rope),
        out_shape=outs,
        grid=(bsz, t // tm),
        in_specs=in_specs,
        out_specs=[tok(n) for n in widths],
        compiler_params=pltpu.CompilerParams(dimension_semantics=("parallel", "parallel"),
                                             vmem_limit_bytes=VMEM_LIMIT),
        name="in_proj0",
    )(x, scale1p, shift, w["p"], w["win"], w["bdq"], w["bdk"], w["cos"], w["sin"])


def _neighbours(x, prev_blk, next_blk, i, n_blocks):
    tm = x.shape[0]
    row = lax.broadcasted_iota(jnp.int32, (tm, 1), 0)
    prev_row = jnp.where(i == 0, 0.0, prev_blk[SUBLANES - 1:SUBLANES, :])
    next_row = jnp.where(i == n_blocks - 1, 0.0, next_blk[0:1, :])
    prev = jnp.where(row == 0, prev_row, pltpu.roll(x, 1, axis=0))
    nxt = jnp.where(row == tm - 1, next_row, pltpu.roll(x, tm - 1, axis=0))
    return prev, nxt


def _halo_specs(tm, c, t):
    r = tm // SUBLANES
    last = t // SUBLANES - 1
    main = pl.BlockSpec((1, tm, c), lambda b, i: (b, i, 0))
    prev = pl.BlockSpec((1, SUBLANES, c), lambda b, i: (b, jnp.maximum(i * r - 1, 0), 0))
    nxt = pl.BlockSpec((1, SUBLANES, c), lambda b, i: (b, jnp.minimum((i + 1) * r, last), 0))
    return [main, prev, nxt]


def _rwprep_kernel(n_blocks, s_ref, sp_ref, sn_ref, p_ref, w2_ref, a2_ref, bd_ref,
                   r_ref, k_ref, v_ref, kkn_ref, b_ref, lw_ref):
    i = pl.program_id(1)
    x = s_ref[0]
    prev, nxt = _neighbours(x, sp_ref[0], sn_ref[0], i, n_blocks)
    xs = x + _prow(p_ref, P0_MU, x.shape[-1]) * (0.5 * (prev + nxt) - x)
    c = 512
    r, kr, vr = xs[:, :c], xs[:, c:2 * c], xs[:, 2 * c:3 * c]
    lw_in = xs[:, 3 * c:3 * c + LANES]
    la_in = xs[:, 3 * c + LANES:3 * c + 2 * LANES]
    lor = _mm(jnp.tanh(lw_in), w2_ref[...]) + _prow(p_ref, P0_W0, 2 * c)
    logw = -RW_DECAY_SCALE * jax.nn.sigmoid(lor)
    a = jax.nn.sigmoid(_prow(p_ref, P0_A0, c) + _mm(la_in, a2_ref[...]))
    kkp = kr * _prow(p_ref, P0_KK, c)
    kkn = kkp * lax.rsqrt(_seg_sum(kkp * kkp, bd_ref) + EPS)
    r_ref[0] = r
    k_ref[0] = kr * (1.0 + (a - 1.0) * _prow(p_ref, P0_KA, c))
    v_ref[0] = vr
    kkn_ref[0] = kkn
    b_ref[0] = kkn * a
    lw_ref[0, 0] = logw[:, :c]
    lw_ref[1, 0] = logw[:, c:]


def _rwprep(slab, w, tm):
    bsz, t, cs = slab.shape
    c = 512
    nb = t // tm
    tok = pl.BlockSpec((1, tm, c), lambda b, i: (b, i, 0))
    outs = [jax.ShapeDtypeStruct((bsz, t, c), F32)] * 5 + [jax.ShapeDtypeStruct((2, bsz, t, c), F32)]
    names = ("p", "w2", "a2", "bd")
    return pl.pallas_call(
        functools.partial(_rwprep_kernel, nb),
        out_shape=outs,
        grid=(bsz, nb),
        in_specs=_halo_specs(tm, cs, t) + [_const_spec(w[k].shape) for k in names],
        out_specs=[tok] * 5 + [pl.BlockSpec((2, 1, tm, c), lambda b, i: (0, b, i, 0))],
        compiler_params=pltpu.CompilerParams(dimension_semantics=("parallel", "parallel"),
                                             vmem_limit_bytes=VMEM_LIMIT),
        name="rwkv_prep",
    )(slab, slab, slab, *[w[k] for k in names])


QUAD = 4 * HEAD64


RW_SUB = 4
GDN_SUB = 4
GLA_SUB = 4


def _rwscan_kernel(nc, nsub, rf, kf, vf, kkf, bf, lwf, rb, kb, vb, kkb, bb, lwb, s0_ref,
                   of_ref, ob_ref, sf_ref, st_ref):
    c = pl.program_id(1)
    L = CHUNK
    nq = rf.shape[-1] // QUAD

    @pl.when(c == 0)
    def _():
        st_ref[...] = s0_ref[0]

    hmask = _lane_group_masks(QUAD, HEAD64, 4)
    rowq = lax.broadcasted_iota(jnp.int32, (QUAD, QUAD), 0)
    colq = lax.broadcasted_iota(jnp.int32, (QUAD, QUAD), 1)
    bd = (rowq >> 6) == (colq >> 6)
    masks = []
    rowc = lax.broadcasted_iota(jnp.int32, (L, QUAD), 0)
    colc = lax.broadcasted_iota(jnp.int32, (L, QUAD), 1) & (L - 1)
    for rev in (False, True):
        _, incl = _order_masks(L, rev)
        dlt = (colc - rowc) if rev else (rowc - colc)
        masks.append((jnp.where(incl, 1.0, 0.0).astype(BF16), dlt > 0, dlt >= 0))
    refs = ((rf, kf, vf, kkf, bf, lwf), (rb, kb, vb, kkb, bb, lwb))

    rows = lambda d, s: slice(s * L, (s + 1) * L) if d == 0 else slice((nsub - 1 - s) * L, (nsub - s) * L)
    units = [(s, d, qd) for s in range(nsub) for d in range(2) for qd in range(nq)]

    dat = []
    for s, d, qd in units:
        sl = slice(qd * QUAD, (qd + 1) * QUAD)
        rw = rows(d, s)
        r_, k_, v_, kk_, b_, lw_ = refs[d]
        dat.append(dict(r=r_[0, rw, sl], k=k_[0, rw, sl], v=v_[0, rw, sl], kk=kk_[0, rw, sl], b=b_[0, rw, sl],
                        lw=lw_[0, 0, rw, sl]))
    cws = [_mm_exact_l(masks[d][0], x["lw"]) for (_, d, _), x in zip(units, dat)]
    for x, cw in zip(dat, cws):
        c0 = cw[L // 2:L // 2 + 1, :]
        tot = jnp.sum(x["lw"], axis=0, keepdims=True)
        e_rel = jnp.exp(cw - c0)
        e_inv = jnp.exp(c0 - cw)
        ec0 = jnp.exp(c0)
        e_fin = jnp.exp(tot - c0)
        a_rel = x["kk"] * e_rel * jnp.exp(-x["lw"])
        r_rel = x["r"] * e_rel
        b_rel = x["b"] * e_inv
        k_rel = x["k"] * e_inv
        x.update(ar_abs=jnp.concatenate([a_rel * ec0, r_rel * ec0], axis=0),
                 ar_rel=jnp.concatenate([a_rel, r_rel], axis=0),
                 sbk=jnp.concatenate([_stack_heads(b_rel, hmask), _stack_heads(k_rel, hmask)], axis=0),
                 bk_fin=jnp.concatenate([b_rel * e_fin, k_rel * e_fin], axis=0), w_tot=jnp.exp(tot),
                 sv=_stack_heads(x["v"], hmask))
    for (_, d, _), x in zip(units, dat):
        _, strict_c, incl_c = masks[d]
        xx = _mm_nt(x["ar_rel"], x["sbk"])
        x["n_c"] = jnp.where(strict_c, xx[0:L, 0:QUAD], 0.0)
        x["mp"] = jnp.concatenate([jnp.where(strict_c, xx[0:L, QUAD:], 0.0),
                                   jnp.where(incl_c, xx[L:, QUAD:], 0.0)], axis=0)
        x["p_rb"] = jnp.where(incl_c, xx[L:, 0:QUAD], 0.0)
    npairs = [x["n_c"][:, p * 2 * L:(p + 1) * 2 * L] for x in dat for p in range(2)]
    half = len(npairs) // 2
    tinv = _tri_inv_pairs(npairs[:half], _mm) + _tri_inv_pairs(npairs[half:], _mm)
    for i, x in enumerate(dat):
        x["t_c"] = jnp.concatenate(tinv[2 * i:2 * i + 2], axis=-1)
        x["mpv"] = _mm(x["mp"], x["sv"])

    state = {(d, qd): st_ref[d, qd] for d in range(2) for qd in range(nq)}
    for s in range(nsub):
        cur = [(d, qd, dat[(s * 2 + d) * nq + qd]) for d in range(2) for qd in range(nq)]
        for d, qd, x in cur:
            x["ars"] = _mm_nt(x["ar_abs"], state[d, qd])
        for d, qd, x in cur:
            rhs = -x["ars"][0:L] - x["mpv"][0:L]
            x["e"] = _mm(x["t_c"], _stack_heads(rhs, hmask))
        for d, qd, x in cur:
            x["o"] = x["ars"][L:2 * L] + x["mpv"][L:2 * L] + _mm(x["p_rb"], _stack_heads(x["e"], hmask))
            ev = jnp.concatenate([x["e"], x["v"]], axis=0)
            state[d, qd] = state[d, qd] * x["w_tot"] + jnp.where(bd, _mm_tn(ev, x["bk_fin"]), 0.0)
        for d, o_ref in ((0, of_ref), (1, ob_ref)):
            o_ref[0, rows(d, s), :] = jnp.concatenate([x["o"] for dd, _, x in cur if dd == d], axis=-1)
    for (d, qd), val in state.items():
        st_ref[d, qd] = val

    @pl.when(c == nc - 1)
    def _():
        sf_ref[0] = st_ref[...]


def _dir_specs(block, nc, lead=None):
    specs = []
    for d in range(2):
        idx = (lambda b_, c: c) if d == 0 else (lambda b_, c: nc - 1 - c)
        if lead is None:
            specs.append(pl.BlockSpec((1,) + block, functools.partial(
                lambda f, b_, c: (b_, f(b_, c)) + (0,) * (len(block) - 1), idx)))
        else:
            specs.append(pl.BlockSpec((1, 1) + block, functools.partial(
                lambda f, dd, b_, c: (dd, b_, f(b_, c)) + (0,) * (len(block) - 1), idx, d)))
    return specs


def _rwscan(r, k, v, kk, b, lw, s0):
    bsz, t, cdim = r.shape
    blk = RW_SUB * CHUNK
    nc = t // blk
    nq = cdim // QUAD
    tf, tb = _dir_specs((blk, cdim), nc)
    lf, lb = _dir_specs((blk, cdim), nc, lead=True)
    sspec = pl.BlockSpec((1, 2, nq, QUAD, QUAD), lambda b_, c: (b_, 0, 0, 0, 0))
    return pl.pallas_call(
        functools.partial(_rwscan_kernel, nc, RW_SUB),
        out_shape=[jax.ShapeDtypeStruct((bsz, t, cdim), F32)] * 2
                  + [jax.ShapeDtypeStruct((bsz, 2, nq, QUAD, QUAD), F32)],
        grid=(bsz, nc),
        in_specs=[tf] * 5 + [lf] + [tb] * 5 + [lb] + [sspec],
        out_specs=[tf, tb, sspec],
        scratch_shapes=[pltpu.VMEM((2, nq, QUAD, QUAD), F32)],
        compiler_params=pltpu.CompilerParams(dimension_semantics=("parallel", "arbitrary"),
                                             vmem_limit_bytes=VMEM_LIMIT),
        name="rwkv_scan",
    )(r, k, v, kk, b, lw, r, k, v, kk, b, lw, s0)


def _attn_kernel(n_kv, q_ref, k_ref, vt_ref, o_ref):
    n_heads = q_ref.shape[-1] // HEAD64
    group = n_heads // n_kv
    lane = lax.broadcasted_iota(jnp.int32, (1, LANES), 1)
    kmat = k_ref[0]
    ones = vt_ref[0, n_kv * HEAD64:, :]
    vts = [jnp.concatenate([vt_ref[0, kv * HEAD64:(kv + 1) * HEAD64, :], ones], axis=0) for kv in range(n_kv)]
    def scores(h):
        kv = h // group
        pair = q_ref[0, :, (h // 2) * LANES:(h // 2 + 1) * LANES] * (HEAD64 ** -0.5 * LOG2E)
        if (h % 2) != kv:
            pair = pltpu.roll(pair, HEAD64, axis=1)
        qh = jnp.where((lane >= kv * HEAD64) & (lane < (kv + 1) * HEAD64), pair, 0.0).astype(BF16)
        return lax.dot_general(kmat, qh, (((1,), (1,)), ((), ())), preferred_element_type=F32).astype(BF16)

    outs = []
    ahead = 1
    pending = [scores(h) for h in range(min(ahead, n_heads))]
    for h in range(n_heads):
        st = pending.pop(0)
        if h + ahead < n_heads:
            pending.append(scores(h + ahead))
        m = jnp.max(st, axis=0, keepdims=True)
        pt = jnp.exp2(st - m)
        res = jnp.dot(vts[h // group], pt, preferred_element_type=F32)
        outs.append(res[0:HEAD64] / res[HEAD64:HEAD64 + 1])
    o_ref[0] = jnp.concatenate(outs, axis=0).T


def _attention(q, k_all, vt_all, tq):
    bsz, t, c = q.shape
    tk = k_all.shape[1]
    n_kv = k_all.shape[-1] // HEAD64
    assert n_kv * HEAD64 == LANES
    return pl.pallas_call(
        functools.partial(_attn_kernel, n_kv),
        out_shape=jax.ShapeDtypeStruct((bsz, t, c), F32),
        grid=(bsz, t // tq),
        in_specs=[pl.BlockSpec((1, tq, c), lambda b, i: (b, i, 0)),
                  pl.BlockSpec((1, tk, LANES), lambda b, i: (b, 0, 0), pipeline_mode=pl.Buffered(1)),
                  pl.BlockSpec((1, vt_all.shape[1], tk), lambda b, i: (b, 0, 0), pipeline_mode=pl.Buffered(1))],
        out_specs=pl.BlockSpec((1, tq, c), lambda b, i: (b, i, 0)),
        compiler_params=pltpu.CompilerParams(dimension_semantics=("parallel", "parallel"),
                                             vmem_limit_bytes=VMEM_LIMIT),
        name="attention",
    )(q, k_all, vt_all)


def _postnorm_residual(x, out, g_post, gate):
    ms = jnp.mean(out * out, axis=-1, keepdims=True)
    return x + gate * (out * lax.rsqrt(ms + EPS) * g_post)


def _out0_kernel(x_ref, gate_ref, of_ref, ob_ref, r_ref, k_ref, v_ref, rg_ref, oat_ref, ag_ref,
                 p_ref, bd_ref, wo_ref, y_ref):
    o = of_ref[0] + ob_ref[0]
    c = o.shape[-1]
    inv = 1.0 / HEAD64
    mu = _seg_sum(o, bd_ref) * inv
    dlt = o - mu
    var = _seg_sum(dlt * dlt, bd_ref) * inv
    gn = dlt * lax.rsqrt(var + GN_EPS) * _prow(p_ref, P0_GNG, c) + _prow(p_ref, P0_GNB, c)
    bonus = _seg_sum(r_ref[0] * k_ref[0] * _prow(p_ref, P0_RK, c), bd_ref) * v_ref[0]
    o_rw = (gn + bonus) * _silu(rg_ref[0])
    o_at = oat_ref[0] * _silu(ag_ref[0])
    out = _mm(o_rw, wo_ref[0:c, :]) + _mm(o_at, wo_ref[c:2 * c, :])
    y_ref[0] = _postnorm_residual(x_ref[0], out, _prow(p_ref, P0_G_POST, x_ref.shape[-1]), gate_ref[0])


def _out0(x, gate, o_f, o_b, r, k, v, rg, o_at, ag, w, tm):
    bsz, t, d = x.shape
    c = 512
    tokd = pl.BlockSpec((1, tm, d), lambda b, i: (b, i, 0))
    tok = pl.BlockSpec((1, tm, c), lambda b, i: (b, i, 0))
    names = ("p", "bd", "wout")
    return pl.pallas_call(
        _out0_kernel,
        out_shape=jax.ShapeDtypeStruct((bsz, t, d), F32),
        grid=(bsz, t // tm),
        in_specs=[tokd, _mod_spec(gate, d)] + [tok] * 8
                 + [_const_spec(w[n].shape) for n in names],
        out_specs=tokd,
        compiler_params=pltpu.CompilerParams(dimension_semantics=("parallel", "parallel"),
                                             vmem_limit_bytes=VMEM_LIMIT),
        name="out_proj0",
    )(x, gate, o_f, o_b, r, k, v, rg, o_at, ag, *[w[n] for n in names])


(P1_G_PRE, P1_G_POST, P1_GB, P1_ALOG_F, P1_ALOG_B, P1_DTB_F, P1_DTB_B, P1_GLA_G, P1_GDN_G, P1_CONV) = range(10)
W1_COLS = (0, 256, 512, 1024, 1152, 1664, 3200, 3328, 3456, 3968)


def _in1_kernel(x_ref, sc_ref, sh_ref, p_ref, w_ref, g2_ref,
                q_ref, k_ref, v_ref, lw_ref, gg_ref, dqkv_ref, small_ref, dg_ref):
    dm = x_ref.shape[-1]
    h = _prenorm(x_ref[0], _prow(p_ref, P1_G_PRE, dm), sc_ref[0], sh_ref[0]).astype(BF16)
    c = W1_COLS
    proj = lambda j: jnp.dot(h, w_ref[:, c[j]:c[j + 1]], preferred_element_type=F32)
    q_ref[0] = proj(0) * (HEAD64 ** -0.5)
    k_ref[0] = proj(1)
    v_ref[0] = proj(2)
    gg_ref[0] = proj(4)
    dqkv_ref[0] = proj(5)
    small_ref[0, 0] = proj(6)
    small_ref[1, 0] = proj(7)
    dg_ref[0] = proj(8)
    lw = _log_sigmoid(_mm(proj(3), g2_ref[...]) + _prow(p_ref, P1_GB, g2_ref.shape[-1])) * (1.0 / GLA_GATE_NORM)
    half = lw.shape[-1] // 2
    lw_ref[0, 0] = lw[:, :half]
    lw_ref[1, 0] = lw[:, half:]


def _in1(x, scale1p, shift, w, tm):
    bsz, t, d = x.shape
    widths = [256, 256, 512, -256, 512, 1536, -LANES, 512]
    tok = lambda n: pl.BlockSpec((1, tm, n), lambda b, i: (b, i, 0))
    outs, ospecs = [], []
    for n in widths:
        if n < 0:
            outs.append(jax.ShapeDtypeStruct((2, bsz, t, -n), F32))
            ospecs.append(pl.BlockSpec((2, 1, tm, -n), lambda b, i: (0, b, i, 0)))
        else:
            outs.append(jax.ShapeDtypeStruct((bsz, t, n), F32))
            ospecs.append(tok(n))
    return pl.pallas_call(
        _in1_kernel,
        out_shape=outs,
        grid=(bsz, t // tm),
        in_specs=[tok(d), _mod_spec(scale1p, d), _mod_spec(shift, d), _const_spec(w["p"].shape),
                  _const_spec(w["win"].shape, single=True), _const_spec(w["g2"].shape)],
        out_specs=ospecs,
        compiler_params=pltpu.CompilerParams(dimension_semantics=("parallel", "parallel"),
                                             vmem_limit_bytes=VMEM_LIMIT),
        name="in_proj1",
    )(x, scale1p, shift, w["p"], w["win"], w["g2"])


def _gdnprep_kernel(n_blocks, x_ref, xp_ref, xn_ref, s_ref, p_ref, q_ref, k_ref, v_ref, g_ref):
    i = pl.program_id(1)
    x = x_ref[0]
    prev, nxt = _neighbours(x, xp_ref[0], xn_ref[0], i, n_blocks)
    cw = lambda j: _prow(p_ref, P1_CONV + j, x.shape[-1])
    y = _silu(prev * cw(0) + x * cw(1) + nxt * cw(2))
    c = 512
    qs, ks = [], []
    for h in range(c // HEAD128):
        qh = y[:, h * HEAD128:(h + 1) * HEAD128]
        kh = y[:, c + h * HEAD128:c + (h + 1) * HEAD128]
        qs.append(qh * lax.rsqrt(jnp.sum(qh * qh, axis=-1, keepdims=True) + EPS) * (HEAD128 ** -0.5))
        ks.append(kh * lax.rsqrt(jnp.sum(kh * kh, axis=-1, keepdims=True) + EPS))
    q_ref[0] = jnp.concatenate(qs, axis=-1)
    k_ref[0] = jnp.concatenate(ks, axis=-1)
    v_ref[0] = y[:, 2 * c:]
    lane = lax.broadcasted_iota(jnp.int32, (1, LANES), 1)
    nh = c // HEAD128
    for d, (ia, ib) in enumerate(((P1_ALOG_F, P1_DTB_F), (P1_ALOG_B, P1_DTB_B))):
        s = s_ref[d, 0]
        loga = -jnp.exp(_prow(p_ref, ia, LANES)) * _softplus(s + _prow(p_ref, ib, LANES))
        g_ref[d, 0] = jnp.where(lane < nh, loga, jnp.where(lane < 2 * nh, jax.nn.sigmoid(s), 0.0))


def _gdnprep(dqkv, small, w, tm):
    bsz, t, cs = dqkv.shape
    c = 512
    nb = t // tm
    tok = pl.BlockSpec((1, tm, c), lambda b, i: (b, i, 0))
    tokl = pl.BlockSpec((2, 1, tm, LANES), lambda b, i: (0, b, i, 0))
    return pl.pallas_call(
        functools.partial(_gdnprep_kernel, nb),
        out_shape=[jax.ShapeDtypeStruct((bsz, t, c), F32)] * 3 + [jax.ShapeDtypeStruct((2, bsz, t, LANES), F32)],
        grid=(bsz, nb),
        in_specs=_halo_specs(tm, cs, t) + [tokl, _const_spec(w["p"].shape)],
        out_specs=[tok] * 3 + [tokl],
        compiler_params=pltpu.CompilerParams(dimension_semantics=("parallel", "parallel"),
                                             vmem_limit_bytes=VMEM_LIMIT),
        name="gdn_prep",
    )(dqkv, dqkv, dqkv, small, w["p"])


def _glascan_kernel(nc, nsub, qf, kf, vf, lwf, qb, kb, vb, lwb, s0_ref, of_ref, ob_ref, sf_ref, st_ref):
    c = pl.program_id(1)
    L = CHUNK
    nh = qf.shape[-1] // HEAD64

    @pl.when(c == 0)
    def _():
        st_ref[...] = s0_ref[0]

    hmask = _lane_group_masks(nh * HEAD64, HEAD64, nh)
    rowv = lax.broadcasted_iota(jnp.int32, st_ref.shape[1:], 0)
    colk = lax.broadcasted_iota(jnp.int32, st_ref.shape[1:], 1)
    bd = (rowv >> 7) == (colk >> 6)
    refs = ((qf, kf, vf, lwf, of_ref), (qb, kb, vb, lwb, ob_ref))
    rows = lambda d, s: slice(s * L, (s + 1) * L) if d == 0 else slice((nsub - 1 - s) * L, (nsub - s) * L)
    masks = []
    for d in range(2):
        _, incl = _order_masks(L, d == 1)
        _, incl4 = _order_masks(L, d == 1, reps=nh)
        masks.append((jnp.where(incl, 1.0, 0.0).astype(BF16), incl4))

    units = [(s, d) for s in range(nsub) for d in range(2)]
    dat = []
    for s, d in units:
        q_, k_, v_, lw_, _ = refs[d]
        rw = rows(d, s)
        dat.append(dict(q=q_[0, rw, :], k=k_[0, rw, :], v=v_[0, rw, :], lw=lw_[0, 0, rw, :]))
    cws = [_mm_exact_l(masks[d][0], x["lw"]) for (_, d), x in zip(units, dat)]
    for x, cw in zip(dat, cws):
        c0 = cw[L // 2:L // 2 + 1, :]
        tot = jnp.sum(x["lw"], axis=0, keepdims=True)
        q_rel = x["q"] * jnp.exp(cw - c0)
        k_rel = x["k"] * jnp.exp(c0 - cw)
        x.update(sq=_stack_heads(q_rel, hmask), k_rel=k_rel, q_abs=q_rel * jnp.exp(c0),
                 k_fin=k_rel * jnp.exp(tot - c0), w_tot=jnp.exp(tot))
    for (_, d), x in zip(units, dat):
        x["att"] = jnp.where(masks[d][1], _mm_nt(x["sq"], x["k_rel"]), 0.0)
        x["vk"] = jnp.where(bd, _mm_tn(x["v"], x["k_fin"]), 0.0)
    for x in dat:
        x["intra"] = jnp.concatenate(
            [_mm(x["att"][h * L:(h + 1) * L], x["v"][:, h * HEAD128:(h + 1) * HEAD128]) for h in range(nh)], axis=-1)

    state = [st_ref[0], st_ref[1]]
    for s in range(nsub):
        for d in range(2):
            x = dat[s * 2 + d]
            refs[d][4][0, rows(d, s), :] = x["intra"] + _mm_nt(x["q_abs"], state[d])
            state[d] = state[d] * x["w_tot"] + x["vk"]
    st_ref[0] = state[0]
    st_ref[1] = state[1]

    @pl.when(c == nc - 1)
    def _():
        sf_ref[0] = st_ref[...]


def _glascan(q, k, v, lw, s0):
    bsz, t, ck = q.shape
    cv = v.shape[-1]
    blk = GLA_SUB * CHUNK
    nc = t // blk
    kf, kb = _dir_specs((blk, ck), nc)
    vf, vb = _dir_specs((blk, cv), nc)
    lf, lb = _dir_specs((blk, ck), nc, lead=True)
    sspec = pl.BlockSpec((1, 2, cv, ck), lambda b_, c: (b_, 0, 0, 0))
    return pl.pallas_call(
        functools.partial(_glascan_kernel, nc, GLA_SUB),
        out_shape=[jax.ShapeDtypeStruct((bsz, t, cv), F32)] * 2 + [jax.ShapeDtypeStruct((bsz, 2, cv, ck), F32)],
        grid=(bsz, nc),
        in_specs=[kf, kf, vf, lf, kb, kb, vb, lb, sspec],
        out_specs=[vf, vb, sspec],
        scratch_shapes=[pltpu.VMEM((2, cv, ck), F32)],
        compiler_params=pltpu.CompilerParams(dimension_semantics=("parallel", "arbitrary"),
                                             vmem_limit_bytes=VMEM_LIMIT),
        name="gla_scan",
    )(q, k, v, lw, q, k, v, lw, s0)


def _gdnscan_kernel(nc, nsub, qf, kf, vf, gcf, qb, kb, vb, gcb, s0_ref, of_ref, ob_ref, sf_ref, st_ref):
    c = pl.program_id(1)
    L = CHUNK
    nh = qf.shape[-1] // HEAD128

    @pl.when(c == 0)
    def _():
        st_ref[...] = s0_ref[0]

    refs = ((qf, kf, vf, gcf), (qb, kb, vb, gcb))
    rows = lambda d, s: slice(s * L, (s + 1) * L) if d == 0 else slice((nsub - 1 - s) * L, (nsub - s) * L)
    per_dir = {}
    for d in range(2):
        strict, incl = _order_masks(L, d == 1)
        _, incl_t = _order_masks(L, d == 0)
        incl_bf = jnp.where(incl, 1.0, 0.0).astype(BF16)
        incl_t_bf = jnp.where(incl_t, 1.0, 0.0).astype(BF16)
        for s in range(nsub):
            gcol = refs[d][3][0, 0, rows(d, s), :]
            per_dir[d, s] = dict(strict=strict, incl=incl, gcol=gcol,
                                 cum_c=_mm_exact_l(incl_bf, gcol),
                                 cum_r=_mm_exact_tn(gcol, incl_t_bf),
                                 tot=jnp.sum(gcol, axis=0, keepdims=True))

    units = [(s, d, h) for s in range(nsub) for d in range(2) for h in range(nh)]
    dat = []
    for s, d, h in units:
        hs = slice(h * HEAD128, (h + 1) * HEAD128)
        pd = per_dir[d, s]
        rw = rows(d, s)
        q, k, v = refs[d][0][0, rw, hs], refs[d][1][0, rw, hs], refs[d][2][0, rw, hs]
        g = pd["cum_c"][:, h:h + 1]
        beta = pd["gcol"][:, nh + h:nh + h + 1]
        g_last = pd["tot"][:, h:h + 1]
        gam = jnp.exp(g)
        dat.append(dict(q=q, k=k, qg=q * gam, beta=beta, pd=pd,
                        dec=jnp.exp(jnp.minimum(g - pd["cum_r"][h:h + 1, :], 0.0)),
                        rhs=jnp.concatenate([(beta * gam) * k, beta * v], axis=-1),
                        k_dec=k * jnp.exp(g_last - g), gl=jnp.exp(g_last)))
    for x in dat:
        x["kk"] = _mm_nt(x["k"], x["k"])
        x["aqk"] = jnp.where(x["pd"]["incl"], x["dec"] * _mm_nt(x["q"], x["k"]), 0.0)
    tinv = _tri_inv_many([jnp.where(x["pd"]["strict"], x["dec"] * x["kk"] * x["beta"], 0.0) for x in dat], _mm)
    for x, t in zip(dat, tinv):
        x["wu"] = _mm(t, x["rhs"])
        x["wq"] = jnp.concatenate([x["wu"][:, :HEAD128], x["qg"]], axis=0)

    state = {(d, h): st_ref[d, h] for d in range(2) for h in range(nh)}
    for s in range(nsub):
        cur = [(d, h, dat[(s * 2 + d) * nh + h]) for d in range(2) for h in range(nh)]
        for d, h, x in cur:
            x["ws"] = _mm(x["wq"], state[d, h])
        for d, h, x in cur:
            x["u"] = x["wu"][:, HEAD128:] - x["ws"][0:L]
        for d, h, x in cur:
            x["o"] = x["ws"][L:2 * L] + _mm(x["aqk"], x["u"])
            state[d, h] = state[d, h] * x["gl"] + _mm_tn(x["k_dec"], x["u"])
        for d, o_ref in ((0, of_ref), (1, ob_ref)):
            o_ref[0, rows(d, s), :] = jnp.concatenate([x["o"] for dd, _, x in cur if dd == d], axis=-1)
    for (d, h), val in state.items():
        st_ref[d, h] = val

    @pl.when(c == nc - 1)
    def _():
        sf_ref[0] = st_ref[...]


def _gdnscan(q, k, v, gcol, s0):
    bsz, t, cdim = q.shape
    nh = cdim // HEAD128
    blk = GDN_SUB * CHUNK
    nc = t // blk
    tf, tb = _dir_specs((blk, cdim), nc)
    gcf, gcb = _dir_specs((blk, LANES), nc, lead=True)
    sspec = pl.BlockSpec((1, 2, nh, HEAD128, HEAD128), lambda b_, c: (b_, 0, 0, 0, 0))
    return pl.pallas_call(
        functools.partial(_gdnscan_kernel, nc, GDN_SUB),
        out_shape=[jax.ShapeDtypeStruct((bsz, t, cdim), F32)] * 2
                  + [jax.ShapeDtypeStruct((bsz, 2, nh, HEAD128, HEAD128), F32)],
        grid=(bsz, nc),
        in_specs=[tf, tf, tf, gcf, tb, tb, tb, gcb, sspec],
        out_specs=[tf, tb, sspec],
        scratch_shapes=[pltpu.VMEM((2, nh, HEAD128, HEAD128), F32)],
        compiler_params=pltpu.CompilerParams(dimension_semantics=("parallel", "arbitrary"),
                                             vmem_limit_bytes=VMEM_LIMIT),
        name="gdn_scan",
    )(q, k, v, gcol, q, k, v, gcol, s0)


def _head_rmsnorm(o, g):
    parts = []
    for h in range(o.shape[-1] // HEAD128):
        oh = o[:, h * HEAD128:(h + 1) * HEAD128]
        parts.append(oh * lax.rsqrt(jnp.mean(oh * oh, axis=-1, keepdims=True) + EPS) * g)
    return jnp.concatenate(parts, axis=-1)


def _out1_kernel(x_ref, gate_ref, glf_ref, glb_ref, gdf_ref, gdb_ref, gg_ref, dg_ref, p_ref, wo_ref, y_ref):
    c = gg_ref.shape[-1]
    o_gla = _head_rmsnorm(glf_ref[0] + glb_ref[0], _prow(p_ref, P1_GLA_G, HEAD128)) * _silu(gg_ref[0])
    o_gdn = _head_rmsnorm(gdf_ref[0] + gdb_ref[0], _prow(p_ref, P1_GDN_G, HEAD128)) * _silu(dg_ref[0])
    out = _mm(o_gla, wo_ref[0:c, :]) + _mm(o_gdn, wo_ref[c:2 * c, :])
    y_ref[0] = _postnorm_residual(x_ref[0], out, _prow(p_ref, P1_G_POST, x_ref.shape[-1]), gate_ref[0])


def _out1(x, gate, gla_f, gla_b, gdn_f, gdn_b, gg, dg, w, tm):
    bsz, t, d = x.shape
    c = 512
    tokd = pl.BlockSpec((1, tm, d), lambda b, i: (b, i, 0))
    tok = pl.BlockSpec((1, tm, c), lambda b, i: (b, i, 0))
    names = ("p", "wout")
    return pl.pallas_call(
        _out1_kernel,
        out_shape=jax.ShapeDtypeStruct((bsz, t, d), F32),
        grid=(bsz, t // tm),
        in_specs=[tokd, _mod_spec(gate, d)] + [tok] * 6
                 + [_const_spec(w[n].shape) for n in names],
        out_specs=tokd,
        compiler_params=pltpu.CompilerParams(dimension_semantics=("parallel", "parallel"),
                                             vmem_limit_bytes=VMEM_LIMIT),
        name="out_proj1",
    )(x, gate, gla_f, gla_b, gdn_f, gdn_b, gg, dg, *[w[n] for n in names])


def _block_ones(n_groups, width):
    return jnp.asarray(np.kron(np.eye(n_groups, dtype=np.float32), np.ones((width, width), np.float32)), BF16)


def _rope_tables(t):
    rows = t // GRID_W
    row_id = np.repeat(np.arange(rows, dtype=np.float32), GRID_W)
    col_id = np.tile(np.arange(GRID_W, dtype=np.float32), rows)
    nf = HEAD64 // 4
    inv = np.float32(ROPE_THETA) ** (-np.arange(nf, dtype=np.float32) / np.float32(nf))
    ang_r = (row_id[:, None] * inv[None, :]).astype(np.float32)
    ang_c = (col_id[:, None] * inv[None, :]).astype(np.float32)
    cos = np.concatenate([np.cos(ang_r)] * 2 + [np.cos(ang_c)] * 2, axis=-1)
    sin = np.concatenate([-np.sin(ang_r), np.sin(ang_r), -np.sin(ang_c), np.sin(ang_c)], axis=-1)
    tile2 = lambda a: jnp.asarray(np.concatenate([a, a], axis=-1).astype(np.float32))
    return tile2(cos), tile2(sin)


def _pack_rows(rows, width):
    return jnp.stack([jnp.pad(r.reshape(-1).astype(F32), (0, width - r.size)) for r in rows])


def _layer0_weights(p):
    w_in = p["w_in"]
    rw_w = 512
    slab_w = 3 * rw_w + 192
    d = w_in.shape[0]
    w = {}
    w["win"] = jnp.concatenate([w_in[:, :slab_w], jnp.zeros((d, W0_COLS[1] - slab_w), F32), w_in[:, slab_w:]],
                               axis=1).astype(BF16)
    w["wout"] = p["w_out"].astype(BF16)
    w["bdq"] = _block_ones(8, HEAD64)
    w["bdk"] = _block_ones(2, HEAD64)
    w["bd"] = w["bdq"]
    w["p"] = _pack_rows([p["g_pre"], p["g_post"], p["rw_mu"], jnp.concatenate([p["rw_w0_f"], p["rw_w0_b"]]),
                         p["rw_a0"], p["rw_k_k"], p["rw_k_a"], p["rw_r_k"], p["rw_gn_g"], p["rw_gn_b"],
                         jnp.tile(p["at_gq"], 8), jnp.tile(p["at_gk"], 2)], W0_COLS[1])
    z = jnp.zeros((64, rw_w), F32)
    w["w2"] = jnp.concatenate([jnp.concatenate([p["rw_w2_f"], z], axis=1),
                               jnp.concatenate([z, p["rw_w2_b"]], axis=1)], axis=0).astype(BF16)
    w["a2"] = jnp.concatenate([p["rw_a2"], z], axis=0).astype(BF16)
    return w


def _layer1_weights(p):
    w_in = p["w_in"]
    d = w_in.shape[0]
    zc = lambda n: jnp.zeros((d, n), F32)
    af, ab, be = w_in[:, 3104:3108], w_in[:, 3108:3112], w_in[:, 3112:3116]
    w = {}
    w["win"] = jnp.concatenate([w_in[:, :1024], w_in[:, 1024:1056], zc(LANES - 32), w_in[:, 1056:3104],
                                af, be, zc(LANES - 8), ab, be, zc(LANES - 8), w_in[:, 3116:]],
                               axis=1).astype(BF16)
    w["wout"] = p["w_out"].astype(BF16)
    z = jnp.zeros((16, 256), F32)
    g2 = jnp.concatenate([jnp.concatenate([p["gla_g2_f"], z], axis=1),
                          jnp.concatenate([z, p["gla_g2_b"]], axis=1)], axis=0)
    w["g2"] = jnp.pad(g2, ((0, LANES - 32), (0, 0))).astype(BF16)
    conv = p["gdn_conv"]
    w["p"] = _pack_rows([p["g_pre"], p["g_post"], jnp.concatenate([p["gla_gb_f"], p["gla_gb_b"]]),
                         p["gdn_A_log_f"], p["gdn_A_log_b"], p["gdn_dt_bias_f"], p["gdn_dt_bias_b"],
                         p["gla_norm_g"], p["gdn_norm_g"], conv[0], conv[1], conv[2]], conv.shape[1])
    return w


def _rw_state_to_bd(s):
    bsz = s.shape[0]
    s = s.reshape(bsz, 2, 4, HEAD64, HEAD64)
    eye = jnp.eye(4, dtype=s.dtype)
    return jnp.einsum("bqhvk,hg->bqhvgk", s, eye).reshape(bsz, 2, QUAD, QUAD)


def _rw_state_from_bd(s):
    bsz = s.shape[0]
    s = s.reshape(bsz, 2, 4, HEAD64, 4, HEAD64)
    return jnp.stack([s[:, :, h, :, h, :] for h in range(4)], axis=2).reshape(bsz, 8, HEAD64, HEAD64)


def _gla_state_to_bd(s):
    bsz = s.shape[0]
    eye = jnp.eye(4, dtype=s.dtype)
    return jnp.einsum("bhkv,hg->bhvgk", s, eye).reshape(bsz, 4 * HEAD128, 4 * HEAD64)


def _gla_state_from_bd(s):
    bsz = s.shape[0]
    s = s.reshape(bsz, 4, HEAD128, 4, HEAD64)
    return jnp.stack([jnp.swapaxes(s[:, h, :, h, :], -1, -2) for h in range(4)], axis=1)


def _trunk(x, mods, w0, w1, ctx, tm, tq):
    bsz, t, dm = x.shape
    latent = ctx is not None
    ts = min(tm, t)
    shared = mods[0][0].shape[0] == 1
    flat = (lambda a: a.reshape(a.shape[:-3] + (1, bsz * t, a.shape[-1]))) if shared else (lambda a: a)
    unflat = lambda a: a.reshape(a.shape[:-3] + (bsz, t, a.shape[-1]))

    sc, sh, gt = mods[0]
    slab, rg, q, k, v, ag = [unflat(a) for a in _in0(flat(x), sc, sh, w0, latent, tm)]
    r, kmod, vr, kkn, bvec, lw = _rwprep(slab, w0, ts)
    if latent:
        s0f, s0b, k_ctx, v_ctx = ctx[:4]
        s0 = jnp.stack([_rw_state_to_bd(s0f), _rw_state_to_bd(s0b)], axis=1)
        k_all = jnp.concatenate([k_ctx.reshape(bsz, -1, 2 * HEAD64), k], axis=1)
        v_all = jnp.concatenate([v_ctx.reshape(bsz, -1, 2 * HEAD64), v], axis=1)
    else:
        s0 = jnp.zeros((bsz, 2, 2, QUAD, QUAD), F32)
        k_all, v_all = k, v
    o_rwf, o_rwb, s_rw = _rwscan(r, kmod, vr, kkn, bvec, lw, s0)
    vt_all = jnp.concatenate([jnp.swapaxes(v_all, 1, 2), jnp.ones((bsz, 16, v_all.shape[1]), F32)], axis=1)
    o_at = _attention(q, k_all.astype(BF16), vt_all.astype(BF16), tq)
    x1 = _out0(flat(x), gt, *[flat(a) for a in (o_rwf, o_rwb, r, kmod, vr, rg, o_at, ag)], w0, tm)

    sc, sh, gt = mods[1]
    gq, gk, gv, glw, gg, dqkv, small, dg = [unflat(a) for a in _in1(x1, sc, sh, w1, tm)]
    dq, dk, dv, gcol = _gdnprep(dqkv, small, w1, ts)
    if latent:
        sgf, sgb, sdf, sdb = ctx[4:]
        s0_gla = jnp.stack([_gla_state_to_bd(sgf), _gla_state_to_bd(sgb)], axis=1)
        s0_gdn = jnp.stack([sdf, sdb], axis=1)
    else:
        s0_gla = jnp.zeros((bsz, 2, 4 * HEAD128, 4 * HEAD64), F32)
        s0_gdn = jnp.zeros((bsz, 2, 4, HEAD128, HEAD128), F32)
    gla_f, gla_b, s_gla = _glascan(gq, gk, gv, glw, s0_gla)
    gdn_f, gdn_b, s_gdn = _gdnscan(dq, dk, dv, gcol, s0_gdn)
    y = unflat(_out1(x1, gt, *[flat(a) for a in (gla_f, gla_b, gdn_f, gdn_b, gg, dg)], w1, tm))

    new = None
    if not latent:
        new = (_rw_state_from_bd(s_rw[:, 0]), _rw_state_from_bd(s_rw[:, 1]),
               k.reshape(bsz, t, 2, HEAD64), v.reshape(bsz, t, 2, HEAD64),
               _gla_state_from_bd(s_gla[:, 0]), _gla_state_from_bd(s_gla[:, 1]), s_gdn[:, 0], s_gdn[:, 1])
    return y, new


def _split_mod(m, d):
    shift, scale, gate = m[..., :d], m[..., d:2 * d], m[..., 2 * d:]
    return (1.0 + scale)[:, None, :], shift[:, None, :], gate[:, None, :]


def kernel(x_prompt, x_sample, state_l0_rwkv_fwd, state_l0_rwkv_bwd, cache_l0_k, cache_l0_v, state_l1_gla_fwd, state_l1_gla_bwd, state_l1_gdn_fwd, state_l1_gdn_bwd, c, c_ctx, l0_mod_w, l0_mod_b, l0_g_pre, l0_g_post, l0_w_in, l0_w_out, l0_rw_mu, l0_rw_w0_f, l0_rw_w2_f, l0_rw_w0_b, l0_rw_w2_b, l0_rw_a0, l0_rw_a2, l0_rw_k_k, l0_rw_k_a, l0_rw_r_k, l0_rw_gn_g, l0_rw_gn_b, l0_at_gq, l0_at_gk, l1_mod_w, l1_mod_b, l1_g_pre, l1_g_post, l1_w_in, l1_w_out, l1_gla_g2_f, l1_gla_gb_f, l1_gla_g2_b, l1_gla_gb_b, l1_gla_norm_g, l1_gdn_conv, l1_gdn_A_log_f, l1_gdn_dt_bias_f, l1_gdn_A_log_b, l1_gdn_dt_bias_b, l1_gdn_norm_g):
    p0 = {"g_pre": l0_g_pre, "g_post": l0_g_post, "w_in": l0_w_in, "w_out": l0_w_out, "rw_mu": l0_rw_mu,
          "rw_w0_f": l0_rw_w0_f, "rw_w2_f": l0_rw_w2_f, "rw_w0_b": l0_rw_w0_b, "rw_w2_b": l0_rw_w2_b,
          "rw_a0": l0_rw_a0, "rw_a2": l0_rw_a2, "rw_k_k": l0_rw_k_k, "rw_k_a": l0_rw_k_a,
          "rw_r_k": l0_rw_r_k, "rw_gn_g": l0_rw_gn_g, "rw_gn_b": l0_rw_gn_b,
          "at_gq": l0_at_gq, "at_gk": l0_at_gk}
    p1 = {"g_pre": l1_g_pre, "g_post": l1_g_post, "w_in": l1_w_in, "w_out": l1_w_out,
          "gla_g2_f": l1_gla_g2_f, "gla_gb_f": l1_gla_gb_f, "gla_g2_b": l1_gla_g2_b,
          "gla_gb_b": l1_gla_gb_b, "gla_norm_g": l1_gla_norm_g, "gdn_conv": l1_gdn_conv,
          "gdn_A_log_f": l1_gdn_A_log_f, "gdn_dt_bias_f": l1_gdn_dt_bias_f,
          "gdn_A_log_b": l1_gdn_A_log_b, "gdn_dt_bias_b": l1_gdn_dt_bias_b,
          "gdn_norm_g": l1_gdn_norm_g}
    d = x_prompt.shape[-1]
    nb = c.shape[0]
    w0 = _layer0_weights(p0)
    w1 = _layer1_weights(p1)
    cos, sin = _rope_tables(x_sample.shape[1])
    w0["cos"], w0["sin"] = cos, sin

    cvec = jnp.concatenate([c, c_ctx[None, :], jnp.zeros((SUBLANES - nb - 1, d), F32)], axis=0)
    m0 = _modulation(cvec, l0_mod_w, l0_mod_b)
    m1 = _modulation(cvec, l1_mod_w, l1_mod_b)
    mods_lat = [_split_mod(m[:nb], d) for m in (m0, m1)]
    mods_ctx = [_split_mod(m[nb:nb + 1], d) for m in (m0, m1)]

    tm = 512
    y_prompt, new = _trunk(x_prompt, mods_ctx, w0, w1, None, tm, min(512, x_prompt.shape[1]))
    ctx = (state_l0_rwkv_fwd, state_l0_rwkv_bwd, cache_l0_k, cache_l0_v,
           state_l1_gla_fwd, state_l1_gla_bwd, state_l1_gdn_fwd, state_l1_gdn_bwd)
    y_sample, _ = _trunk(x_sample, mods_lat, w0, w1, ctx, tm, min(512, x_sample.shape[1]))
    return (y_prompt, y_sample) + tuple(new)
```
